```python
import math
import jax, jax.numpy as jnp
from jax import lax
import numpy as np

D_MODEL = 1024
BATCH = 16
SEQ = 256
DEPTH = 4
DEC_BATCH = 4
DEC_SEQ = 4096
PAST_LEN = 512

GRID_W = 64
BLOCK = 128
A_HEADS = 8
A_KV_HEADS = 2
A_HEAD_DIM = 64
WINDOW = 128
B_HEADS = 4
B_DK = 128
B_DV = 128
CHUNK = 64
SHORT_CONV = 3
C_HEADS = 8
C_Q_LORA = 384
C_KV_LORA = 256
C_NOPE = 128
C_ROPE = 64
C_V = 128
D_FF = 2816
FFN_CONV = 3
ROPE_BASE = 10000.0
EPS = 1e-6
N_MOD = 6

A_Q = A_HEADS * A_HEAD_DIM
A_KV = A_KV_HEADS * A_HEAD_DIM
B_QK = B_HEADS * B_DK
B_VW = B_HEADS * B_DV
B_CONV_CH = 2 * B_QK + B_VW
EVEN_IN = A_Q + 2 * A_KV + B_CONV_CH + B_VW + 4 * B_HEADS
EVEN_OUT = A_Q + B_VW
ODD_IN = C_Q_LORA + C_KV_LORA + C_ROPE

kernel_name = "hybrid_diffusion_prefix_trunk_step"


def rmsnorm(x, g):
    xf = x.astype(jnp.float32)
    y = xf * lax.rsqrt(jnp.mean(xf * xf, axis=-1, keepdims=True) + EPS)
    return (y * g.astype(jnp.float32)).astype(x.dtype)


def l2norm(x):
    xf = x.astype(jnp.float32)
    return (xf * lax.rsqrt(jnp.sum(xf * xf, axis=-1, keepdims=True) + EPS)).astype(x.dtype)


def dwconv_centred(x, w):
    k = w.shape[0]
    return lax.conv_general_dilated(
        x, w[:, None, :].astype(x.dtype), window_strides=(1,), padding=[(k // 2, k // 2)],
        dimension_numbers=("NWC", "WIO", "NWC"), feature_group_count=x.shape[-1])


def axial_rope_tables(seq_len, rot_dim):
    rows = seq_len // GRID_W
    t = jnp.arange(rows * GRID_W)
    row = (t // GRID_W).astype(jnp.float32)
    col = (t % GRID_W).astype(jnp.float32)
    n_freq = rot_dim // 4
    inv_freq = ROPE_BASE ** (-jnp.arange(n_freq, dtype=jnp.float32) / n_freq)
    ang = jnp.concatenate([row[:, None] * inv_freq, col[:, None] * inv_freq], axis=-1)
    return jnp.cos(ang), jnp.sin(ang)


def apply_rope(x, cos, sin):
    half = x.shape[-1] // 2
    shape = (1, x.shape[1]) + (1,) * (x.ndim - 3) + (half,)
    c = cos.reshape(shape).astype(x.dtype)
    s = sin.reshape(shape).astype(x.dtype)
    x1, x2 = x[..., :half], x[..., half:]
    return jnp.concatenate([x1 * c - x2 * s, x2 * c + x1 * s], axis=-1)


def softmax_sink(s, sink, groups, rep):
    if sink is None:
        return jax.nn.softmax(s, axis=-1)
    sk = sink.astype(jnp.float32).reshape(groups, rep, 1, 1)
    m = jnp.maximum(jnp.max(s, axis=-1, keepdims=True), sk)
    e = jnp.exp(s - m)
    return e / (jnp.sum(e, axis=-1, keepdims=True) + jnp.exp(sk - m))


def dense_attention(q, k, v, sink, scale):
    bsz, lq, n_heads, hd = q.shape
    groups = k.shape[2]
    rep = n_heads // groups
    n_blk = lq // BLOCK
    qb = jnp.moveaxis(q.reshape(bsz, n_blk, BLOCK, groups, rep, hd), 1, 0)

    def block(q_i):
        s = jnp.einsum("bqgrd,bkgd->bgrqk", q_i, k, preferred_element_type=jnp.float32) * scale
        p = softmax_sink(s, sink, groups, rep).astype(v.dtype)
        return jnp.einsum("bgrqk,bkgv->bqgrv", p, v)

    o = lax.map(block, qb)
    return jnp.moveaxis(o, 0, 1).reshape(bsz, lq, n_heads, v.shape[-1])


def window_ctx_attention(q, k, v, k_ctx, v_ctx, sink, scale):
    bsz, seq, n_heads, hd = q.shape
    groups = k.shape[2]
    rep = n_heads // groups
    n_blk = seq // BLOCK
    band = BLOCK + 2 * WINDOW
    n_ctx = k_ctx.shape[1]
    pad = ((0, 0), (WINDOW, WINDOW), (0, 0), (0, 0))
    kp, vp = jnp.pad(k, pad), jnp.pad(v, pad)
    qb = jnp.moveaxis(q.reshape(bsz, n_blk, BLOCK, groups, rep, hd), 1, 0)
    offs_q = jnp.arange(BLOCK)
    offs_k = jnp.arange(band)

    def block(args):
        i, q_i = args
        start = i * BLOCK
        k_i = lax.dynamic_slice_in_dim(kp, start, band, axis=1)
        v_i = lax.dynamic_slice_in_dim(vp, start, band, axis=1)
        qpos = start + offs_q
        kpos = start - WINDOW + offs_k
        valid = ((jnp.abs(qpos[:, None] - kpos[None, :]) <= WINDOW)
                 & (kpos >= 0)[None, :] & (kpos < seq)[None, :])
        s_ctx = jnp.einsum("bqgrd,bkgd->bgrqk", q_i, k_ctx, preferred_element_type=jnp.float32) * scale
        s_loc = jnp.einsum("bqgrd,bkgd->bgrqk", q_i, k_i, preferred_element_type=jnp.float32) * scale
        s_loc = jnp.where(valid, s_loc, -jnp.inf)
        p = softmax_sink(jnp.concatenate([s_ctx, s_loc], axis=-1), sink, groups, rep).astype(v.dtype)
        return (jnp.einsum("bgrqk,bkgv->bqgrv", p[..., :n_ctx], v_ctx)
                + jnp.einsum("bgrqk,bkgv->bqgrv", p[..., n_ctx:], v_i))

    o = lax.map(block, (jnp.arange(n_blk), qb))
    return jnp.moveaxis(o, 0, 1).reshape(bsz, seq, n_heads, v.shape[-1])


def gated_delta_chunked(q, k, v, g, beta, s0):
    bsz, seq, n_heads, dk = q.shape
    dv = v.shape[-1]
    n = seq // CHUNK
    f32 = jnp.float32

    def ch4(t):
        return t.astype(f32).reshape(bsz, n, CHUNK, n_heads, t.shape[-1]).transpose(1, 0, 3, 2, 4)

    def ch3(t):
        return t.astype(f32).reshape(bsz, n, CHUNK, n_heads).transpose(1, 0, 3, 2)

    qc = ch4(q) * (dk ** -0.5)
    kc, vc = ch4(k), ch4(v)
    gc = jnp.cumsum(ch3(g), axis=-1)
    bc = ch3(beta)
    idx = jnp.arange(CHUNK)
    causal = idx[:, None] >= idx[None, :]
    strict = idx[:, None] > idx[None, :]
    diff = gc[..., :, None] - gc[..., None, :]
    decay = jnp.where(causal, jnp.exp(jnp.where(causal, diff, 0.0)), 0.0)
    kb = kc * bc[..., None]
    a = jnp.where(strict, jnp.einsum("nbhid,nbhjd->nbhij", kb, kc) * decay, 0.0)
    eye = jnp.eye(CHUNK, dtype=f32)
    t_inv = lax.linalg.triangular_solve(a + eye, jnp.broadcast_to(eye, a.shape), left_side=True,
                                        lower=True, unit_diagonal=True)
    u = t_inv @ (vc * bc[..., None])
    w = t_inv @ (kb * jnp.exp(gc)[..., None])
    qk = jnp.where(causal, jnp.einsum("nbhid,nbhjd->nbhij", qc, kc) * decay, 0.0)

    def step(s, xs):
        q_i, k_i, u_i, w_i, qk_i, g_i = xs
        v_new = u_i - w_i @ s
        o_i = (q_i * jnp.exp(g_i)[..., None]) @ s + qk_i @ v_new
        g_last = g_i[..., -1:]
        s = (s * jnp.exp(g_last)[..., None]
             + jnp.einsum("bhcd,bhce->bhde", k_i * jnp.exp(g_last - g_i)[..., None], v_new))
        return s, o_i

    s, o = lax.scan(step, s0.astype(f32), (qc, kc, u, w, qk, gc))
    o = o.transpose(1, 0, 3, 2, 4).reshape(bsz, seq, n_heads, dv)
    return o.astype(v.dtype), s


def delta_bidir(q, k, v, g, beta, s0_f, s0_b):
    o_f, s_f = gated_delta_chunked(q, k, v, g[:, :, 0], beta[:, :, 0], s0_f)
    rev = lambda t: jnp.flip(t, axis=1)
    o_b, s_b = gated_delta_chunked(rev(q), rev(k), rev(v), rev(g[:, :, 1]), rev(beta[:, :, 1]), s0_b)
    return o_f + rev(o_b), s_f, s_b


def even_mixer(h, w_in, conv_w, a_log, dt_bias, sink, out_norm, w_out, ctx):
    bsz, seq, _ = h.shape
    cuts = np.cumsum([A_Q, A_KV, A_KV, B_CONV_CH, B_VW]).tolist()
    qa, ka, va, qkv_b, gate_b, ab = jnp.split(h @ w_in, cuts, axis=-1)
    qa = qa.reshape(bsz, seq, A_HEADS, A_HEAD_DIM)
    ka = ka.reshape(bsz, seq, A_KV_HEADS, A_HEAD_DIM)
    va = va.reshape(bsz, seq, A_KV_HEADS, A_HEAD_DIM)
    qkv_b = jax.nn.silu(dwconv_centred(qkv_b, conv_w))
    qb, kb, vb = jnp.split(qkv_b, [B_QK, 2 * B_QK], axis=-1)
    qb = l2norm(qb.reshape(bsz, seq, B_HEADS, B_DK))
    kb = l2norm(kb.reshape(bsz, seq, B_HEADS, B_DK))
    vb = vb.reshape(bsz, seq, B_HEADS, B_DV)
    ab = ab.astype(jnp.float32).reshape(bsz, seq, 2, 2, B_HEADS)
    g = -jnp.exp(a_log.astype(jnp.float32)) * jax.nn.softplus(ab[:, :, :, 0] + dt_bias.astype(jnp.float32))
    beta = jax.nn.sigmoid(ab[:, :, :, 1])
    scale = A_HEAD_DIM ** -0.5
    if ctx is None:
        oa = dense_attention(qa, ka, va, sink, scale)
        s0 = jnp.zeros((bsz, B_HEADS, B_DK, B_DV), jnp.float32)
        ob, s_f, s_b = delta_bidir(qb, kb, vb, g, beta, s0, s0)
        new_ctx = (ka, va, s_f.astype(h.dtype), s_b.astype(h.dtype))
    else:
        k_ctx, v_ctx, s0_f, s0_b = ctx
        cos, sin = axial_rope_tables(seq, A_HEAD_DIM)
        oa = window_ctx_attention(apply_rope(qa, cos, sin), apply_rope(ka, cos, sin), va,
                                  k_ctx, v_ctx, sink, scale)
        ob, _, _ = delta_bidir(qb, kb, vb, g, beta, s0_f, s0_b)
        new_ctx = None
    ob = rmsnorm(ob, out_norm) * jax.nn.silu(gate_b.reshape(bsz, seq, B_HEADS, B_DV))
    out = jnp.concatenate([oa.reshape(bsz, seq, A_Q), ob.reshape(bsz, seq, B_VW)], axis=-1) @ w_out
    return out, new_ctx


def odd_mixer(h, w_in, q_norm, kv_norm, w_q_up, w_kv_up, w_out, ctx):
    bsz, seq, _ = h.shape
    cq, ckv, kpe = jnp.split(h @ w_in, [C_Q_LORA, C_Q_LORA + C_KV_LORA], axis=-1)
    q = (rmsnorm(cq, q_norm) @ w_q_up).reshape(bsz, seq, C_HEADS, C_NOPE + C_ROPE)
    q_nope, q_pe = q[..., :C_NOPE], q[..., C_NOPE:]
    ckv = rmsnorm(ckv, kv_norm)
    if ctx is None:
        ckv_all, kpe_all = ckv, kpe
        new_ctx = (ckv, kpe)
    else:
        ckv_ctx, kpe_ctx = ctx
        cos, sin = axial_rope_tables(seq, C_ROPE)
        q_pe = apply_rope(q_pe, cos, sin)
        ckv_all = jnp.concatenate([ckv_ctx, ckv], axis=1)
        kpe_all = jnp.concatenate([kpe_ctx, apply_rope(kpe, cos, sin)], axis=1)
        new_ctx = None
    n_keys = ckv_all.shape[1]
    kv = (ckv_all @ w_kv_up).reshape(bsz, n_keys, C_HEADS, C_NOPE + C_V)
    k = jnp.concatenate([kv[..., :C_NOPE],
                         jnp.broadcast_to(kpe_all[:, :, None, :], (bsz, n_keys, C_HEADS, C_ROPE))], axis=-1)
    v = kv[..., C_NOPE:]
    o = dense_attention(jnp.concatenate([q_nope, q_pe], axis=-1), k, v, None, (C_NOPE + C_ROPE) ** -0.5)
    return o.reshape(bsz, seq, C_HEADS * C_V) @ w_out, new_ctx


def conv_ffn(h, w_up, conv_w, w_down):
    u = dwconv_centred(h @ w_up, conv_w)
    a, b = jnp.split(u, 2, axis=-1)
    return (jax.nn.silu(a) * b) @ w_down


def trunk_layer(l, x, cvec, prm, ctx):
    mod = jax.nn.silu(cvec) @ prm["w_mod"][l] + prm["b_mod"][l]
    shift1, scale1, gate1, shift2, scale2, gate2 = jnp.split(mod, N_MOD, axis=-1)
    h = rmsnorm(x, prm["norm_pre"][l, 0]) * (1.0 + scale1) + shift1
    j = l // 2
    if l % 2 == 0:
        mix, new_ctx = even_mixer(h, prm["ev_w_in"][j], prm["ev_conv"][j], prm["ev_a_log"][j],
                                  prm["ev_dt_bias"][j], prm["ev_sink"][j], prm["ev_out_norm"][j],
                                  prm["ev_w_out"][j], ctx)
    else:
        mix, new_ctx = odd_mixer(h, prm["od_w_in"][j], prm["od_q_norm"][j], prm["od_kv_norm"][j],
                                 prm["od_w_q_up"][j], prm["od_w_kv_up"][j], prm["od_w_out"][j], ctx)
    x = x + gate1 * rmsnorm(mix, prm["norm_post"][l, 0])
    h = rmsnorm(x, prm["norm_pre"][l, 1]) * (1.0 + scale2) + shift2
    f = conv_ffn(h, prm["ffn_w_up"][l], prm["ffn_conv"][l], prm["ffn_w_down"][l])
    x = x + gate2 * rmsnorm(f, prm["norm_post"][l, 1])
    return x, new_ctx


def setup_inputs(seed: int = 0) -> dict:
    key = jax.random.key(seed)
    ks = jax.random.split(key, 32)
    f32 = jnp.float32
    ne, no = (DEPTH + 1) // 2, DEPTH // 2

    def nrm(i, shape, scale=1.0):
        return scale * jax.random.normal(ks[i], shape, f32)

    def gain(i, shape):
        return 1.0 + 0.05 * jax.random.normal(ks[i], shape, f32)

    dt = jnp.exp(jax.random.uniform(ks[20], (ne, 2, B_HEADS), f32, math.log(1e-3), math.log(0.1)))
    dt_bias = dt + jnp.log(-jnp.expm1(-dt))
    a_log = jnp.log(jax.random.uniform(ks[19], (ne, 2, B_HEADS), f32, 1.0, 16.0))
    return {
        "x_prompt": nrm(0, (BATCH, SEQ, D_MODEL)),
        "x_sample": nrm(1, (DEC_BATCH, DEC_SEQ, D_MODEL)),
        "cache_attn_k": nrm(2, (DEC_BATCH, ne, PAST_LEN, A_KV_HEADS, A_HEAD_DIM)),
        "cache_attn_v": nrm(3, (DEC_BATCH, ne, PAST_LEN, A_KV_HEADS, A_HEAD_DIM)),
        "state_delta_fwd": nrm(4, (DEC_BATCH, ne, B_HEADS, B_DK, B_DV), B_DK ** -0.5),
        "state_delta_bwd": nrm(5, (DEC_BATCH, ne, B_HEADS, B_DK, B_DV), B_DK ** -0.5),
        "cache_mla_ckv": nrm(6, (DEC_BATCH, no, PAST_LEN, C_KV_LORA)),
        "cache_mla_kpe": nrm(7, (DEC_BATCH, no, PAST_LEN, C_ROPE)),
        "c": nrm(8, (DEC_BATCH, D_MODEL)),
        "c_ctx": nrm(9, (D_MODEL,)),
        "w_mod": nrm(10, (DEPTH, D_MODEL, N_MOD * D_MODEL), 0.5 * D_MODEL ** -0.5),
        "b_mod": nrm(11, (DEPTH, N_MOD * D_MODEL), 0.02),
        "norm_pre": gain(12, (DEPTH, 2, D_MODEL)),
        "norm_post": gain(13, (DEPTH, 2, D_MODEL)),
        "ffn_w_up": nrm(14, (DEPTH, D_MODEL, 2 * D_FF), D_MODEL ** -0.5),
        "ffn_conv": nrm(15, (DEPTH, FFN_CONV, 2 * D_FF), FFN_CONV ** -0.5),
        "ffn_w_down": nrm(16, (DEPTH, D_FF, D_MODEL), D_FF ** -0.5),
        "ev_w_in": nrm(17, (ne, D_MODEL, EVEN_IN), D_MODEL ** -0.5),
        "ev_conv": nrm(18, (ne, SHORT_CONV, B_CONV_CH), SHORT_CONV ** -0.5),
        "ev_a_log": a_log,
        "ev_dt_bias": dt_bias,
        "ev_sink": nrm(21, (ne, A_HEADS), 0.5),
        "ev_out_norm": gain(22, (ne, B_DV)),
        "ev_w_out": nrm(23, (ne, EVEN_OUT, D_MODEL), EVEN_OUT ** -0.5),
        "od_w_in": nrm(24, (no, D_MODEL, ODD_IN), D_MODEL ** -0.5),
        "od_q_norm": gain(25, (no, C_Q_LORA)),
        "od_kv_norm": gain(26, (no, C_KV_LORA)),
        "od_w_q_up": nrm(27, (no, C_Q_LORA, C_HEADS * (C_NOPE + C_ROPE)), C_Q_LORA ** -0.5),
        "od_w_kv_up": nrm(28, (no, C_KV_LORA, C_HEADS * (C_NOPE + C_V)), C_KV_LORA ** -0.5),
        "od_w_out": nrm(29, (no, C_HEADS * C_V, D_MODEL), (C_HEADS * C_V) ** -0.5),
    }


def reference(x_prompt, x_sample, cache_attn_k, cache_attn_v, state_delta_fwd, state_delta_bwd,
              cache_mla_ckv, cache_mla_kpe, c, c_ctx, w_mod, b_mod, norm_pre, norm_post,
              ffn_w_up, ffn_conv, ffn_w_down, ev_w_in, ev_conv, ev_a_log, ev_dt_bias, ev_sink,
              ev_out_norm, ev_w_out, od_w_in, od_q_norm, od_kv_norm, od_w_q_up, od_w_kv_up, od_w_out):
    prm = {
        "w_mod": w_mod, "b_mod": b_mod, "norm_pre": norm_pre, "norm_post": norm_post,
        "ffn_w_up": ffn_w_up, "ffn_conv": ffn_conv, "ffn_w_down": ffn_w_down,
        "ev_w_in": ev_w_in, "ev_conv": ev_conv, "ev_a_log": ev_a_log, "ev_dt_bias": ev_dt_bias,
        "ev_sink": ev_sink, "ev_out_norm": ev_out_norm, "ev_w_out": ev_w_out,
        "od_w_in": od_w_in, "od_q_norm": od_q_norm, "od_kv_norm": od_kv_norm,
        "od_w_q_up": od_w_q_up, "od_w_kv_up": od_w_kv_up, "od_w_out": od_w_out,
    }
    xp = x_prompt
    c_ctx_b = c_ctx[None, None, :]
    attn_k, attn_v, d_fwd, d_bwd, mla_ckv, mla_kpe = [], [], [], [], [], []
    for l in range(DEPTH):
        xp, ctx = trunk_layer(l, xp, c_ctx_b, prm, None)
        if l % 2 == 0:
            attn_k.append(ctx[0])
            attn_v.append(ctx[1])
            d_fwd.append(ctx[2])
            d_bwd.append(ctx[3])
        else:
            mla_ckv.append(ctx[0])
            mla_kpe.append(ctx[1])
    y_prompt = xp
    xs = x_sample
    c_b = c[:, None, :]
    for l in range(DEPTH):
        j = l // 2
        if l % 2 == 0:
            ctx = (cache_attn_k[:, j], cache_attn_v[:, j], state_delta_fwd[:, j], state_delta_bwd[:, j])
        else:
            ctx = (cache_mla_ckv[:, j], cache_mla_kpe[:, j])
        xs, _ = trunk_layer(l, xs, c_b, prm, ctx)
    y_sample = xs
    new_attn_k = jnp.stack(attn_k, axis=1)
    new_attn_v = jnp.stack(attn_v, axis=1)
    new_delta_fwd = jnp.stack(d_fwd, axis=1)
    new_delta_bwd = jnp.stack(d_bwd, axis=1)
    new_mla_ckv = jnp.stack(mla_ckv, axis=1)
    new_mla_kpe = jnp.stack(mla_kpe, axis=1)
    return (y_prompt, y_sample, new_attn_k, new_attn_v, new_delta_fwd, new_delta_bwd, new_mla_ckv, new_mla_kpe)
```

```python
import functools
import math

import jax
import jax.numpy as jnp
from jax import lax
from jax.experimental import pallas as pl
from jax.experimental.pallas import tpu as pltpu

F32 = jnp.float32
BF16 = jnp.bfloat16

D_MODEL = 1024
GRID_W = 64
A_HEADS = 8
A_KV_HEADS = 2
A_HEAD_DIM = 64
A_REP = A_HEADS // A_KV_HEADS
WINDOW = 128
Q_BLOCK = 128
B_HEADS = 4
B_DK = 128
B_DV = 128
CHUNK = 64
C_HEADS = 8
C_Q_LORA = 384
C_KV_LORA = 256
C_NOPE = 128
C_ROPE = 64
C_V = 128
C_QK_PAD = 256
D_FF = 2816
ROPE_BASE = 10000.0
EPS = 1e-6
N_MOD = 6

A_Q = A_HEADS * A_HEAD_DIM
A_KV = A_KV_HEADS * A_HEAD_DIM
B_QK = B_HEADS * B_DK
B_VW = B_HEADS * B_DV
B_CONV_CH = 2 * B_QK + B_VW
KA0 = A_Q
VA0 = KA0 + A_KV
QKVB0 = VA0 + A_KV
GATE0 = QKVB0 + B_CONV_CH
AB0 = GATE0 + B_VW
LANES = 128
SUBLANES = 8
EVEN_IN_PAD = AB0 + LANES
ODD_IN_PAD = C_Q_LORA + C_KV_LORA + LANES
FF_TILE = 256
VMEM_LIMIT_BYTES = 56 * 1024 * 1024


def _params(*sem):
    return pltpu.CompilerParams(dimension_semantics=sem, vmem_limit_bytes=VMEM_LIMIT_BYTES)


def _dot(a, b):
    return jnp.dot(a, b, preferred_element_type=F32)


def _dot_nt(a, b):
    return lax.dot_general(a, b, (((1,), (1,)), ((), ())), preferred_element_type=F32)


def _hdot(a, b):
    return jnp.dot(a, b, preferred_element_type=F32, precision=lax.Precision.HIGHEST)


def _sigmoid(x):
    return 1.0 / (1.0 + jnp.exp(-x))


def _silu(x):
    return x * _sigmoid(x)


def _softplus(x):
    return jnp.maximum(x, 0.0) + jnp.log(1.0 + jnp.exp(-jnp.abs(x)))


def _rms(x, g):
    return x * lax.rsqrt(jnp.mean(x * x, axis=-1, keepdims=True) + EPS) * g


def _norm_mod(x, g, scale, shift):
    return _rms(x, g) * (1.0 + scale) + shift


def _rope(x, cos_t, sin_t):
    lane = lax.broadcasted_iota(jnp.int32, x.shape, 1)
    swapped = jnp.where((lane & 63) < 32, pltpu.roll(x, LANES - 32, 1), pltpu.roll(x, 32, 1))
    return x * cos_t + swapped * sin_t


def _dwconv3(u, up, dn, cw):
    tm = u.shape[0]
    row = lax.broadcasted_iota(jnp.int32, u.shape, 0)
    u_prev = jnp.where(row == 0, up, pltpu.roll(u, 1, 0))
    u_next = jnp.where(row == tm - 1, dn, pltpu.roll(u, tm - 1, 0))
    return cw[0:1] * u_prev + cw[1:2] * u + cw[2:3] * u_next


def _halo_specs(tm, seq):
    per = tm // SUBLANES
    last = seq // SUBLANES - 1
    prev = pl.BlockSpec((1, SUBLANES, D_MODEL), lambda b, i, *_: (b, jnp.maximum(i * per - 1, 0), 0))
    nxt = pl.BlockSpec((1, SUBLANES, D_MODEL), lambda b, i, *_: (b, jnp.minimum((i + 1) * per, last), 0))
    return prev, nxt


def _mod_spec(mod_rows, layer, row_of, k):
    return pl.BlockSpec((1, 1, D_MODEL),
                        lambda b, *_: ((layer * mod_rows + row_of(b)) * N_MOD + k, 0, 0))


def _const_spec(shape):
    nd = len(shape)
    return pl.BlockSpec(shape, lambda *_: (0,) * nd)


def _mod_body(c_ref, w_ref, b_ref, o_ref):
    s = _silu(c_ref[...]).astype(BF16)
    o_ref[0] = _dot(s, w_ref[0].astype(BF16)) + b_ref[0]


def _mod_all(cvec, w_mod, b_mod):
    depth, d, n = w_mod.shape
    rows = cvec.shape[0]
    tn = n // 4
    return pl.pallas_call(
        _mod_body, grid=(depth, n // tn),
        in_specs=[pl.BlockSpec((rows, d), lambda l, j: (0, 0)),
                  pl.BlockSpec((1, d, tn), lambda l, j: (l, 0, j)),
                  pl.BlockSpec((1, 1, tn), lambda l, j: (l, 0, j))],
        out_specs=pl.BlockSpec((1, rows, tn), lambda l, j: (l, 0, j)),
        out_shape=jax.ShapeDtypeStruct((depth, rows, n), F32),
        name="mod_vectors", compiler_params=_params("parallel", "parallel"),
    )(cvec, w_mod, b_mod.reshape(depth, 1, n))


def _even_in_body(rope, n_tiles, x_ref, xp_ref, xn_ref, sc_ref, sh_ref, g_ref, w_ref, cw_ref,
                  alog_ref, dt_ref, *rest):
    if rope:
        cos_ref, sin_ref = rest[:2]
        rest = rest[2:]
    qa_ref, ka_ref, va_ref, qkv_ref, gate_ref, gb_ref, gbt_ref = rest
    i = pl.program_id(1)
    tm = x_ref.shape[1]
    g, sc, sh = g_ref[...], sc_ref[0], sh_ref[0]
    h = _norm_mod(x_ref[0], g, sc, sh).astype(BF16)
    y = _dot(h, w_ref[...])
    xh = jnp.concatenate([xp_ref[0], xn_ref[0]], axis=0)
    hh = _norm_mod(xh, g, sc, sh).astype(BF16)
    yh = _dot(hh, w_ref[:, QKVB0:GATE0])
    up = jnp.where(i > 0, yh[SUBLANES - 1:SUBLANES], 0.0)
    dn = jnp.where(i < n_tiles - 1, yh[SUBLANES:SUBLANES + 1], 0.0)

    qa = y[:, :A_Q]
    ka = y[:, KA0:VA0]
    if rope:
        cos_t, sin_t = cos_ref[...], sin_ref[...]
        qa = jnp.concatenate([_rope(qa[:, j * LANES:(j + 1) * LANES], cos_t, sin_t)
                              for j in range(A_Q // LANES)], axis=1)
        ka = _rope(ka, cos_t, sin_t)
    qa_ref[0] = qa
    ka_ref[0] = ka
    va_ref[0] = y[:, VA0:QKVB0]
    gate_ref[0] = y[:, GATE0:AB0]

    s = _silu(_dwconv3(y[:, QKVB0:GATE0], up, dn, cw_ref[...]))
    for j in range(B_CONV_CH // LANES):
        seg = s[:, j * LANES:(j + 1) * LANES]
        if j < 2 * B_HEADS:
            seg = seg * lax.rsqrt(jnp.sum(seg * seg, axis=-1, keepdims=True) + EPS)
        qkv_ref[0, :, j * LANES:(j + 1) * LANES] = seg

    ab = y[:, AB0:EVEN_IN_PAD]
    lane = lax.broadcasted_iota(jnp.int32, ab.shape, 1)
    gdec = -jnp.exp(alog_ref[...]) * _softplus(ab + dt_ref[...])
    gb = jnp.where((lane & B_HEADS) == 0, gdec, _sigmoid(ab))
    gb_ref[0] = gb
    gbt = gb.T
    for c in range(tm // CHUNK):
        gbt_ref[0, c] = gbt[:4 * B_HEADS, c * CHUNK:(c + 1) * CHUNK]


def _even_in(x, modflat, mod_rows, row_of, layer, g_pre, w_in, conv_w, alog_vec, dt_vec, rope_tabs):
    bsz, seq, d = x.shape
    tm = 512 if seq % 512 == 0 else 256
    nt = seq // tm
    rope = rope_tabs is not None
    prev, nxt = _halo_specs(tm, seq)
    in_specs = [pl.BlockSpec((1, tm, d), lambda b, i: (b, i, 0)), prev, nxt,
                _mod_spec(mod_rows, layer, row_of, 1), _mod_spec(mod_rows, layer, row_of, 0),
                _const_spec((1, d)), _const_spec(w_in.shape), _const_spec(conv_w.shape),
                _const_spec((1, LANES)), _const_spec((1, LANES))]
    args = [x, x, x, modflat, modflat, g_pre, w_in, conv_w, alog_vec, dt_vec]
    if rope:
        in_specs += [pl.BlockSpec((tm, LANES), lambda b, i: (i, 0))] * 2
        args += list(rope_tabs)

    def out(width, dtype=F32):
        return (pl.BlockSpec((1, tm, width), lambda b, i: (b, i, 0)),
                jax.ShapeDtypeStruct((bsz, seq, width), dtype))

    outs = [out(A_Q), out(A_KV), out(A_KV), out(B_CONV_CH), out(B_VW), out(LANES),
            (pl.BlockSpec((1, tm // CHUNK, 4 * B_HEADS, CHUNK), lambda b, i: (b, i, 0, 0)),
             jax.ShapeDtypeStruct((bsz, seq // CHUNK, 4 * B_HEADS, CHUNK), F32))]
    return pl.pallas_call(
        functools.partial(_even_in_body, rope, nt), grid=(bsz, nt), in_specs=in_specs,
        out_specs=[o[0] for o in outs], out_shape=[o[1] for o in outs],
        name="even_in", compiler_params=_params("parallel", "parallel"),
    )(*args)


def _even_attn_body(windowed, q_ref, k_ref, v_ref, *rest):
    if windowed:
        kc_ref, vc_ref, sink_ref, o_ref = rest
    else:
        sink_ref, o_ref = rest
    i = pl.program_id(1)
    seq = k_ref.shape[1]
    q_all = q_ref[0] * (A_HEAD_DIM ** -0.5)
    rows = A_REP * Q_BLOCK
    if windowed:
        band = Q_BLOCK + 2 * WINDOW
        k0 = pl.multiple_of(jnp.clip(i * Q_BLOCK - WINDOW, 0, seq - band), Q_BLOCK)
        k_all = k_ref[0, pl.ds(k0, band), :].astype(BF16)
        v_all = v_ref[0, pl.ds(k0, band), :].astype(BF16)
        kc_all = kc_ref[0].astype(BF16)
        vc_all = vc_ref[0].astype(BF16)
        qpos = i * Q_BLOCK + (lax.broadcasted_iota(jnp.int32, (rows, band), 0) & (Q_BLOCK - 1))
        kpos = k0 + lax.broadcasted_iota(jnp.int32, (rows, band), 1)
        valid = jnp.abs(qpos - kpos) <= WINDOW
    else:
        k_all = k_ref[0].astype(BF16)
        v_all = v_ref[0].astype(BF16)
    rid = lax.broadcasted_iota(jnp.int32, (rows, 1), 0) // Q_BLOCK
    outs = []
    for g in range(A_KV_HEADS):
        lo, hi = g * A_HEAD_DIM, (g + 1) * A_HEAD_DIM
        heads = range(g * A_REP, (g + 1) * A_REP)
        q = jnp.concatenate([q_all[:, h * A_HEAD_DIM:(h + 1) * A_HEAD_DIM] for h in heads],
                            axis=0).astype(BF16)
        sink = jnp.zeros((rows, 1), F32)
        for r, h in enumerate(heads):
            sink = jnp.where(rid == r, sink_ref[h], sink)
        s = _dot_nt(q, k_all[:, lo:hi])
        if windowed:
            s = jnp.where(valid, s, -jnp.inf)
            s_c = _dot_nt(q, kc_all[:, lo:hi])
            m = jnp.maximum(jnp.max(s_c, axis=-1, keepdims=True), sink)
        else:
            m = sink
        m = jnp.maximum(jnp.max(s, axis=-1, keepdims=True), m)
        e = jnp.exp(s - m)
        den = jnp.sum(e, axis=-1, keepdims=True) + jnp.exp(sink - m)
        o = _dot(e.astype(BF16), v_all[:, lo:hi])
        if windowed:
            e_c = jnp.exp(s_c - m)
            den = den + jnp.sum(e_c, axis=-1, keepdims=True)
            o = o + _dot(e_c.astype(BF16), vc_all[:, lo:hi])
        o = o / den
        outs += [o[r * Q_BLOCK:(r + 1) * Q_BLOCK] for r in range(A_REP)]
    o_ref[0] = jnp.concatenate(outs, axis=1).astype(o_ref.dtype)


def _even_attn(qa, ka, va, ctx, sink):
    bsz, seq, _ = qa.shape
    windowed = ctx is not None
    in_specs = [pl.BlockSpec((1, Q_BLOCK, A_Q), lambda b, i: (b, i, 0)),
                pl.BlockSpec((1, seq, A_KV), lambda b, i: (b, 0, 0)),
                pl.BlockSpec((1, seq, A_KV), lambda b, i: (b, 0, 0))]
    args = [qa, ka, va]
    if windowed:
        past = ctx[0].shape[1]
        in_specs += [pl.BlockSpec((1, past, A_KV), lambda b, i: (b, 0, 0))] * 2
        args += list(ctx)
    in_specs.append(pl.BlockSpec(memory_space=pltpu.SMEM))
    args.append(sink)
    return pl.pallas_call(
        functools.partial(_even_attn_body, windowed), grid=(bsz, seq // Q_BLOCK), in_specs=in_specs,
        out_specs=pl.BlockSpec((1, Q_BLOCK, A_Q), lambda b, i: (b, i, 0)),
        out_shape=jax.ShapeDtypeStruct((bsz, seq, A_Q), BF16),
        name="even_attn", compiler_params=_params("parallel", "parallel"),
    )(*args)


def _unit_tri_inverse(a, eye):
    p = -a
    t = eye + p
    for _ in range(int(math.log2(CHUNK)) - 1):
        p = _hdot(p, p)
        t = t + _hdot(t, p)
    return t


def _delta_chunk(q, k, v, gcol, grow, bcol, s, incl, strict, eye, g_last):
    diff = gcol - grow
    decay = jnp.where(incl, jnp.exp(jnp.where(incl, diff, 0.0)), 0.0)
    qs = q * (B_DK ** -0.5)
    kbeta = k * bcol
    k16 = k.astype(BF16)
    a = jnp.where(strict, _dot_nt(kbeta.astype(BF16), k16) * decay, 0.0)
    t_inv = _unit_tri_inverse(a, eye)
    eg = jnp.exp(gcol)
    rhs = jnp.concatenate([v * bcol, kbeta * eg], axis=1).astype(BF16)
    uw = _dot(t_inv.astype(BF16), rhs)
    u, w = uw[:, :B_DV], uw[:, B_DV:]
    qk = jnp.where(incl, _dot_nt(qs.astype(BF16), k16) * decay, 0.0)
    s16 = s.astype(BF16)
    v_new = u - _dot(w.astype(BF16), s16)
    v16 = v_new.astype(BF16)
    o = _dot((qs * eg).astype(BF16), s16) + _dot(qk.astype(BF16), v16)
    kd = (k * jnp.exp(g_last - gcol)).T.astype(BF16)
    s_new = s * jnp.exp(g_last) + _dot(kd, v16)
    return o, s_new


def _delta_body(zero_init, qf_ref, qb_ref, gf_ref, gb_ref, gtf_ref, gtb_ref, *rest):
    if zero_init:
        of_ref, ob_ref, sfo_ref, sbo_ref, sf_scr, sb_scr = rest
    else:
        s0f_ref, s0b_ref, of_ref, ob_ref, sfo_ref, sbo_ref, sf_scr, sb_scr = rest
    n = pl.program_id(1)

    @pl.when(n == 0)
    def _():
        if zero_init:
            sf_scr[...] = jnp.zeros_like(sf_scr)
            sb_scr[...] = jnp.zeros_like(sb_scr)
        else:
            sf_scr[...] = s0f_ref[0]
            sb_scr[...] = s0b_ref[0]

    ii = lax.broadcasted_iota(jnp.int32, (CHUNK, CHUNK), 0)
    jj = lax.broadcasted_iota(jnp.int32, (CHUNK, CHUNK), 1)
    lower, upper = ii >= jj, ii <= jj
    tri_l, tri_u = lower.astype(F32), upper.astype(F32)
    eye = (ii == jj).astype(F32)
    gf, gb = gf_ref[0], gb_ref[0]
    gcf, gcb = _hdot(tri_l, gf), _hdot(tri_u, gb)
    gcf_t, gcb_t = _hdot(gtf_ref[0, 0], tri_u), _hdot(gtb_ref[0, 0], tri_l)
    for h in range(B_HEADS):
        cq = slice(h * B_DK, (h + 1) * B_DK)
        ck = slice(B_QK + h * B_DK, B_QK + (h + 1) * B_DK)
        cv = slice(2 * B_QK + h * B_DV, 2 * B_QK + (h + 1) * B_DV)
        co = slice(h * B_DV, (h + 1) * B_DV)
        lg, lb = h, B_HEADS + h
        gcol = gcf[:, lg:lg + 1]
        o, s_new = _delta_chunk(qf_ref[0, :, cq], qf_ref[0, :, ck], qf_ref[0, :, cv],
                                gcol, gcf_t[lg:lg + 1, :], gf[:, lb:lb + 1], sf_scr[h],
                                lower, ii > jj, eye, gcol[CHUNK - 1:CHUNK])
        of_ref[0, :, co] = o
        sf_scr[h] = s_new
        lg, lb = 2 * B_HEADS + h, 3 * B_HEADS + h
        gcol = gcb[:, lg:lg + 1]
        o, s_new = _delta_chunk(qb_ref[0, :, cq], qb_ref[0, :, ck], qb_ref[0, :, cv],
                                gcol, gcb_t[lg:lg + 1, :], gb[:, lb:lb + 1], sb_scr[h],
                                upper, ii < jj, eye, gcol[0:1])
        ob_ref[0, :, co] = o
        sb_scr[h] = s_new

    @pl.when(n == pl.num_programs(1) - 1)
    def _():
        sfo_ref[0] = sf_scr[...]
        sbo_ref[0] = sb_scr[...]


def _delta(qkvb, gbv, gbt, states):
    bsz, seq, _ = qkvb.shape
    nb = seq // CHUNK
    zero_init = states is None

    def fwd(*tail):
        return lambda b, n: (b, n) + tail

    def bwd(*tail):
        return lambda b, n: (b, nb - 1 - n) + tail

    st_spec = pl.BlockSpec((1, B_HEADS, B_DK, B_DV), lambda b, n: (b, 0, 0, 0))
    in_specs = [pl.BlockSpec((1, CHUNK, B_CONV_CH), fwd(0)), pl.BlockSpec((1, CHUNK, B_CONV_CH), bwd(0)),
                pl.BlockSpec((1, CHUNK, LANES), fwd(0)), pl.BlockSpec((1, CHUNK, LANES), bwd(0)),
                pl.BlockSpec((1, 1, 4 * B_HEADS, CHUNK), fwd(0, 0)),
                pl.BlockSpec((1, 1, 4 * B_HEADS, CHUNK), bwd(0, 0))]
    args = [qkvb, qkvb, gbv, gbv, gbt, gbt]
    if not zero_init:
        in_specs += [st_spec, st_spec]
        args += list(states)
    st_shape = jax.ShapeDtypeStruct((bsz, B_HEADS, B_DK, B_DV), F32)
    o_shape = jax.ShapeDtypeStruct((bsz, seq, B_VW), F32)
    return pl.pallas_call(
        functools.partial(_delta_body, zero_init), grid=(bsz, nb), in_specs=in_specs,
        out_specs=[pl.BlockSpec((1, CHUNK, B_VW), fwd(0)), pl.BlockSpec((1, CHUNK, B_VW), bwd(0)),
                   st_spec, st_spec],
        out_shape=[o_shape, o_shape, st_shape, st_shape],
        scratch_shapes=[pltpu.VMEM((B_HEADS, B_DK, B_DV), F32)] * 2,
        name="delta_rule", compiler_params=_params("parallel", "arbitrary"),
    )(*args)


def _even_out_body(oa_ref, of_ref, ob_ref, gate_ref, x_ref, g1_ref, on_ref, w_ref, gp_ref, o_ref):
    ob = of_ref[0] + ob_ref[0]
    on = on_ref[...]
    segs = [_rms(ob[:, h * B_DV:(h + 1) * B_DV], on) for h in range(B_HEADS)]
    obg = (jnp.concatenate(segs, axis=1) * _silu(gate_ref[0])).astype(BF16)
    mix = _dot(oa_ref[0], w_ref[:A_Q, :]) + _dot(obg, w_ref[A_Q:, :])
    o_ref[0] = x_ref[0] + g1_ref[0] * _rms(mix, gp_ref[...])


def _odd_out_body(a_ref, x_ref, g1_ref, w_ref, gp_ref, o_ref):
    mix = _dot(a_ref[0], w_ref[...])
    o_ref[0] = x_ref[0] + g1_ref[0] * _rms(mix, gp_ref[...])


def _mixer_out(acts, x, modflat, mod_rows, row_of, layer, w_out, g_post, out_norm=None):
    bsz, seq, d = x.shape
    tm = 512 if seq % 512 == 0 else 256

    def row_spec(width):
        return pl.BlockSpec((1, tm, width), lambda b, i: (b, i, 0))

    in_specs = [row_spec(a.shape[-1]) for a in acts] + [row_spec(d), _mod_spec(mod_rows, layer, row_of, 2)]
    args = list(acts) + [x, modflat]
    if out_norm is not None:
        in_specs.append(_const_spec(out_norm.shape))
        args.append(out_norm)
    in_specs += [_const_spec(w_out.shape), _const_spec((1, d))]
    args += [w_out, g_post]
    return pl.pallas_call(
        _even_out_body if out_norm is not None else _odd_out_body,
        grid=(bsz, seq // tm), in_specs=in_specs, out_specs=row_spec(d),
        out_shape=jax.ShapeDtypeStruct(x.shape, F32),
        name="mixer_out", compiler_params=_params("parallel", "parallel"),
    )(*args)


def _odd_in_body(rope, x_ref, sc_ref, sh_ref, g_ref, w_ref, qn_ref, kvn_ref, wq_ref, wkv_ref, *rest):
    if rope:
        cos_ref, sin_ref = rest[:2]
        rest = rest[2:]
    q_ref, ckv_ref, kpe_ref, kn_ref, vv_ref = rest
    h = _norm_mod(x_ref[0], g_ref[...], sc_ref[0], sh_ref[0]).astype(BF16)
    y = _dot(h, w_ref[...])
    cq = _rms(y[:, :C_Q_LORA], qn_ref[...]).astype(BF16)
    q = _dot(cq, wq_ref[...])
    ckv = _rms(y[:, C_Q_LORA:C_Q_LORA + C_KV_LORA], kvn_ref[...])
    kpe = y[:, C_Q_LORA + C_KV_LORA:]
    scale = (C_NOPE + C_ROPE) ** -0.5
    if rope:
        cos_t, sin_t = cos_ref[...], sin_ref[...]
        kpe = _rope(kpe, cos_t, sin_t)
    for hd in range(C_HEADS):
        base = hd * C_QK_PAD
        q_ref[0, :, base:base + C_NOPE] = (q[:, base:base + C_NOPE] * scale).astype(BF16)
        qpe = q[:, base + C_NOPE:base + C_QK_PAD]
        if rope:
            qpe = _rope(qpe, cos_t, sin_t)
        q_ref[0, :, base + C_NOPE:base + C_QK_PAD] = (qpe * scale).astype(BF16)
    ckv_ref[0] = ckv
    kpe_ref[0] = kpe
    kv = _dot(ckv.astype(BF16), wkv_ref[...])
    kn_ref[0] = kv[:, :C_HEADS * C_NOPE].astype(BF16)
    vv_ref[0] = kv[:, C_HEADS * C_NOPE:].astype(BF16)


def _odd_in(x, modflat, mod_rows, row_of, layer, g_pre, w_in, q_norm, kv_norm, w_q, w_kv, rope_tabs):
    bsz, seq, d = x.shape
    tm = 512 if seq % 512 == 0 else 256
    rope = rope_tabs is not None
    in_specs = [pl.BlockSpec((1, tm, d), lambda b, i: (b, i, 0)),
                _mod_spec(mod_rows, layer, row_of, 1), _mod_spec(mod_rows, layer, row_of, 0),
                _const_spec((1, d)), _const_spec(w_in.shape), _const_spec(q_norm.shape),
                _const_spec(kv_norm.shape), _const_spec(w_q.shape), _const_spec(w_kv.shape)]
    args = [x, modflat, modflat, g_pre, w_in, q_norm, kv_norm, w_q, w_kv]
    if rope:
        in_specs += [pl.BlockSpec((tm, LANES), lambda b, i: (i, 0))] * 2
        args += list(rope_tabs)

    def out(width, dtype):
        return (pl.BlockSpec((1, tm, width), lambda b, i: (b, i, 0)),
                jax.ShapeDtypeStruct((bsz, seq, width), dtype))

    outs = [out(C_HEADS * C_QK_PAD, BF16), out(C_KV_LORA, F32), out(LANES, F32),
            out(C_HEADS * C_NOPE, BF16), out(C_HEADS * C_V, BF16)]
    return pl.pallas_call(
        functools.partial(_odd_in_body, rope), grid=(bsz, seq // tm), in_specs=in_specs,
        out_specs=[o[0] for o in outs], out_shape=[o[1] for o in outs],
        name="odd_in", compiler_params=_params("parallel", "parallel"),
    )(*args)


def _kv_up_body(ckv_ref, wkv_ref, kn_ref, vv_ref):
    kv = _dot(ckv_ref[0].astype(BF16), wkv_ref[...])
    kn_ref[0] = kv[:, :C_HEADS * C_NOPE].astype(BF16)
    vv_ref[0] = kv[:, C_HEADS * C_NOPE:].astype(BF16)


def _kv_up(ckv, w_kv):
    bsz, n, r = ckv.shape
    width = C_HEADS * C_NOPE
    spec = pl.BlockSpec((1, n, width), lambda b: (b, 0, 0))
    shape = jax.ShapeDtypeStruct((bsz, n, width), BF16)
    return pl.pallas_call(
        _kv_up_body, grid=(bsz,),
        in_specs=[pl.BlockSpec((1, n, r), lambda b: (b, 0, 0)), _const_spec(w_kv.shape)],
        out_specs=[spec, spec], out_shape=[shape, shape],
        name="kv_up", compiler_params=_params("parallel"),
    )(ckv, w_kv)


def _mla_attn_body(has_ctx, q_ref, kn_ref, kpe_ref, vv_ref, *rest):
    if has_ctx:
        knc_ref, kpec_ref, vvc_ref, o_ref = rest
    else:
        (o_ref,) = rest
    q = q_ref[0]
    k = jnp.concatenate([kn_ref[0], kpe_ref[0].astype(BF16)], axis=1)
    s = _dot_nt(q, k)
    m = jnp.max(s, axis=-1, keepdims=True)
    if has_ctx:
        kc = jnp.concatenate([knc_ref[0], kpec_ref[0].astype(BF16)], axis=1)
        s_c = _dot_nt(q, kc)
        m = jnp.maximum(m, jnp.max(s_c, axis=-1, keepdims=True))
    e = jnp.exp(s - m)
    den = jnp.sum(e, axis=-1, keepdims=True)
    o = _dot(e.astype(BF16), vv_ref[0])
    if has_ctx:
        e_c = jnp.exp(s_c - m)
        den = den + jnp.sum(e_c, axis=-1, keepdims=True)
        o = o + _dot(e_c.astype(BF16), vvc_ref[0])
    o_ref[0] = (o / den).astype(o_ref.dtype)


def _mla_attn(q, kn, kpe, vv, ctx):
    bsz, seq, _ = q.shape
    tq = 256
    has_ctx = ctx is not None

    def keys(n, per_head):
        return pl.BlockSpec((1, n, LANES), (lambda b, h, i: (b, 0, h)) if per_head else (lambda b, h, i: (b, 0, 0)))

    in_specs = [pl.BlockSpec((1, tq, C_QK_PAD), lambda b, h, i: (b, i, h)),
                keys(seq, True), keys(seq, False), keys(seq, True)]
    args = [q, kn, kpe, vv]
    if has_ctx:
        past = ctx[0].shape[1]
        in_specs += [keys(past, True), keys(past, False), keys(past, True)]
        args += list(ctx)
    return pl.pallas_call(
        functools.partial(_mla_attn_body, has_ctx), grid=(bsz, C_HEADS, seq // tq), in_specs=in_specs,
        out_specs=pl.BlockSpec((1, tq, C_V), lambda b, h, i: (b, i, h)),
        out_shape=jax.ShapeDtypeStruct((bsz, seq, C_HEADS * C_V), BF16),
        name="mla_attn", compiler_params=_params("parallel", "parallel", "parallel"),
    )(*args)


def _ffn_body(n_tiles, x_ref, xp_ref, xn_ref, sc_ref, sh_ref, g2_ref, g_ref, gp_ref,
              wa_ref, wb_ref, ca_ref, cb_ref, wd_ref, o_ref, h_scr, acc_scr):
    i = pl.program_id(1)
    f = pl.program_id(2)
    tm = x_ref.shape[1]

    @pl.when(f == 0)
    def _():
        g, sc, sh = g_ref[...], sc_ref[0], sh_ref[0]
        h_scr[:tm] = _norm_mod(x_ref[0], g, sc, sh).astype(BF16)
        xh = jnp.concatenate([xp_ref[0], xn_ref[0]], axis=0)
        h_scr[tm:] = _norm_mod(xh, g, sc, sh).astype(BF16)
        acc_scr[...] = jnp.zeros_like(acc_scr)

    h = h_scr[...]
    has_up, has_dn = i > 0, i < n_tiles - 1

    def branch(w_ref, c_ref):
        u = _dot(h, w_ref[...])
        up = jnp.where(has_up, u[tm + SUBLANES - 1:tm + SUBLANES], 0.0)
        dn = jnp.where(has_dn, u[tm + SUBLANES:tm + SUBLANES + 1], 0.0)
        return _dwconv3(u[:tm], up, dn, c_ref[...])

    act = (_silu(branch(wa_ref, ca_ref)) * branch(wb_ref, cb_ref)).astype(BF16)
    acc_scr[...] += _dot(act, wd_ref[...])

    @pl.when(f == pl.num_programs(2) - 1)
    def _():
        o_ref[0] = x_ref[0] + g2_ref[0] * _rms(acc_scr[...], gp_ref[...])


def _ffn(x, modflat, mod_rows, row_of, layer, g_pre, g_post, w_up, conv_w, w_down):
    bsz, seq, d = x.shape
    tm = next(t for t in (1024, 512, 256) if seq % t == 0)
    nt = seq // tm
    nf = D_FF // FF_TILE
    prev, nxt = _halo_specs(tm, seq)
    in_specs = [pl.BlockSpec((1, tm, d), lambda b, i, f: (b, i, 0)), prev, nxt,
                _mod_spec(mod_rows, layer, row_of, 4), _mod_spec(mod_rows, layer, row_of, 3),
                _mod_spec(mod_rows, layer, row_of, 5), _const_spec((1, d)), _const_spec((1, d)),
                pl.BlockSpec((d, FF_TILE), lambda b, i, f: (0, f)),
                pl.BlockSpec((d, FF_TILE), lambda b, i, f: (0, nf + f)),
                pl.BlockSpec((3, FF_TILE), lambda b, i, f: (0, f)),
                pl.BlockSpec((3, FF_TILE), lambda b, i, f: (0, nf + f)),
                pl.BlockSpec((FF_TILE, d), lambda b, i, f: (f, 0))]
    return pl.pallas_call(
        functools.partial(_ffn_body, nt), grid=(bsz, nt, nf), in_specs=in_specs,
        out_specs=pl.BlockSpec((1, tm, d), lambda b, i, f: (b, i, 0)),
        out_shape=jax.ShapeDtypeStruct(x.shape, F32),
        scratch_shapes=[pltpu.VMEM((tm + 2 * SUBLANES, d), BF16), pltpu.VMEM((tm, d), F32)],
        name="conv_ffn", compiler_params=_params("parallel", "parallel", "arbitrary"),
    )(x, x, x, modflat, modflat, modflat, g_pre, g_post, w_up, w_up, conv_w, conv_w, w_down)


def _rope_tables(seq):
    t = jnp.arange(seq)
    row = (t // GRID_W).astype(F32)
    col = (t % GRID_W).astype(F32)
    n_freq = A_HEAD_DIM // 4
    inv_freq = ROPE_BASE ** (-jnp.arange(n_freq, dtype=F32) / n_freq)
    ang = jnp.concatenate([row[:, None] * inv_freq, col[:, None] * inv_freq], axis=-1)
    cos, sin = jnp.cos(ang), jnp.sin(ang)
    return (jnp.concatenate([cos, cos, cos, cos], axis=-1), jnp.concatenate([-sin, sin, -sin, sin], axis=-1))


def _lane_vec(p):
    v = jnp.zeros((2, 2, B_HEADS), F32).at[:, 0, :].set(p.astype(F32)).reshape(1, 4 * B_HEADS)
    return jnp.pad(v, ((0, 0), (0, LANES - 4 * B_HEADS)))


def _even_layer(x, layer, j, prm, mod, rope_tabs, ctx):
    modflat, mod_rows, row_of = mod
    w_in = jnp.pad(prm["ev_w_in"][j], ((0, 0), (0, EVEN_IN_PAD - prm["ev_w_in"].shape[-1]))).astype(BF16)
    qa, ka, va, qkvb, gate, gbv, gbt = _even_in(
        x, modflat, mod_rows, row_of, layer, prm["norm_pre"][layer, 0][None], w_in, prm["ev_conv"][j],
        _lane_vec(prm["ev_a_log"][j]), _lane_vec(prm["ev_dt_bias"][j]), rope_tabs)
    if ctx is None:
        oa = _even_attn(qa, ka, va, None, prm["ev_sink"][j])
        o_f, o_b, s_f, s_b = _delta(qkvb, gbv, gbt, None)
    else:
        k_ctx, v_ctx, s0_f, s0_b = ctx
        flat = lambda t: t.reshape(t.shape[0], t.shape[1], A_KV)
        oa = _even_attn(qa, ka, va, (flat(k_ctx), flat(v_ctx)), prm["ev_sink"][j])
        o_f, o_b, s_f, s_b = _delta(qkvb, gbv, gbt, (s0_f, s0_b))
    x = _mixer_out([oa, o_f, o_b, gate], x, modflat, mod_rows, row_of, layer,
                   prm["ev_w_out"][j].astype(BF16), prm["norm_post"][layer, 0][None],
                   prm["ev_out_norm"][j][None])
    return x, (ka, va, s_f, s_b)


def _odd_layer(x, layer, j, prm, mod, rope_tabs, ctx):
    modflat, mod_rows, row_of = mod
    w_in = jnp.pad(prm["od_w_in"][j], ((0, 0), (0, ODD_IN_PAD - prm["od_w_in"].shape[-1]))).astype(BF16)
    w_q = prm["od_w_q_up"][j].reshape(C_Q_LORA, C_HEADS, C_NOPE + C_ROPE)
    w_q = jnp.pad(w_q, ((0, 0), (0, 0), (0, C_QK_PAD - C_NOPE - C_ROPE)))
    w_q = w_q.reshape(C_Q_LORA, C_HEADS * C_QK_PAD).astype(BF16)
    w_kv = prm["od_w_kv_up"][j].reshape(C_KV_LORA, C_HEADS, 2, C_NOPE)
    w_kv = w_kv.transpose(0, 2, 1, 3).reshape(C_KV_LORA, 2 * C_HEADS * C_NOPE).astype(BF16)
    q, ckv, kpe, kn, vv = _odd_in(
        x, modflat, mod_rows, row_of, layer, prm["norm_pre"][layer, 0][None], w_in,
        prm["od_q_norm"][j][None], prm["od_kv_norm"][j][None], w_q, w_kv, rope_tabs)
    if ctx is None:
        o = _mla_attn(q, kn, kpe, vv, None)
    else:
        ckv_ctx, kpe_ctx = ctx
        kn_c, vv_c = _kv_up(ckv_ctx, w_kv)
        kpe_c = jnp.pad(kpe_ctx, ((0, 0), (0, 0), (0, LANES - C_ROPE)))
        o = _mla_attn(q, kn, kpe, vv, (kn_c, kpe_c, vv_c))
    x = _mixer_out([o], x, modflat, mod_rows, row_of, layer, prm["od_w_out"][j].astype(BF16),
                   prm["norm_post"][layer, 0][None])
    return x, (ckv, kpe[..., :C_ROPE])


def _trunk(x, prm, mod, rope_tabs, ctxs):
    depth = prm["w_mod"].shape[0]
    modflat, mod_rows, row_of = mod
    new_ctx = []
    for layer in range(depth):
        j = layer // 2
        ctx = None if ctxs is None else ctxs[layer]
        if layer % 2 == 0:
            x, nc = _even_layer(x, layer, j, prm, mod, rope_tabs, ctx)
        else:
            x, nc = _odd_layer(x, layer, j, prm, mod, rope_tabs, ctx)
        new_ctx.append(nc)
        x = _ffn(x, modflat, mod_rows, row_of, layer, prm["norm_pre"][layer, 1][None],
                 prm["norm_post"][layer, 1][None], prm["ffn_w_up"][layer].astype(BF16),
                 prm["ffn_conv"][layer], prm["ffn_w_down"][layer].astype(BF16))
    return x, new_ctx


def kernel(x_prompt, x_sample, cache_attn_k, cache_attn_v, state_delta_fwd, state_delta_bwd,
           cache_mla_ckv, cache_mla_kpe, c, c_ctx, w_mod, b_mod, norm_pre, norm_post,
           ffn_w_up, ffn_conv, ffn_w_down, ev_w_in, ev_conv, ev_a_log, ev_dt_bias, ev_sink,
           ev_out_norm, ev_w_out, od_w_in, od_q_norm, od_kv_norm, od_w_q_up, od_w_kv_up, od_w_out):
    prm = {
        "w_mod": w_mod, "b_mod": b_mod, "norm_pre": norm_pre, "norm_post": norm_post,
        "ffn_w_up": ffn_w_up, "ffn_conv": ffn_conv, "ffn_w_down": ffn_w_down,
        "ev_w_in": ev_w_in, "ev_conv": ev_conv, "ev_a_log": ev_a_log, "ev_dt_bias": ev_dt_bias,
        "ev_sink": ev_sink, "ev_out_norm": ev_out_norm, "ev_w_out": ev_w_out,
        "od_w_in": od_w_in, "od_q_norm": od_q_norm, "od_kv_norm": od_kv_norm,
        "od_w_q_up": od_w_q_up, "od_w_kv_up": od_w_kv_up, "od_w_out": od_w_out,
    }
    depth = w_mod.shape[0]
    n_c = c.shape[0]
    mod_rows = -(-(n_c + 1) // SUBLANES) * SUBLANES
    cvec = jnp.concatenate([c, c_ctx[None], jnp.zeros((mod_rows - n_c - 1, c.shape[1]), F32)], axis=0)
    modflat = _mod_all(cvec, w_mod, b_mod).reshape(depth * mod_rows * N_MOD, 1, D_MODEL)

    y_prompt, pctx = _trunk(x_prompt, prm, (modflat, mod_rows, lambda b: n_c), None, None)

    ctxs = []
    for layer in range(depth):
        j = layer // 2
        if layer % 2 == 0:
            ctxs.append((cache_attn_k[:, j], cache_attn_v[:, j], state_delta_fwd[:, j], state_delta_bwd[:, j]))
        else:
            ctxs.append((cache_mla_ckv[:, j], cache_mla_kpe[:, j]))
    y_sample, _ = _trunk(x_sample, prm, (modflat, mod_rows, lambda b: b), _rope_tables(x_sample.shape[1]), ctxs)

    bsz, seq = x_prompt.shape[:2]
    even, odd = pctx[0::2], pctx[1::2]
    kv_shape = (bsz, len(even), seq, A_KV_HEADS, A_HEAD_DIM)
    new_attn_k = jnp.stack([e[0] for e in even], axis=1).reshape(kv_shape)
    new_attn_v = jnp.stack([e[1] for e in even], axis=1).reshape(kv_shape)
    new_delta_fwd = jnp.stack([e[2] for e in even], axis=1)
    new_delta_bwd = jnp.stack([e[3] for e in even], axis=1)
    new_mla_ckv = jnp.stack([o[0] for o in odd], axis=1)
    new_mla_kpe = jnp.stack([o[1] for o in odd], axis=1)
    return (y_prompt, y_sample, new_attn_k, new_attn_v, new_delta_fwd, new_delta_bwd, new_mla_ckv, new_mla_kpe)
```

```python
import functools
import math

import jax
import jax.numpy as jnp
from jax import lax
from jax.experimental import pallas as pl
from jax.experimental.pallas import tpu as pltpu

F32 = jnp.float32
BF16 = jnp.bfloat16

D_MODEL = 1024
GRID_W = 64
A_HEADS = 8
A_KV_HEADS = 2
A_HEAD_DIM = 64
A_REP = A_HEADS // A_KV_HEADS
WINDOW = 128
Q_BLOCK = 128
B_HEADS = 4
B_DK = 128
B_DV = 128
CHUNK = 64
C_HEADS = 8
C_Q_LORA = 384
C_KV_LORA = 256
C_NOPE = 128
C_ROPE = 64
C_V = 128
C_QK_PAD = 256
D_FF = 2816
ROPE_BASE = 10000.0
EPS = 1e-6
N_MOD = 6

A_Q = A_HEADS * A_HEAD_DIM
A_KV = A_KV_HEADS * A_HEAD_DIM
B_QK = B_HEADS * B_DK
B_VW = B_HEADS * B_DV
B_CONV_CH = 2 * B_QK + B_VW
KA0 = A_Q
VA0 = KA0 + A_KV
QKVB0 = VA0 + A_KV
GATE0 = QKVB0 + B_CONV_CH
AB0 = GATE0 + B_VW
LANES = 128
SUBLANES = 8
EVEN_IN_PAD = AB0 + LANES
ODD_IN_PAD = C_Q_LORA + C_KV_LORA + LANES
FF_TILE = 256
MLA_KEY_CHUNK = 512
VMEM_LIMIT_BYTES = 56 * 1024 * 1024


def _params(*sem):
    return pltpu.CompilerParams(dimension_semantics=sem, vmem_limit_bytes=VMEM_LIMIT_BYTES)


def _dot(a, b):
    return jnp.dot(a, b, preferred_element_type=F32)


def _dot_nt(a, b):
    return lax.dot_general(a, b, (((1,), (1,)), ((), ())), preferred_element_type=F32)


def _sigmoid(x):
    return 1.0 / (1.0 + jnp.exp(-x))


def _silu(x):
    return x * _sigmoid(x)


def _softplus(x):
    return jnp.maximum(x, 0.0) + jnp.log(1.0 + jnp.exp(-jnp.abs(x)))


def _rms(x, g):
    return x * lax.rsqrt(jnp.mean(x * x, axis=-1, keepdims=True) + EPS) * g


def _norm_mod(x, g, scale, shift):
    return _rms(x, g) * (1.0 + scale) + shift


def _rope(x, cos_t, sin_t):
    lane = lax.broadcasted_iota(jnp.int32, x.shape, 1)
    swapped = jnp.where((lane & 63) < 32, pltpu.roll(x, LANES - 32, 1), pltpu.roll(x, 32, 1))
    return x * cos_t + swapped * sin_t


def _dwconv3(u, up, dn, cw):
    tm = u.shape[0]
    row = lax.broadcasted_iota(jnp.int32, u.shape, 0)
    u_prev = jnp.where(row == 0, up, pltpu.roll(u, 1, 0))
    u_next = jnp.where(row == tm - 1, dn, pltpu.roll(u, tm - 1, 0))
    return cw[0:1] * u_prev + cw[1:2] * u + cw[2:3] * u_next


def _halo_specs(tm, seq):
    per = tm // SUBLANES
    last = seq // SUBLANES - 1
    prev = pl.BlockSpec((1, SUBLANES, D_MODEL), lambda b, i, *_: (b, jnp.maximum(i * per - 1, 0), 0))
    nxt = pl.BlockSpec((1, SUBLANES, D_MODEL), lambda b, i, *_: (b, jnp.minimum((i + 1) * per, last), 0))
    return prev, nxt


def _mod_spec(mod_rows, layer, row_of, k):
    return pl.BlockSpec((1, 1, D_MODEL),
                        lambda b, *_: ((layer * mod_rows + row_of(b)) * N_MOD + k, 0, 0))


def _const_spec(shape):
    nd = len(shape)
    return pl.BlockSpec(shape, lambda *_: (0,) * nd)


def _mod_body(c_ref, w_ref, b_ref, o_ref):
    s = _silu(c_ref[...]).astype(BF16)
    o_ref[0] = _dot(s, w_ref[0].astype(BF16)) + b_ref[0]


def _mod_all(cvec, w_mod, b_mod):
    depth, d, n = w_mod.shape
    rows = cvec.shape[0]
    tn = n // 4
    return pl.pallas_call(
        _mod_body, grid=(depth, n // tn),
        in_specs=[pl.BlockSpec((rows, d), lambda l, j: (0, 0)),
                  pl.BlockSpec((1, d, tn), lambda l, j: (l, 0, j)),
                  pl.BlockSpec((1, 1, tn), lambda l, j: (l, 0, j))],
        out_specs=pl.BlockSpec((1, rows, tn), lambda l, j: (l, 0, j)),
        out_shape=jax.ShapeDtypeStruct((depth, rows, n), F32),
        name="mod_vectors", compiler_params=_params("parallel", "parallel"),
    )(cvec, w_mod, b_mod.reshape(depth, 1, n))


def _even_in_body(rope, n_tiles, x_ref, xp_ref, xn_ref, sc_ref, sh_ref, g_ref, w_ref, cw_ref,
                  alog_ref, dt_ref, *rest):
    if rope:
        cos_ref, sin_ref = rest[:2]
        rest = rest[2:]
    qa_ref, ka_ref, va_ref, qkv_ref, gate_ref, gb_ref, gbt_ref = rest
    i = pl.program_id(1)
    tm = x_ref.shape[1]
    g, sc, sh = g_ref[...], sc_ref[0], sh_ref[0]
    h = _norm_mod(x_ref[0], g, sc, sh).astype(BF16)
    y = _dot(h, w_ref[...])
    xh = jnp.concatenate([xp_ref[0], xn_ref[0]], axis=0)
    hh = _norm_mod(xh, g, sc, sh).astype(BF16)
    yh = _dot(hh, w_ref[:, QKVB0:GATE0])
    up = jnp.where(i > 0, yh[SUBLANES - 1:SUBLANES], 0.0)
    dn = jnp.where(i < n_tiles - 1, yh[SUBLANES:SUBLANES + 1], 0.0)

    qa = y[:, :A_Q]
    ka = y[:, KA0:VA0]
    if rope:
        cos_t, sin_t = cos_ref[...], sin_ref[...]
        qa = jnp.concatenate([_rope(qa[:, j * LANES:(j + 1) * LANES], cos_t, sin_t)
                              for j in range(A_Q // LANES)], axis=1)
        ka = _rope(ka, cos_t, sin_t)
    qa_ref[0] = qa
    ka_ref[0] = ka
    va_ref[0] = y[:, VA0:QKVB0]
    gate_ref[0] = y[:, GATE0:AB0]

    s = _silu(_dwconv3(y[:, QKVB0:GATE0], up, dn, cw_ref[...]))
    for j in range(B_CONV_CH // LANES):
        seg = s[:, j * LANES:(j + 1) * LANES]
        if j < 2 * B_HEADS:
            seg = seg * lax.rsqrt(jnp.sum(seg * seg, axis=-1, keepdims=True) + EPS)
        qkv_ref[0, :, j * LANES:(j + 1) * LANES] = seg

    ab = y[:, AB0:EVEN_IN_PAD]
    lane = lax.broadcasted_iota(jnp.int32, ab.shape, 1)
    gdec = -jnp.exp(alog_ref[...]) * _softplus(ab + dt_ref[...])
    gb = jnp.where((lane & B_HEADS) == 0, gdec, _sigmoid(ab))
    gb_ref[0] = gb
    gbt = gb.T
    for c in range(tm // CHUNK):
        gbt_ref[0, c] = gbt[:4 * B_HEADS, c * CHUNK:(c + 1) * CHUNK]


def _even_in(x, modflat, mod_rows, row_of, layer, g_pre, w_in, conv_w, alog_vec, dt_vec, rope_tabs):
    bsz, seq, d = x.shape
    tm = 512 if seq % 512 == 0 else 256
    nt = seq // tm
    rope = rope_tabs is not None
    prev, nxt = _halo_specs(tm, seq)
    in_specs = [pl.BlockSpec((1, tm, d), lambda b, i: (b, i, 0)), prev, nxt,
                _mod_spec(mod_rows, layer, row_of, 1), _mod_spec(mod_rows, layer, row_of, 0),
                _const_spec((1, d)), _const_spec(w_in.shape), _const_spec(conv_w.shape),
                _const_spec((1, LANES)), _const_spec((1, LANES))]
    args = [x, x, x, modflat, modflat, g_pre, w_in, conv_w, alog_vec, dt_vec]
    if rope:
        in_specs += [pl.BlockSpec((tm, LANES), lambda b, i: (i, 0))] * 2
        args += list(rope_tabs)

    def out(width, dtype=F32):
        return (pl.BlockSpec((1, tm, width), lambda b, i: (b, i, 0)),
                jax.ShapeDtypeStruct((bsz, seq, width), dtype))

    outs = [out(A_Q), out(A_KV), out(A_KV), out(B_CONV_CH), out(B_VW), out(LANES),
            (pl.BlockSpec((1, tm // CHUNK, 4 * B_HEADS, CHUNK), lambda b, i: (b, i, 0, 0)),
             jax.ShapeDtypeStruct((bsz, seq // CHUNK, 4 * B_HEADS, CHUNK), F32))]
    return pl.pallas_call(
        functools.partial(_even_in_body, rope, nt), grid=(bsz, nt), in_specs=in_specs,
        out_specs=[o[0] for o in outs], out_shape=[o[1] for o in outs],
        name="even_in", compiler_params=_params("parallel", "parallel"),
    )(*args)


def _even_attn_body(windowed, q_ref, k_ref, v_ref, *rest):
    if windowed:
        kc_ref, vc_ref, sink_ref, o_ref = rest
    else:
        sink_ref, o_ref = rest
    i = pl.program_id(1)
    seq = k_ref.shape[1]
    q_all = q_ref[0] * (A_HEAD_DIM ** -0.5)
    rows = A_REP * Q_BLOCK
    if windowed:
        band = Q_BLOCK + 2 * WINDOW
        k0 = pl.multiple_of(jnp.clip(i * Q_BLOCK - WINDOW, 0, seq - band), Q_BLOCK)
        k_all = k_ref[0, pl.ds(k0, band), :].astype(BF16)
        v_all = v_ref[0, pl.ds(k0, band), :].astype(BF16)
        kc_all = kc_ref[0].astype(BF16)
        vc_all = vc_ref[0].astype(BF16)
        qpos = i * Q_BLOCK + (lax.broadcasted_iota(jnp.int32, (rows, band), 0) & (Q_BLOCK - 1))
        kpos = k0 + lax.broadcasted_iota(jnp.int32, (rows, band), 1)
        valid = jnp.abs(qpos - kpos) <= WINDOW
    else:
        k_all = k_ref[0].astype(BF16)
        v_all = v_ref[0].astype(BF16)
    rid = lax.broadcasted_iota(jnp.int32, (rows, 1), 0) // Q_BLOCK
    groups = range(A_KV_HEADS)
    cols = [slice(g * A_HEAD_DIM, (g + 1) * A_HEAD_DIM) for g in groups]
    q, sink = [], []
    for g in groups:
        heads = range(g * A_REP, (g + 1) * A_REP)
        q.append(jnp.concatenate([q_all[:, h * A_HEAD_DIM:(h + 1) * A_HEAD_DIM] for h in heads],
                                 axis=0).astype(BF16))
        sk = jnp.zeros((rows, 1), F32)
        for r, h in enumerate(heads):
            sk = jnp.where(rid == r, sink_ref[h], sk)
        sink.append(sk)
    s = [_dot_nt(q[g], k_all[:, cols[g]]) for g in groups]
    if windowed:
        s = [jnp.where(valid, s[g], -jnp.inf) for g in groups]
        s_c = [_dot_nt(q[g], kc_all[:, cols[g]]) for g in groups]
        m = [jnp.maximum(jnp.max(s_c[g], axis=-1, keepdims=True), sink[g]) for g in groups]
    else:
        m = sink
    m = [jnp.maximum(jnp.max(s[g], axis=-1, keepdims=True), m[g]) for g in groups]
    e = [jnp.exp(s[g] - m[g]) for g in groups]
    den = [jnp.sum(e[g], axis=-1, keepdims=True) + jnp.exp(sink[g] - m[g]) for g in groups]
    o = [_dot(e[g].astype(BF16), v_all[:, cols[g]]) for g in groups]
    if windowed:
        e_c = [jnp.exp(s_c[g] - m[g]) for g in groups]
        den = [den[g] + jnp.sum(e_c[g], axis=-1, keepdims=True) for g in groups]
        o = [o[g] + _dot(e_c[g].astype(BF16), vc_all[:, cols[g]]) for g in groups]
    outs = []
    for g in groups:
        og = o[g] / den[g]
        outs += [og[r * Q_BLOCK:(r + 1) * Q_BLOCK] for r in range(A_REP)]
    o_ref[0] = jnp.concatenate(outs, axis=1).astype(o_ref.dtype)


def _even_attn(qa, ka, va, ctx, sink):
    bsz, seq, _ = qa.shape
    windowed = ctx is not None
    in_specs = [pl.BlockSpec((1, Q_BLOCK, A_Q), lambda b, i: (b, i, 0)),
                pl.BlockSpec((1, seq, A_KV), lambda b, i: (b, 0, 0)),
                pl.BlockSpec((1, seq, A_KV), lambda b, i: (b, 0, 0))]
    args = [qa, ka, va]
    if windowed:
        past = ctx[0].shape[1]
        in_specs += [pl.BlockSpec((1, past, A_KV), lambda b, i: (b, 0, 0))] * 2
        args += list(ctx)
    in_specs.append(pl.BlockSpec(memory_space=pltpu.SMEM))
    args.append(sink)
    return pl.pallas_call(
        functools.partial(_even_attn_body, windowed), grid=(bsz, seq // Q_BLOCK), in_specs=in_specs,
        out_specs=pl.BlockSpec((1, Q_BLOCK, A_Q), lambda b, i: (b, i, 0)),
        out_shape=jax.ShapeDtypeStruct((bsz, seq, A_Q), BF16),
        name="even_attn", compiler_params=_params("parallel", "parallel"),
    )(*args)


def _bf16_terms(x, n):
    terms = []
    for _ in range(n):
        t = x.astype(BF16).astype(F32)
        terms.append(t)
        x = x - t
    return terms


def _dot_split(a, b):
    ah, al = _bf16_terms(a, 2)
    bh, bl = _bf16_terms(b, 2)
    return _dot(jnp.concatenate([ah, al, ah, al], axis=1).astype(BF16),
                jnp.concatenate([bh, bh, bl, bl], axis=0).astype(BF16))


def _mask_dot(mask, x, mask_first):
    terms = _bf16_terms(x, 3)
    m16 = mask.astype(BF16)
    if mask_first:
        return _dot(jnp.concatenate([m16] * 3, axis=1), jnp.concatenate(terms, axis=0).astype(BF16))
    return _dot(jnp.concatenate(terms, axis=1).astype(BF16), jnp.concatenate([m16] * 3, axis=0))


def _delta_chunk_terms(chains, eye):
    n = range(len(chains))
    q, k, v, gcol, grow, bcol, incl, strict, g_last = zip(*chains)
    decay = [jnp.where(incl[i], jnp.exp(jnp.where(incl[i], gcol[i] - grow[i], 0.0)), 0.0) for i in n]
    qs = [q[i] * (B_DK ** -0.5) for i in n]
    kbeta = [k[i] * bcol[i] for i in n]
    both = [_dot_nt(jnp.concatenate([kbeta[i], qs[i]], axis=0).astype(BF16), k[i].astype(BF16)) for i in n]
    qk = [jnp.where(incl[i], both[i][CHUNK:] * decay[i], 0.0).astype(BF16) for i in n]
    p = [-jnp.where(strict[i], both[i][:CHUNK] * decay[i], 0.0) for i in n]
    t_inv = [eye + p[i] for i in n]
    for _ in range(int(math.log2(CHUNK)) - 1):
        p = [_dot_split(p[i], p[i]) for i in n]
        t_inv = [t_inv[i] + _dot_split(t_inv[i], p[i]) for i in n]
    eg = [jnp.exp(gcol[i]) for i in n]
    rhs = [jnp.concatenate([v[i] * bcol[i], kbeta[i] * eg[i]], axis=1).astype(BF16) for i in n]
    uw = [_dot(t_inv[i].astype(BF16), rhs[i]).astype(BF16) for i in n]
    kd = [(k[i] * jnp.exp(g_last[i] - gcol[i])).T.astype(BF16) for i in n]
    qk_uw = [_dot(qk[i], uw[i]) for i in n]
    cp = [_dot(kd[i], uw[i]) for i in n]
    out = []
    for i in n:
        q_eff = qs[i] * eg[i] - qk_uw[i][:, B_DV:]
        pq = jnp.concatenate([cp[i][:, B_DV:], q_eff], axis=0).astype(BF16)
        out.append((pq, cp[i][:, :B_DV], qk_uw[i][:, :B_DV], jnp.exp(g_last[i])))
    return out


def _delta_body(zero_init, qf_ref, qb_ref, gf_ref, gb_ref, gtf_ref, gtb_ref, *rest):
    if not zero_init:
        s0f_ref, s0b_ref = rest[:2]
        rest = rest[2:]
    of_ref, ob_ref, sfo_ref, sbo_ref, s_scr, pq_scr, c_scr, oi_scr, dec_scr = rest
    n = pl.program_id(1)
    n_chunks = gtf_ref.shape[1]
    n_chains = 2 * B_HEADS

    @pl.when(n == 0)
    def _():
        if zero_init:
            s_scr[...] = jnp.zeros_like(s_scr)
        else:
            s_scr[:B_HEADS] = s0f_ref[0]
            s_scr[B_HEADS:] = s0b_ref[0]

    ii = lax.broadcasted_iota(jnp.int32, (CHUNK, CHUNK), 0)
    jj = lax.broadcasted_iota(jnp.int32, (CHUNK, CHUNK), 1)
    lower, upper = ii >= jj, ii <= jj
    strict_lower, strict_upper = ii > jj, ii < jj
    tri_l, tri_u = lower.astype(F32), upper.astype(F32)
    eye = (ii == jj).astype(F32)

    def prepare(c, carry):
        rows = pl.ds(pl.multiple_of(c * CHUNK, CHUNK), CHUNK)
        gf, gb = gf_ref[0, rows, :], gb_ref[0, rows, :]
        gcf, gcb = _mask_dot(tri_l, gf, True), _mask_dot(tri_u, gb, True)
        gcf_t, gcb_t = _mask_dot(tri_u, gtf_ref[0, c], False), _mask_dot(tri_l, gtb_ref[0, c], False)
        chains = []
        for h in range(B_HEADS):
            cq = slice(h * B_DK, (h + 1) * B_DK)
            ck = slice(B_QK + h * B_DK, B_QK + (h + 1) * B_DK)
            cv = slice(2 * B_QK + h * B_DV, 2 * B_QK + (h + 1) * B_DV)
            lg, lb = h, B_HEADS + h
            gcol = gcf[:, lg:lg + 1]
            chains.append((qf_ref[0, rows, cq], qf_ref[0, rows, ck], qf_ref[0, rows, cv], gcol,
                           gcf_t[lg:lg + 1, :], gf[:, lb:lb + 1], lower, strict_lower, gcol[CHUNK - 1:CHUNK]))
        for h in range(B_HEADS):
            cq = slice(h * B_DK, (h + 1) * B_DK)
            ck = slice(B_QK + h * B_DK, B_QK + (h + 1) * B_DK)
            cv = slice(2 * B_QK + h * B_DV, 2 * B_QK + (h + 1) * B_DV)
            lg, lb = 2 * B_HEADS + h, 3 * B_HEADS + h
            gcol = gcb[:, lg:lg + 1]
            chains.append((qb_ref[0, rows, cq], qb_ref[0, rows, ck], qb_ref[0, rows, cv], gcol,
                           gcb_t[lg:lg + 1, :], gb[:, lb:lb + 1], upper, strict_upper, gcol[0:1]))
        for idx, (pq, cc, oi, dec) in enumerate(_delta_chunk_terms(chains, eye)):
            pq_scr[c, idx] = pq
            c_scr[c, idx] = cc
            oi_scr[c, idx] = oi
            dec_scr[c, idx] = jnp.broadcast_to(dec, (SUBLANES, LANES))
        return carry

    lax.fori_loop(0, n_chunks, prepare, 0)

    def advance(t, carry):
        cf, cb = t, n_chunks - 1 - t
        outs, states = [], []
        for idx in range(n_chains):
            c = cf if idx < B_HEADS else cb
            s = s_scr[idx]
            r = _dot(pq_scr[c, idx], s.astype(BF16))
            outs.append(r[B_DK:] + oi_scr[c, idx])
            states.append(dec_scr[c, idx][0:1, :] * s + c_scr[c, idx] - r[:B_DK])
        of_ref[0, pl.ds(pl.multiple_of(cf * CHUNK, CHUNK), CHUNK), :] = jnp.concatenate(outs[:B_HEADS], axis=1)
        ob_ref[0, pl.ds(pl.multiple_of(cb * CHUNK, CHUNK), CHUNK), :] = jnp.concatenate(outs[B_HEADS:], axis=1)
        for idx in range(n_chains):
            s_scr[idx] = states[idx]
        return carry

    lax.fori_loop(0, n_chunks, advance, 0)

    @pl.when(n == pl.num_programs(1) - 1)
    def _():
        sfo_ref[0] = s_scr[:B_HEADS]
        sbo_ref[0] = s_scr[B_HEADS:]


def _delta(qkvb, gbv, gbt, states):
    bsz, seq, _ = qkvb.shape
    per = 4
    tb = per * CHUNK
    nb = seq // tb
    zero_init = states is None
    n_chains = 2 * B_HEADS

    def fwd(*tail):
        return lambda b, n: (b, n) + tail

    def bwd(*tail):
        return lambda b, n: (b, nb - 1 - n) + tail

    st_spec = pl.BlockSpec((1, B_HEADS, B_DK, B_DV), lambda b, n: (b, 0, 0, 0))
    in_specs = [pl.BlockSpec((1, tb, B_CONV_CH), fwd(0)), pl.BlockSpec((1, tb, B_CONV_CH), bwd(0)),
                pl.BlockSpec((1, tb, LANES), fwd(0)), pl.BlockSpec((1, tb, LANES), bwd(0)),
                pl.BlockSpec((1, per, 4 * B_HEADS, CHUNK), fwd(0, 0)),
                pl.BlockSpec((1, per, 4 * B_HEADS, CHUNK), bwd(0, 0))]
    args = [qkvb, qkvb, gbv, gbv, gbt, gbt]
    if not zero_init:
        in_specs += [st_spec, st_spec]
        args += list(states)
    st_shape = jax.ShapeDtypeStruct((bsz, B_HEADS, B_DK, B_DV), F32)
    o_shape = jax.ShapeDtypeStruct((bsz, seq, B_VW), F32)
    return pl.pallas_call(
        functools.partial(_delta_body, zero_init), grid=(bsz, nb), in_specs=in_specs,
        out_specs=[pl.BlockSpec((1, tb, B_VW), fwd(0)), pl.BlockSpec((1, tb, B_VW), bwd(0)),
                   st_spec, st_spec],
        out_shape=[o_shape, o_shape, st_shape, st_shape],
        scratch_shapes=[pltpu.VMEM((n_chains, B_DK, B_DV), F32),
                        pltpu.VMEM((per, n_chains, B_DK + CHUNK, B_DV), BF16),
                        pltpu.VMEM((per, n_chains, B_DK, B_DV), F32),
                        pltpu.VMEM((per, n_chains, CHUNK, B_DV), F32),
                        pltpu.VMEM((per, n_chains, SUBLANES, LANES), F32)],
        name="delta_rule", compiler_params=_params("parallel", "arbitrary"),
    )(*args)


def _even_out_body(oa_ref, of_ref, ob_ref, gate_ref, x_ref, g1_ref, on_ref, w_ref, gp_ref, o_ref):
    ob = of_ref[0] + ob_ref[0]
    on = on_ref[...]
    segs = [_rms(ob[:, h * B_DV:(h + 1) * B_DV], on) for h in range(B_HEADS)]
    obg = (jnp.concatenate(segs, axis=1) * _silu(gate_ref[0])).astype(BF16)
    mix = _dot(oa_ref[0], w_ref[:A_Q, :]) + _dot(obg, w_ref[A_Q:, :])
    o_ref[0] = x_ref[0] + g1_ref[0] * _rms(mix, gp_ref[...])


def _odd_out_body(a_ref, x_ref, g1_ref, w_ref, gp_ref, o_ref):
    mix = _dot(a_ref[0], w_ref[...])
    o_ref[0] = x_ref[0] + g1_ref[0] * _rms(mix, gp_ref[...])


def _mixer_out(acts, x, modflat, mod_rows, row_of, layer, w_out, g_post, out_norm=None):
    bsz, seq, d = x.shape
    tm = 512 if seq % 512 == 0 else 256

    def row_spec(width):
        return pl.BlockSpec((1, tm, width), lambda b, i: (b, i, 0))

    in_specs = [row_spec(a.shape[-1]) for a in acts] + [row_spec(d), _mod_spec(mod_rows, layer, row_of, 2)]
    args = list(acts) + [x, modflat]
    if out_norm is not None:
        in_specs.append(_const_spec(out_norm.shape))
        args.append(out_norm)
    in_specs += [_const_spec(w_out.shape), _const_spec((1, d))]
    args += [w_out, g_post]
    return pl.pallas_call(
        _even_out_body if out_norm is not None else _odd_out_body,
        grid=(bsz, seq // tm), in_specs=in_specs, out_specs=row_spec(d),
        out_shape=jax.ShapeDtypeStruct(x.shape, F32),
        name="mixer_out", compiler_params=_params("parallel", "parallel"),
    )(*args)


def _write_keys_values(kv, kpe, k_ref, v_ref):
    kpe16 = kpe.astype(BF16)
    for hd in range(C_HEADS):
        base = hd * C_QK_PAD
        k_ref[0, :, base:base + C_NOPE] = kv[:, hd * C_NOPE:(hd + 1) * C_NOPE].astype(BF16)
        k_ref[0, :, base + C_NOPE:base + C_QK_PAD] = kpe16
    v_ref[0] = kv[:, C_HEADS * C_NOPE:].astype(BF16)


def _odd_in_body(rope, x_ref, sc_ref, sh_ref, g_ref, w_ref, qn_ref, kvn_ref, wq_ref, wkv_ref, *rest):
    if rope:
        cos_ref, sin_ref = rest[:2]
        rest = rest[2:]
    q_ref, ckv_ref, kpe_ref, kn_ref, vv_ref = rest
    h = _norm_mod(x_ref[0], g_ref[...], sc_ref[0], sh_ref[0]).astype(BF16)
    y = _dot(h, w_ref[...])
    cq = _rms(y[:, :C_Q_LORA], qn_ref[...]).astype(BF16)
    q = _dot(cq, wq_ref[...])
    ckv = _rms(y[:, C_Q_LORA:C_Q_LORA + C_KV_LORA], kvn_ref[...])
    kpe = y[:, C_Q_LORA + C_KV_LORA:]
    scale = (C_NOPE + C_ROPE) ** -0.5
    if rope:
        cos_t, sin_t = cos_ref[...], sin_ref[...]
        kpe = _rope(kpe, cos_t, sin_t)
    for hd in range(C_HEADS):
        base = hd * C_QK_PAD
        q_ref[0, :, base:base + C_NOPE] = (q[:, base:base + C_NOPE] * scale).astype(BF16)
        qpe = q[:, base + C_NOPE:base + C_QK_PAD]
        if rope:
            qpe = _rope(qpe, cos_t, sin_t)
        q_ref[0, :, base + C_NOPE:base + C_QK_PAD] = (qpe * scale).astype(BF16)
    ckv_ref[0] = ckv
    kpe_ref[0] = kpe
    _write_keys_values(_dot(ckv.astype(BF16), wkv_ref[...]), kpe, kn_ref, vv_ref)


def _odd_in(x, modflat, mod_rows, row_of, layer, g_pre, w_in, q_norm, kv_norm, w_q, w_kv, rope_tabs):
    bsz, seq, d = x.shape
    tm = 512 if seq % 512 == 0 else 256
    rope = rope_tabs is not None
    in_specs = [pl.BlockSpec((1, tm, d), lambda b, i: (b, i, 0)),
                _mod_spec(mod_rows, layer, row_of, 1), _mod_spec(mod_rows, layer, row_of, 0),
                _const_spec((1, d)), _const_spec(w_in.shape), _const_spec(q_norm.shape),
                _const_spec(kv_norm.shape), _const_spec(w_q.shape), _const_spec(w_kv.shape)]
    args = [x, modflat, modflat, g_pre, w_in, q_norm, kv_norm, w_q, w_kv]
    if rope:
        in_specs += [pl.BlockSpec((tm, LANES), lambda b, i: (i, 0))] * 2
        args += list(rope_tabs)

    def out(width, dtype):
        return (pl.BlockSpec((1, tm, width), lambda b, i: (b, i, 0)),
                jax.ShapeDtypeStruct((bsz, seq, width), dtype))

    outs = [out(C_HEADS * C_QK_PAD, BF16), out(C_KV_LORA, F32), out(LANES, F32),
            out(C_HEADS * C_QK_PAD, BF16), out(C_HEADS * C_V, BF16)]
    return pl.pallas_call(
        functools.partial(_odd_in_body, rope), grid=(bsz, seq // tm), in_specs=in_specs,
        out_specs=[o[0] for o in outs], out_shape=[o[1] for o in outs],
        name="odd_in", compiler_params=_params("parallel", "parallel"),
    )(*args)


def _kv_up_body(ckv_ref, kpe_ref, wkv_ref, kn_ref, vv_ref):
    _write_keys_values(_dot(ckv_ref[0].astype(BF16), wkv_ref[...]), kpe_ref[0], kn_ref, vv_ref)


def _kv_up(ckv, kpe, w_kv):
    bsz, n, r = ckv.shape

    def rows(width):
        return pl.BlockSpec((1, n, width), lambda b: (b, 0, 0))

    return pl.pallas_call(
        _kv_up_body, grid=(bsz,),
        in_specs=[rows(r), rows(LANES), _const_spec(w_kv.shape)],
        out_specs=[rows(C_HEADS * C_QK_PAD), rows(C_HEADS * C_V)],
        out_shape=[jax.ShapeDtypeStruct((bsz, n, C_HEADS * C_QK_PAD), BF16),
                   jax.ShapeDtypeStruct((bsz, n, C_HEADS * C_V), BF16)],
        name="kv_up", compiler_params=_params("parallel"),
    )(ckv, kpe, w_kv)


def _mla_attn_body(has_ctx, q_ref, k_ref, v_ref, *rest):
    if has_ctx:
        kc_ref, vc_ref, o_ref = rest
        sources = [(kc_ref, vc_ref), (k_ref, v_ref)]
    else:
        (o_ref,) = rest
        sources = [(k_ref, v_ref)]
    q = q_ref[0]
    chunks = []
    for kr, vr in sources:
        n = kr.shape[1]
        step = min(MLA_KEY_CHUNK, n)
        chunks += [(kr, vr, lo, step) for lo in range(0, n, step)]

    def scores(j):
        kr, _, lo, step = chunks[j]
        return _dot_nt(q, kr[0, lo:lo + step, :])

    nxt = scores(0)
    for j, (_, vr, lo, step) in enumerate(chunks):
        s = nxt
        if j + 1 < len(chunks):
            nxt = scores(j + 1)
        top = jnp.max(s, axis=-1, keepdims=True)
        m_new = top if j == 0 else jnp.maximum(m, top)
        e = jnp.exp(s - m_new)
        d = jnp.sum(e, axis=-1, keepdims=True)
        part = _dot(e.astype(BF16), vr[0, lo:lo + step, :])
        if j == 0:
            den, o = d, part
        else:
            alpha = jnp.exp(m - m_new)
            den, o = alpha * den + d, alpha * o + part
        m = m_new
    o_ref[0] = (o / den).astype(o_ref.dtype)


def _mla_attn(q, kc, vv, ctx):
    bsz, seq, _ = q.shape
    tq = 256
    has_ctx = ctx is not None

    def keys(n, width):
        return pl.BlockSpec((1, n, width), lambda b, h, i: (b, 0, h))

    in_specs = [pl.BlockSpec((1, tq, C_QK_PAD), lambda b, h, i: (b, i, h)),
                keys(seq, C_QK_PAD), keys(seq, C_V)]
    args = [q, kc, vv]
    if has_ctx:
        past = ctx[0].shape[1]
        in_specs += [keys(past, C_QK_PAD), keys(past, C_V)]
        args += list(ctx)
    return pl.pallas_call(
        functools.partial(_mla_attn_body, has_ctx), grid=(bsz, C_HEADS, seq // tq), in_specs=in_specs,
        out_specs=pl.BlockSpec((1, tq, C_V), lambda b, h, i: (b, i, h)),
        out_shape=jax.ShapeDtypeStruct((bsz, seq, C_HEADS * C_V), BF16),
        name="mla_attn", compiler_params=_params("parallel", "parallel", "parallel"),
    )(*args)


def _ffn_body(n_tiles, x_ref, xp_ref, xn_ref, sc_ref, sh_ref, g2_ref, g_ref, gp_ref,
              wu_ref, cw_ref, wd_ref, o_ref):
    i = pl.program_id(1)
    tm = x_ref.shape[1]
    g, sc, sh = g_ref[...], sc_ref[0], sh_ref[0]
    x = x_ref[0]
    xh = jnp.concatenate([xp_ref[0], xn_ref[0]], axis=0)
    h = jnp.concatenate([_norm_mod(x, g, sc, sh).astype(BF16), _norm_mod(xh, g, sc, sh).astype(BF16)], axis=0)
    has_up, has_dn = i > 0, i < n_tiles - 1

    def up_proj(f):
        lo = f * FF_TILE
        return _dot(h, wu_ref[:, lo:lo + FF_TILE]), _dot(h, wu_ref[:, D_FF + lo:D_FF + lo + FF_TILE])

    def conv(u, lo):
        up = jnp.where(has_up, u[tm + SUBLANES - 1:tm + SUBLANES], 0.0)
        dn = jnp.where(has_dn, u[tm + SUBLANES:tm + SUBLANES + 1], 0.0)
        return _dwconv3(u[:tm], up, dn, cw_ref[:, lo:lo + FF_TILE])

    nf = D_FF // FF_TILE
    acc = None
    nxt = up_proj(0)
    for f in range(nf):
        ua, ub = nxt
        if f + 1 < nf:
            nxt = up_proj(f + 1)
        lo = f * FF_TILE
        act = (_silu(conv(ua, lo)) * conv(ub, D_FF + lo)).astype(BF16)
        part = _dot(act, wd_ref[lo:lo + FF_TILE, :])
        acc = part if acc is None else acc + part
    o_ref[0] = x + g2_ref[0] * _rms(acc, gp_ref[...])


def _ffn(x, modflat, mod_rows, row_of, layer, g_pre, g_post, w_up, conv_w, w_down):
    bsz, seq, d = x.shape
    tm = 512 if seq % 512 == 0 else 256
    nt = seq // tm
    prev, nxt = _halo_specs(tm, seq)
    in_specs = [pl.BlockSpec((1, tm, d), lambda b, i: (b, i, 0)), prev, nxt,
                _mod_spec(mod_rows, layer, row_of, 4), _mod_spec(mod_rows, layer, row_of, 3),
                _mod_spec(mod_rows, layer, row_of, 5), _const_spec((1, d)), _const_spec((1, d)),
                _const_spec(w_up.shape), _const_spec(conv_w.shape), _const_spec(w_down.shape)]
    return pl.pallas_call(
        functools.partial(_ffn_body, nt), grid=(bsz, nt), in_specs=in_specs,
        out_specs=pl.BlockSpec((1, tm, d), lambda b, i: (b, i, 0)),
        out_shape=jax.ShapeDtypeStruct(x.shape, F32),
        name="conv_ffn", compiler_params=_params("parallel", "parallel"),
    )(x, x, x, modflat, modflat, modflat, g_pre, g_post, w_up, conv_w, w_down)


def _rope_tables(seq):
    t = jnp.arange(seq)
    row = (t // GRID_W).astype(F32)
    col = (t % GRID_W).astype(F32)
    n_freq = A_HEAD_DIM // 4
    inv_freq = ROPE_BASE ** (-jnp.arange(n_freq, dtype=F32) / n_freq)
    ang = jnp.concatenate([row[:, None] * inv_freq, col[:, None] * inv_freq], axis=-1)
    cos, sin = jnp.cos(ang), jnp.sin(ang)
    return (jnp.concatenate([cos, cos, cos, cos], axis=-1), jnp.concatenate([-sin, sin, -sin, sin], axis=-1))


def _lane_vec(p):
    v = jnp.zeros((2, 2, B_HEADS), F32).at[:, 0, :].set(p.astype(F32)).reshape(1, 4 * B_HEADS)
    return jnp.pad(v, ((0, 0), (0, LANES - 4 * B_HEADS)))


def _even_layer(x, layer, j, prm, mod, rope_tabs, ctx):
    modflat, mod_rows, row_of = mod
    w_in = jnp.pad(prm["ev_w_in"][j], ((0, 0), (0, EVEN_IN_PAD - prm["ev_w_in"].shape[-1]))).astype(BF16)
    qa, ka, va, qkvb, gate, gbv, gbt = _even_in(
        x, modflat, mod_rows, row_of, layer, prm["norm_pre"][layer, 0][None], w_in, prm["ev_conv"][j],
        _lane_vec(prm["ev_a_log"][j]), _lane_vec(prm["ev_dt_bias"][j]), rope_tabs)
    if ctx is None:
        oa = _even_attn(qa, ka, va, None, prm["ev_sink"][j])
        o_f, o_b, s_f, s_b = _delta(qkvb, gbv, gbt, None)
    else:
        k_ctx, v_ctx, s0_f, s0_b = ctx
        flat = lambda t: t.reshape(t.shape[0], t.shape[1], A_KV)
        oa = _even_attn(qa, ka, va, (flat(k_ctx), flat(v_ctx)), prm["ev_sink"][j])
        o_f, o_b, s_f, s_b = _delta(qkvb, gbv, gbt, (s0_f, s0_b))
    x = _mixer_out([oa, o_f, o_b, gate], x, modflat, mod_rows, row_of, layer,
                   prm["ev_w_out"][j].astype(BF16), prm["norm_post"][layer, 0][None],
                   prm["ev_out_norm"][j][None])
    return x, (ka, va, s_f, s_b)


def _odd_layer(x, layer, j, prm, mod, rope_tabs, ctx):
    modflat, mod_rows, row_of = mod
    w_in = jnp.pad(prm["od_w_in"][j], ((0, 0), (0, ODD_IN_PAD - prm["od_w_in"].shape[-1]))).astype(BF16)
    w_q = prm["od_w_q_up"][j].reshape(C_Q_LORA, C_HEADS, C_NOPE + C_ROPE)
    w_q = jnp.pad(w_q, ((0, 0), (0, 0), (0, C_QK_PAD - C_NOPE - C_ROPE)))
    w_q = w_q.reshape(C_Q_LORA, C_HEADS * C_QK_PAD).astype(BF16)
    w_kv = prm["od_w_kv_up"][j].reshape(C_KV_LORA, C_HEADS, 2, C_NOPE)
    w_kv = w_kv.transpose(0, 2, 1, 3).reshape(C_KV_LORA, 2 * C_HEADS * C_NOPE).astype(BF16)
    q, ckv, kpe, kn, vv = _odd_in(
        x, modflat, mod_rows, row_of, layer, prm["norm_pre"][layer, 0][None], w_in,
        prm["od_q_norm"][j][None], prm["od_kv_norm"][j][None], w_q, w_kv, rope_tabs)
    if ctx is None:
        o = _mla_attn(q, kn, vv, None)
    else:
        ckv_ctx, kpe_ctx = ctx
        kpe_c = jnp.pad(kpe_ctx, ((0, 0), (0, 0), (0, LANES - C_ROPE)))
        o = _mla_attn(q, kn, vv, _kv_up(ckv_ctx, kpe_c, w_kv))
    x = _mixer_out([o], x, modflat, mod_rows, row_of, layer, prm["od_w_out"][j].astype(BF16),
                   prm["norm_post"][layer, 0][None])
    return x, (ckv, kpe[..., :C_ROPE])


def _trunk(x, prm, mod, rope_tabs, ctxs):
    depth = prm["w_mod"].shape[0]
    modflat, mod_rows, row_of = mod
    new_ctx = []
    for layer in range(depth):
        j = layer // 2
        ctx = None if ctxs is None else ctxs[layer]
        if layer % 2 == 0:
            x, nc = _even_layer(x, layer, j, prm, mod, rope_tabs, ctx)
        else:
            x, nc = _odd_layer(x, layer, j, prm, mod, rope_tabs, ctx)
        new_ctx.append(nc)
        x = _ffn(x, modflat, mod_rows, row_of, layer, prm["norm_pre"][layer, 1][None],
                 prm["norm_post"][layer, 1][None], prm["ffn_w_up"][layer].astype(BF16),
                 prm["ffn_conv"][layer], prm["ffn_w_down"][layer].astype(BF16))
    return x, new_ctx


def kernel(x_prompt, x_sample, cache_attn_k, cache_attn_v, state_delta_fwd, state_delta_bwd,
           cache_mla_ckv, cache_mla_kpe, c, c_ctx, w_mod, b_mod, norm_pre, norm_post,
           ffn_w_up, ffn_conv, ffn_w_down, ev_w_in, ev_conv, ev_a_log, ev_dt_bias, ev_sink,
           ev_out_norm, ev_w_out, od_w_in, od_q_norm, od_kv_norm, od_w_q_up, od_w_kv_up, od_w_out):
    prm = {
        "w_mod": w_mod, "b_mod": b_mod, "norm_pre": norm_pre, "norm_post": norm_post,
        "ffn_w_up": ffn_w_up, "ffn_conv": ffn_conv, "ffn_w_down": ffn_w_down,
        "ev_w_in": ev_w_in, "ev_conv": ev_conv, "ev_a_log": ev_a_log, "ev_dt_bias": ev_dt_bias,
        "ev_sink": ev_sink, "ev_out_norm": ev_out_norm, "ev_w_out": ev_w_out,
        "od_w_in": od_w_in, "od_q_norm": od_q_norm, "od_kv_norm": od_kv_norm,
        "od_w_q_up": od_w_q_up, "od_w_kv_up": od_w_kv_up, "od_w_out": od_w_out,
    }
    depth = w_mod.shape[0]
    n_c = c.shape[0]
    mod_rows = -(-(n_c + 1) // SUBLANES) * SUBLANES
    cvec = jnp.concatenate([c, c_ctx[None], jnp.zeros((mod_rows - n_c - 1, c.shape[1]), F32)], axis=0)
    modflat = _mod_all(cvec, w_mod, b_mod).reshape(depth * mod_rows * N_MOD, 1, D_MODEL)

    y_prompt, pctx = _trunk(x_prompt, prm, (modflat, mod_rows, lambda b: n_c), None, None)

    ctxs = []
    for layer in range(depth):
        j = layer // 2
        if layer % 2 == 0:
            ctxs.append((cache_attn_k[:, j], cache_attn_v[:, j], state_delta_fwd[:, j], state_delta_bwd[:, j]))
        else:
            ctxs.append((cache_mla_ckv[:, j], cache_mla_kpe[:, j]))
    y_sample, _ = _trunk(x_sample, prm, (modflat, mod_rows, lambda b: b), _rope_tables(x_sample.shape[1]), ctxs)

    bsz, seq = x_prompt.shape[:2]
    even, odd = pctx[0::2], pctx[1::2]
    kv_shape = (bsz, len(even), seq, A_KV_HEADS, A_HEAD_DIM)
    new_attn_k = jnp.stack([e[0] for e in even], axis=1).reshape(kv_shape)
    new_attn_v = jnp.stack([e[1] for e in even], axis=1).reshape(kv_shape)
    new_delta_fwd = jnp.stack([e[2] for e in even], axis=1)
    new_delta_bwd = jnp.stack([e[3] for e in even], axis=1)
    new_mla_ckv = jnp.stack([o[0] for o in odd], axis=1)
    new_mla_kpe = jnp.stack([o[1] for o in odd], axis=1)
    return (y_prompt, y_sample, new_attn_k, new_attn_v, new_delta_fwd, new_delta_bwd, new_mla_ckv, new_mla_kpe)
```

```python
import functools
import math

import jax
import jax.numpy as jnp
from jax import lax
from jax.experimental import pallas as pl
from jax.experimental.pallas import tpu as pltpu

F32 = jnp.float32
BF16 = jnp.bfloat16

D_MODEL = 1024
GRID_W = 64
A_HEADS = 8
A_KV_HEADS = 2
A_HEAD_DIM = 64
A_REP = A_HEADS // A_KV_HEADS
WINDOW = 128
Q_BLOCK = 128
B_HEADS = 4
B_DK = 128
B_DV = 128
CHUNK = 64
C_HEADS = 8
C_Q_LORA = 384
C_KV_LORA = 256
C_NOPE = 128
C_ROPE = 64
C_V = 128
C_QK_PAD = 256
D_FF = 2816
ROPE_BASE = 10000.0
EPS = 1e-6
N_MOD = 6

A_Q = A_HEADS * A_HEAD_DIM
A_KV = A_KV_HEADS * A_HEAD_DIM
B_QK = B_HEADS * B_DK
B_VW = B_HEADS * B_DV
B_CONV_CH = 2 * B_QK + B_VW
KA0 = A_Q
VA0 = KA0 + A_KV
QKVB0 = VA0 + A_KV
GATE0 = QKVB0 + B_CONV_CH
AB0 = GATE0 + B_VW
LANES = 128
SUBLANES = 8
EVEN_IN_PAD = AB0 + LANES
ODD_IN_PAD = C_Q_LORA + C_KV_LORA + LANES
FF_TILE = 256
MLA_KEY_CHUNK = 512
DELTA_PREP_CHUNKS = 2
VMEM_LIMIT_BYTES = 56 * 1024 * 1024


def _params(*sem):
    return pltpu.CompilerParams(dimension_semantics=sem, vmem_limit_bytes=VMEM_LIMIT_BYTES)


def _dot(a, b):
    return jnp.dot(a, b, preferred_element_type=F32)


def _dot_nt(a, b):
    return lax.dot_general(a, b, (((1,), (1,)), ((), ())), preferred_element_type=F32)


def _sigmoid(x):
    return 1.0 / (1.0 + jnp.exp(-x))


def _silu(x):
    return x * _sigmoid(x)


def _softplus(x):
    return jnp.maximum(x, 0.0) + jnp.log(1.0 + jnp.exp(-jnp.abs(x)))


def _rms(x, g):
    return x * lax.rsqrt(jnp.mean(x * x, axis=-1, keepdims=True) + EPS) * g


def _norm_mod(x, g, scale, shift):
    return _rms(x, g) * (1.0 + scale) + shift


def _rope(x, cos_t, sin_t):
    lane = lax.broadcasted_iota(jnp.int32, x.shape, 1)
    swapped = jnp.where((lane & 63) < 32, pltpu.roll(x, LANES - 32, 1), pltpu.roll(x, 32, 1))
    return x * cos_t + swapped * sin_t


def _dwconv3(u, up, dn, cw):
    tm = u.shape[0]
    row = lax.broadcasted_iota(jnp.int32, u.shape, 0)
    u_prev = jnp.where(row == 0, up, pltpu.roll(u, 1, 0))
    u_next = jnp.where(row == tm - 1, dn, pltpu.roll(u, tm - 1, 0))
    return cw[0:1] * u_prev + cw[1:2] * u + cw[2:3] * u_next


def _halo_specs(tm, seq):
    per = tm // SUBLANES
    last = seq // SUBLANES - 1
    prev = pl.BlockSpec((1, SUBLANES, D_MODEL), lambda b, i, *_: (b, jnp.maximum(i * per - 1, 0), 0))
    nxt = pl.BlockSpec((1, SUBLANES, D_MODEL), lambda b, i, *_: (b, jnp.minimum((i + 1) * per, last), 0))
    return prev, nxt


def _mod_spec(mod_rows, layer, row_of, k):
    return pl.BlockSpec((1, 1, D_MODEL),
                        lambda b, *_: ((layer * mod_rows + row_of(b)) * N_MOD + k, 0, 0))


def _const_spec(shape):
    nd = len(shape)
    return pl.BlockSpec(shape, lambda *_: (0,) * nd)


def _mod_body(c_ref, w_ref, b_ref, o_ref):
    s = _silu(c_ref[...]).astype(BF16)
    o_ref[0] = _dot(s, w_ref[0].astype(BF16)) + b_ref[0]


def _mod_all(cvec, w_mod, b_mod):
    depth, d, n = w_mod.shape
    rows = cvec.shape[0]
    tn = n // 4
    return pl.pallas_call(
        _mod_body, grid=(depth, n // tn),
        in_specs=[pl.BlockSpec((rows, d), lambda l, j: (0, 0)),
                  pl.BlockSpec((1, d, tn), lambda l, j: (l, 0, j)),
                  pl.BlockSpec((1, 1, tn), lambda l, j: (l, 0, j))],
        out_specs=pl.BlockSpec((1, rows, tn), lambda l, j: (l, 0, j)),
        out_shape=jax.ShapeDtypeStruct((depth, rows, n), F32),
        name="mod_vectors", compiler_params=_params("parallel", "parallel"),
    )(cvec, w_mod, b_mod.reshape(depth, 1, n))


def _even_in_body(rope, n_tiles, x_ref, xp_ref, xn_ref, sc_ref, sh_ref, g_ref, w_ref, cw_ref,
                  alog_ref, dt_ref, *rest):
    if rope:
        cos_ref, sin_ref = rest[:2]
        rest = rest[2:]
    qa_ref, ka_ref, va_ref, qkv_ref, gate_ref, gb_ref, gbt_ref = rest
    i = pl.program_id(1)
    tm = x_ref.shape[1]
    g, sc, sh = g_ref[...], sc_ref[0], sh_ref[0]
    h = _norm_mod(x_ref[0], g, sc, sh).astype(BF16)
    y = _dot(h, w_ref[...])
    xh = jnp.concatenate([xp_ref[0], xn_ref[0]], axis=0)
    hh = _norm_mod(xh, g, sc, sh).astype(BF16)
    yh = _dot(hh, w_ref[:, QKVB0:GATE0])
    up = jnp.where(i > 0, yh[SUBLANES - 1:SUBLANES], 0.0)
    dn = jnp.where(i < n_tiles - 1, yh[SUBLANES:SUBLANES + 1], 0.0)

    qa = y[:, :A_Q]
    ka = y[:, KA0:VA0]
    if rope:
        cos_t, sin_t = cos_ref[...], sin_ref[...]
        qa = jnp.concatenate([_rope(qa[:, j * LANES:(j + 1) * LANES], cos_t, sin_t)
                              for j in range(A_Q // LANES)], axis=1)
        ka = _rope(ka, cos_t, sin_t)
    qa_ref[0] = qa
    ka_ref[0] = ka
    va_ref[0] = y[:, VA0:QKVB0]
    gate_ref[0] = y[:, GATE0:AB0]

    s = _silu(_dwconv3(y[:, QKVB0:GATE0], up, dn, cw_ref[...]))
    for j in range(B_CONV_CH // LANES):
        seg = s[:, j * LANES:(j + 1) * LANES]
        if j < 2 * B_HEADS:
            seg = seg * lax.rsqrt(jnp.sum(seg * seg, axis=-1, keepdims=True) + EPS)
        qkv_ref[0, :, j * LANES:(j + 1) * LANES] = seg

    ab = y[:, AB0:EVEN_IN_PAD]
    lane = lax.broadcasted_iota(jnp.int32, ab.shape, 1)
    gdec = -jnp.exp(alog_ref[...]) * _softplus(ab + dt_ref[...])
    gb = jnp.where((lane & B_HEADS) == 0, gdec, _sigmoid(ab))
    gb_ref[0] = gb
    gbt = gb.T
    for c in range(tm // CHUNK):
        gbt_ref[0, c] = gbt[:4 * B_HEADS, c * CHUNK:(c + 1) * CHUNK]


def _even_in(x, modflat, mod_rows, row_of, layer, g_pre, w_in, conv_w, alog_vec, dt_vec, rope_tabs):
    bsz, seq, d = x.shape
    tm = 512 if seq % 512 == 0 else 256
    nt = seq // tm
    rope = rope_tabs is not None
    prev, nxt = _halo_specs(tm, seq)
    in_specs = [pl.BlockSpec((1, tm, d), lambda b, i: (b, i, 0)), prev, nxt,
                _mod_spec(mod_rows, layer, row_of, 1), _mod_spec(mod_rows, layer, row_of, 0),
                _const_spec((1, d)), _const_spec(w_in.shape), _const_spec(conv_w.shape),
                _const_spec((1, LANES)), _const_spec((1, LANES))]
    args = [x, x, x, modflat, modflat, g_pre, w_in, conv_w, alog_vec, dt_vec]
    if rope:
        in_specs += [pl.BlockSpec((tm, LANES), lambda b, i: (i, 0))] * 2
        args += list(rope_tabs)

    def out(width, dtype=F32):
        return (pl.BlockSpec((1, tm, width), lambda b, i: (b, i, 0)),
                jax.ShapeDtypeStruct((bsz, seq, width), dtype))

    outs = [out(A_Q), out(A_KV), out(A_KV), out(B_CONV_CH), out(B_VW), out(LANES),
            (pl.BlockSpec((1, tm // CHUNK, 4 * B_HEADS, CHUNK), lambda b, i: (b, i, 0, 0)),
             jax.ShapeDtypeStruct((bsz, seq // CHUNK, 4 * B_HEADS, CHUNK), F32))]
    return pl.pallas_call(
        functools.partial(_even_in_body, rope, nt), grid=(bsz, nt), in_specs=in_specs,
        out_specs=[o[0] for o in outs], out_shape=[o[1] for o in outs],
        name="even_in", compiler_params=_params("parallel", "parallel"),
    )(*args)


def _even_attn_body(windowed, q_ref, k_ref, v_ref, *rest):
    if windowed:
        kc_ref, vc_ref, sink_ref, o_ref = rest
    else:
        sink_ref, o_ref = rest
    i = pl.program_id(1)
    seq = k_ref.shape[1]
    q_all = q_ref[0] * (A_HEAD_DIM ** -0.5)
    rows = A_REP * Q_BLOCK
    if windowed:
        band = Q_BLOCK + 2 * WINDOW
        k0 = pl.multiple_of(jnp.clip(i * Q_BLOCK - WINDOW, 0, seq - band), Q_BLOCK)
        k_all = k_ref[0, pl.ds(k0, band), :].astype(BF16)
        v_all = v_ref[0, pl.ds(k0, band), :].astype(BF16)
        kc_all = kc_ref[0].astype(BF16)
        vc_all = vc_ref[0].astype(BF16)
        qpos = i * Q_BLOCK + (lax.broadcasted_iota(jnp.int32, (rows, band), 0) & (Q_BLOCK - 1))
        kpos = k0 + lax.broadcasted_iota(jnp.int32, (rows, band), 1)
        valid = jnp.abs(qpos - kpos) <= WINDOW
    else:
        k_all = k_ref[0].astype(BF16)
        v_all = v_ref[0].astype(BF16)
    rid = lax.broadcasted_iota(jnp.int32, (rows, 1), 0) // Q_BLOCK
    groups = range(A_KV_HEADS)
    cols = [slice(g * A_HEAD_DIM, (g + 1) * A_HEAD_DIM) for g in groups]
    q, sink = [], []
    for g in groups:
        heads = range(g * A_REP, (g + 1) * A_REP)
        q.append(jnp.concatenate([q_all[:, h * A_HEAD_DIM:(h + 1) * A_HEAD_DIM] for h in heads],
                                 axis=0).astype(BF16))
        sk = jnp.zeros((rows, 1), F32)
        for r, h in enumerate(heads):
            sk = jnp.where(rid == r, sink_ref[h], sk)
        sink.append(sk)
    s = [_dot_nt(q[g], k_all[:, cols[g]]) for g in groups]
    if windowed:
        s = [jnp.where(valid, s[g], -jnp.inf) for g in groups]
        s_c = [_dot_nt(q[g], kc_all[:, cols[g]]) for g in groups]
        m = [jnp.maximum(jnp.max(s_c[g], axis=-1, keepdims=True), sink[g]) for g in groups]
    else:
        m = sink
    m = [jnp.maximum(jnp.max(s[g], axis=-1, keepdims=True), m[g]) for g in groups]
    e = [jnp.exp(s[g] - m[g]) for g in groups]
    den = [jnp.sum(e[g], axis=-1, keepdims=True) + jnp.exp(sink[g] - m[g]) for g in groups]
    o = [_dot(e[g].astype(BF16), v_all[:, cols[g]]) for g in groups]
    if windowed:
        e_c = [jnp.exp(s_c[g] - m[g]) for g in groups]
        den = [den[g] + jnp.sum(e_c[g], axis=-1, keepdims=True) for g in groups]
        o = [o[g] + _dot(e_c[g].astype(BF16), vc_all[:, cols[g]]) for g in groups]
    outs = []
    for g in groups:
        og = o[g] / den[g]
        outs += [og[r * Q_BLOCK:(r + 1) * Q_BLOCK] for r in range(A_REP)]
    o_ref[0] = jnp.concatenate(outs, axis=1).astype(o_ref.dtype)


def _even_attn(qa, ka, va, ctx, sink):
    bsz, seq, _ = qa.shape
    windowed = ctx is not None
    in_specs = [pl.BlockSpec((1, Q_BLOCK, A_Q), lambda b, i: (b, i, 0)),
                pl.BlockSpec((1, seq, A_KV), lambda b, i: (b, 0, 0)),
                pl.BlockSpec((1, seq, A_KV), lambda b, i: (b, 0, 0))]
    args = [qa, ka, va]
    if windowed:
        past = ctx[0].shape[1]
        in_specs += [pl.BlockSpec((1, past, A_KV), lambda b, i: (b, 0, 0))] * 2
        args += list(ctx)
    in_specs.append(pl.BlockSpec(memory_space=pltpu.SMEM))
    args.append(sink)
    return pl.pallas_call(
        functools.partial(_even_attn_body, windowed), grid=(bsz, seq // Q_BLOCK), in_specs=in_specs,
        out_specs=pl.BlockSpec((1, Q_BLOCK, A_Q), lambda b, i: (b, i, 0)),
        out_shape=jax.ShapeDtypeStruct((bsz, seq, A_Q), BF16),
        name="even_attn", compiler_params=_params("parallel", "parallel"),
    )(*args)


def _bf16_terms(x, n):
    terms = []
    for _ in range(n):
        t = x.astype(BF16).astype(F32)
        terms.append(t)
        x = x - t
    return terms


def _dot_split(a, b):
    ah, al = _bf16_terms(a, 2)
    bh, bl = _bf16_terms(b, 2)
    return _dot(jnp.concatenate([ah, al, ah, al], axis=1).astype(BF16),
                jnp.concatenate([bh, bh, bl, bl], axis=0).astype(BF16))


def _mask_dot(mask, x, mask_first):
    terms = _bf16_terms(x, 3)
    m16 = mask.astype(BF16)
    if mask_first:
        return _dot(jnp.concatenate([m16] * 3, axis=1), jnp.concatenate(terms, axis=0).astype(BF16))
    return _dot(jnp.concatenate(terms, axis=1).astype(BF16), jnp.concatenate([m16] * 3, axis=0))


def _delta_chunk_terms(chains, eye):
    n = range(len(chains))
    q, k, v, gcol, grow, bcol, incl, strict, g_last = zip(*chains)
    decay = [jnp.where(incl[i], jnp.exp(jnp.where(incl[i], gcol[i] - grow[i], 0.0)), 0.0) for i in n]
    qs = [q[i] * (B_DK ** -0.5) for i in n]
    kbeta = [k[i] * bcol[i] for i in n]
    both = [_dot_nt(jnp.concatenate([kbeta[i], qs[i]], axis=0).astype(BF16), k[i].astype(BF16)) for i in n]
    qk = [jnp.where(incl[i], both[i][CHUNK:] * decay[i], 0.0).astype(BF16) for i in n]
    p = [-jnp.where(strict[i], both[i][:CHUNK] * decay[i], 0.0) for i in n]
    t_inv = [eye + p[i] for i in n]
    for _ in range(int(math.log2(CHUNK)) - 1):
        p = [_dot_split(p[i], p[i]) for i in n]
        t_inv = [t_inv[i] + _dot_split(t_inv[i], p[i]) for i in n]
    eg = [jnp.exp(gcol[i]) for i in n]
    rhs = [jnp.concatenate([v[i] * bcol[i], kbeta[i] * eg[i]], axis=1).astype(BF16) for i in n]
    uw = [_dot(t_inv[i].astype(BF16), rhs[i]).astype(BF16) for i in n]
    kd = [(k[i] * jnp.exp(g_last[i] - gcol[i])).T.astype(BF16) for i in n]
    qk_uw = [_dot(qk[i], uw[i]) for i in n]
    cp = [_dot(kd[i], uw[i]) for i in n]
    out = []
    for i in n:
        q_eff = qs[i] * eg[i] - qk_uw[i][:, B_DV:]
        pq = jnp.concatenate([cp[i][:, B_DV:], q_eff], axis=0).astype(BF16)
        out.append((pq, cp[i][:, :B_DV], qk_uw[i][:, :B_DV], jnp.exp(g_last[i])))
    return out


def _delta_body(zero_init, qf_ref, qb_ref, gf_ref, gb_ref, gtf_ref, gtb_ref, *rest):
    if not zero_init:
        s0f_ref, s0b_ref = rest[:2]
        rest = rest[2:]
    of_ref, ob_ref, sfo_ref, sbo_ref, s_scr, pq_scr, c_scr, oi_scr, dec_scr = rest
    n = pl.program_id(1)
    n_chunks = gtf_ref.shape[1]
    n_chains = 2 * B_HEADS

    @pl.when(n == 0)
    def _():
        if zero_init:
            s_scr[...] = jnp.zeros_like(s_scr)
        else:
            s_scr[:B_HEADS] = s0f_ref[0]
            s_scr[B_HEADS:] = s0b_ref[0]

    ii = lax.broadcasted_iota(jnp.int32, (CHUNK, CHUNK), 0)
    jj = lax.broadcasted_iota(jnp.int32, (CHUNK, CHUNK), 1)
    lower, upper = ii >= jj, ii <= jj
    strict_lower, strict_upper = ii > jj, ii < jj
    tri_l, tri_u = lower.astype(F32), upper.astype(F32)
    eye = (ii == jj).astype(F32)

    def chunk_chains(c):
        rows = pl.ds(pl.multiple_of(c * CHUNK, CHUNK), CHUNK)
        gf, gb = gf_ref[0, rows, :], gb_ref[0, rows, :]
        gcf, gcb = _mask_dot(tri_l, gf, True), _mask_dot(tri_u, gb, True)
        gcf_t, gcb_t = _mask_dot(tri_u, gtf_ref[0, c], False), _mask_dot(tri_l, gtb_ref[0, c], False)
        chains = []
        for h in range(B_HEADS):
            cq = slice(h * B_DK, (h + 1) * B_DK)
            ck = slice(B_QK + h * B_DK, B_QK + (h + 1) * B_DK)
            cv = slice(2 * B_QK + h * B_DV, 2 * B_QK + (h + 1) * B_DV)
            lg, lb = h, B_HEADS + h
            gcol = gcf[:, lg:lg + 1]
            chains.append((qf_ref[0, rows, cq], qf_ref[0, rows, ck], qf_ref[0, rows, cv], gcol,
                           gcf_t[lg:lg + 1, :], gf[:, lb:lb + 1], lower, strict_lower, gcol[CHUNK - 1:CHUNK]))
        for h in range(B_HEADS):
            cq = slice(h * B_DK, (h + 1) * B_DK)
            ck = slice(B_QK + h * B_DK, B_QK + (h + 1) * B_DK)
            cv = slice(2 * B_QK + h * B_DV, 2 * B_QK + (h + 1) * B_DV)
            lg, lb = 2 * B_HEADS + h, 3 * B_HEADS + h
            gcol = gcb[:, lg:lg + 1]
            chains.append((qb_ref[0, rows, cq], qb_ref[0, rows, ck], qb_ref[0, rows, cv], gcol,
                           gcb_t[lg:lg + 1, :], gb[:, lb:lb + 1], upper, strict_upper, gcol[0:1]))
        return chains

    def prepare(pair, carry):
        cs = [DELTA_PREP_CHUNKS * pair + k for k in range(DELTA_PREP_CHUNKS)]
        chains = [ch for c in cs for ch in chunk_chains(c)]
        for j, (pq, cc, oi, dec) in enumerate(_delta_chunk_terms(chains, eye)):
            c, idx = cs[j // n_chains], j % n_chains
            pq_scr[c, idx] = pq
            c_scr[c, idx] = cc
            oi_scr[c, idx] = oi
            dec_scr[c, idx] = jnp.broadcast_to(dec, (SUBLANES, LANES))
        return carry

    lax.fori_loop(0, n_chunks // DELTA_PREP_CHUNKS, prepare, 0)

    def advance(t, carry):
        cf, cb = t, n_chunks - 1 - t
        outs, states = [], []
        for idx in range(n_chains):
            c = cf if idx < B_HEADS else cb
            s = s_scr[idx]
            r = _dot(pq_scr[c, idx], s.astype(BF16))
            outs.append(r[B_DK:] + oi_scr[c, idx])
            states.append(dec_scr[c, idx][0:1, :] * s + c_scr[c, idx] - r[:B_DK])
        of_ref[0, pl.ds(pl.multiple_of(cf * CHUNK, CHUNK), CHUNK), :] = jnp.concatenate(outs[:B_HEADS], axis=1)
        ob_ref[0, pl.ds(pl.multiple_of(cb * CHUNK, CHUNK), CHUNK), :] = jnp.concatenate(outs[B_HEADS:], axis=1)
        for idx in range(n_chains):
            s_scr[idx] = states[idx]
        return carry

    lax.fori_loop(0, n_chunks, advance, 0)

    @pl.when(n == pl.num_programs(1) - 1)
    def _():
        sfo_ref[0] = s_scr[:B_HEADS]
        sbo_ref[0] = s_scr[B_HEADS:]


def _delta(qkvb, gbv, gbt, states):
    bsz, seq, _ = qkvb.shape
    per = 4
    tb = per * CHUNK
    nb = seq // tb
    zero_init = states is None
    n_chains = 2 * B_HEADS

    def fwd(*tail):
        return lambda b, n: (b, n) + tail

    def bwd(*tail):
        return lambda b, n: (b, nb - 1 - n) + tail

    st_spec = pl.BlockSpec((1, B_HEADS, B_DK, B_DV), lambda b, n: (b, 0, 0, 0))
    in_specs = [pl.BlockSpec((1, tb, B_CONV_CH), fwd(0)), pl.BlockSpec((1, tb, B_CONV_CH), bwd(0)),
                pl.BlockSpec((1, tb, LANES), fwd(0)), pl.BlockSpec((1, tb, LANES), bwd(0)),
                pl.BlockSpec((1, per, 4 * B_HEADS, CHUNK), fwd(0, 0)),
                pl.BlockSpec((1, per, 4 * B_HEADS, CHUNK), bwd(0, 0))]
    args = [qkvb, qkvb, gbv, gbv, gbt, gbt]
    if not zero_init:
        in_specs += [st_spec, st_spec]
        args += list(states)
    st_shape = jax.ShapeDtypeStruct((bsz, B_HEADS, B_DK, B_DV), F32)
    o_shape = jax.ShapeDtypeStruct((bsz, seq, B_VW), F32)
    return pl.pallas_call(
        functools.partial(_delta_body, zero_init), grid=(bsz, nb), in_specs=in_specs,
        out_specs=[pl.BlockSpec((1, tb, B_VW), fwd(0)), pl.BlockSpec((1, tb, B_VW), bwd(0)),
                   st_spec, st_spec],
        out_shape=[o_shape, o_shape, st_shape, st_shape],
        scratch_shapes=[pltpu.VMEM((n_chains, B_DK, B_DV), F32),
                        pltpu.VMEM((per, n_chains, B_DK + CHUNK, B_DV), BF16),
                        pltpu.VMEM((per, n_chains, B_DK, B_DV), F32),
                        pltpu.VMEM((per, n_chains, CHUNK, B_DV), F32),
                        pltpu.VMEM((per, n_chains, SUBLANES, LANES), F32)],
        name="delta_rule", compiler_params=_params("parallel", "arbitrary"),
    )(*args)


def _even_out_body(oa_ref, of_ref, ob_ref, gate_ref, x_ref, g1_ref, on_ref, w_ref, gp_ref, o_ref):
    ob = of_ref[0] + ob_ref[0]
    on = on_ref[...]
    segs = [_rms(ob[:, h * B_DV:(h + 1) * B_DV], on) for h in range(B_HEADS)]
    obg = (jnp.concatenate(segs, axis=1) * _silu(gate_ref[0])).astype(BF16)
    mix = _dot(oa_ref[0], w_ref[:A_Q, :]) + _dot(obg, w_ref[A_Q:, :])
    o_ref[0] = x_ref[0] + g1_ref[0] * _rms(mix, gp_ref[...])


def _odd_out_body(a_ref, x_ref, g1_ref, w_ref, gp_ref, o_ref):
    mix = _dot(a_ref[0], w_ref[...])
    o_ref[0] = x_ref[0] + g1_ref[0] * _rms(mix, gp_ref[...])


def _mixer_out(acts, x, modflat, mod_rows, row_of, layer, w_out, g_post, out_norm=None):
    bsz, seq, d = x.shape
    tm = 512 if seq % 512 == 0 else 256

    def row_spec(width):
        return pl.BlockSpec((1, tm, width), lambda b, i: (b, i, 0))

    in_specs = [row_spec(a.shape[-1]) for a in acts] + [row_spec(d), _mod_spec(mod_rows, layer, row_of, 2)]
    args = list(acts) + [x, modflat]
    if out_norm is not None:
        in_specs.append(_const_spec(out_norm.shape))
        args.append(out_norm)
    in_specs += [_const_spec(w_out.shape), _const_spec((1, d))]
    args += [w_out, g_post]
    return pl.pallas_call(
        _even_out_body if out_norm is not None else _odd_out_body,
        grid=(bsz, seq // tm), in_specs=in_specs, out_specs=row_spec(d),
        out_shape=jax.ShapeDtypeStruct(x.shape, F32),
        name="mixer_out", compiler_params=_params("parallel", "parallel"),
    )(*args)


def _write_keys_values(kv, kpe, k_ref, v_ref):
    kpe16 = kpe.astype(BF16)
    for hd in range(C_HEADS):
        base = hd * C_QK_PAD
        k_ref[0, :, base:base + C_NOPE] = kv[:, hd * C_NOPE:(hd + 1) * C_NOPE].astype(BF16)
        k_ref[0, :, base + C_NOPE:base + C_QK_PAD] = kpe16
    v_ref[0] = kv[:, C_HEADS * C_NOPE:].astype(BF16)


def _odd_in_body(rope, x_ref, sc_ref, sh_ref, g_ref, w_ref, qn_ref, kvn_ref, wq_ref, wkv_ref, *rest):
    if rope:
        cos_ref, sin_ref = rest[:2]
        rest = rest[2:]
    q_ref, ckv_ref, kpe_ref, kn_ref, vv_ref = rest
    h = _norm_mod(x_ref[0], g_ref[...], sc_ref[0], sh_ref[0]).astype(BF16)
    y = _dot(h, w_ref[...])
    cq = _rms(y[:, :C_Q_LORA], qn_ref[...]).astype(BF16)
    q = _dot(cq, wq_ref[...])
    ckv = _rms(y[:, C_Q_LORA:C_Q_LORA + C_KV_LORA], kvn_ref[...])
    kpe = y[:, C_Q_LORA + C_KV_LORA:]
    scale = (C_NOPE + C_ROPE) ** -0.5
    if rope:
        cos_t, sin_t = cos_ref[...], sin_ref[...]
        kpe = _rope(kpe, cos_t, sin_t)
    for hd in range(C_HEADS):
        base = hd * C_QK_PAD
        q_ref[0, :, base:base + C_NOPE] = (q[:, base:base + C_NOPE] * scale).astype(BF16)
        qpe = q[:, base + C_NOPE:base + C_QK_PAD]
        if rope:
            qpe = _rope(qpe, cos_t, sin_t)
        q_ref[0, :, base + C_NOPE:base + C_QK_PAD] = (qpe * scale).astype(BF16)
    ckv_ref[0] = ckv
    kpe_ref[0] = kpe
    _write_keys_values(_dot(ckv.astype(BF16), wkv_ref[...]), kpe, kn_ref, vv_ref)


def _odd_in(x, modflat, mod_rows, row_of, layer, g_pre, w_in, q_norm, kv_norm, w_q, w_kv, rope_tabs):
    bsz, seq, d = x.shape
    tm = 512 if seq % 512 == 0 else 256
    rope = rope_tabs is not None
    in_specs = [pl.BlockSpec((1, tm, d), lambda b, i: (b, i, 0)),
                _mod_spec(mod_rows, layer, row_of, 1), _mod_spec(mod_rows, layer, row_of, 0),
                _const_spec((1, d)), _const_spec(w_in.shape), _const_spec(q_norm.shape),
                _const_spec(kv_norm.shape), _const_spec(w_q.shape), _const_spec(w_kv.shape)]
    args = [x, modflat, modflat, g_pre, w_in, q_norm, kv_norm, w_q, w_kv]
    if rope:
        in_specs += [pl.BlockSpec((tm, LANES), lambda b, i: (i, 0))] * 2
        args += list(rope_tabs)

    def out(width, dtype):
        return (pl.BlockSpec((1, tm, width), lambda b, i: (b, i, 0)),
                jax.ShapeDtypeStruct((bsz, seq, width), dtype))

    outs = [out(C_HEADS * C_QK_PAD, BF16), out(C_KV_LORA, F32), out(LANES, F32),
            out(C_HEADS * C_QK_PAD, BF16), out(C_HEADS * C_V, BF16)]
    return pl.pallas_call(
        functools.partial(_odd_in_body, rope), grid=(bsz, seq // tm), in_specs=in_specs,
        out_specs=[o[0] for o in outs], out_shape=[o[1] for o in outs],
        name="odd_in", compiler_params=_params("parallel", "parallel"),
    )(*args)


def _kv_up_body(ckv_ref, kpe_ref, wkv_ref, kn_ref, vv_ref):
    _write_keys_values(_dot(ckv_ref[0].astype(BF16), wkv_ref[...]), kpe_ref[0], kn_ref, vv_ref)


def _kv_up(ckv, kpe, w_kv):
    bsz, n, r = ckv.shape

    def rows(width):
        return pl.BlockSpec((1, n, width), lambda b: (b, 0, 0))

    return pl.pallas_call(
        _kv_up_body, grid=(bsz,),
        in_specs=[rows(r), rows(LANES), _const_spec(w_kv.shape)],
        out_specs=[rows(C_HEADS * C_QK_PAD), rows(C_HEADS * C_V)],
        out_shape=[jax.ShapeDtypeStruct((bsz, n, C_HEADS * C_QK_PAD), BF16),
                   jax.ShapeDtypeStruct((bsz, n, C_HEADS * C_V), BF16)],
        name="kv_up", compiler_params=_params("parallel"),
    )(ckv, kpe, w_kv)


def _mla_attn_body(has_ctx, q_ref, k_ref, v_ref, *rest):
    if has_ctx:
        kc_ref, vc_ref, o_ref = rest
        sources = [(kc_ref, vc_ref), (k_ref, v_ref)]
    else:
        (o_ref,) = rest
        sources = [(k_ref, v_ref)]
    q = q_ref[0]
    chunks = []
    for kr, vr in sources:
        n = kr.shape[1]
        step = min(MLA_KEY_CHUNK, n)
        chunks += [(kr, vr, lo, step) for lo in range(0, n, step)]

    def scores(j):
        kr, _, lo, step = chunks[j]
        return _dot_nt(q, kr[0, lo:lo + step, :])

    nxt = scores(0)
    for j, (_, vr, lo, step) in enumerate(chunks):
        s = nxt
        if j + 1 < len(chunks):
            nxt = scores(j + 1)
        top = jnp.max(s, axis=-1, keepdims=True)
        m_new = top if j == 0 else jnp.maximum(m, top)
        e = jnp.exp(s - m_new)
        d = jnp.sum(e, axis=-1, keepdims=True)
        part = _dot(e.astype(BF16), vr[0, lo:lo + step, :])
        if j == 0:
            den, o = d, part
        else:
            alpha = jnp.exp(m - m_new)
            den, o = alpha * den + d, alpha * o + part
        m = m_new
    o_ref[0] = (o / den).astype(o_ref.dtype)


def _mla_attn(q, kc, vv, ctx):
    bsz, seq, _ = q.shape
    tq = 512 if seq % 512 == 0 else 256
    has_ctx = ctx is not None

    def keys(n, width):
        return pl.BlockSpec((1, n, width), lambda b, h, i: (b, 0, h))

    in_specs = [pl.BlockSpec((1, tq, C_QK_PAD), lambda b, h, i: (b, i, h)),
                keys(seq, C_QK_PAD), keys(seq, C_V)]
    args = [q, kc, vv]
    if has_ctx:
        past = ctx[0].shape[1]
        in_specs += [keys(past, C_QK_PAD), keys(past, C_V)]
        args += list(ctx)
    return pl.pallas_call(
        functools.partial(_mla_attn_body, has_ctx), grid=(bsz, C_HEADS, seq // tq), in_specs=in_specs,
        out_specs=pl.BlockSpec((1, tq, C_V), lambda b, h, i: (b, i, h)),
        out_shape=jax.ShapeDtypeStruct((bsz, seq, C_HEADS * C_V), BF16),
        name="mla_attn", compiler_params=_params("parallel", "parallel", "parallel"),
    )(*args)


def _ffn_body(n_tiles, x_ref, xp_ref, xn_ref, sc_ref, sh_ref, g2_ref, g_ref, gp_ref,
              wu_ref, cw_ref, wd_ref, o_ref):
    i = pl.program_id(1)
    tm = x_ref.shape[1]
    g, sc, sh = g_ref[...], sc_ref[0], sh_ref[0]
    x = x_ref[0]
    xh = jnp.concatenate([xp_ref[0], xn_ref[0]], axis=0)
    h = jnp.concatenate([_norm_mod(x, g, sc, sh).astype(BF16), _norm_mod(xh, g, sc, sh).astype(BF16)], axis=0)
    has_up, has_dn = i > 0, i < n_tiles - 1

    def up_proj(f):
        lo = f * FF_TILE
        return _dot(h, wu_ref[:, lo:lo + FF_TILE]), _dot(h, wu_ref[:, D_FF + lo:D_FF + lo + FF_TILE])

    def conv(u, lo):
        up = jnp.where(has_up, u[tm + SUBLANES - 1:tm + SUBLANES], 0.0)
        dn = jnp.where(has_dn, u[tm + SUBLANES:tm + SUBLANES + 1], 0.0)
        return _dwconv3(u[:tm], up, dn, cw_ref[:, lo:lo + FF_TILE])

    nf = D_FF // FF_TILE
    acts = []
    nxt = up_proj(0)
    for f in range(nf):
        ua, ub = nxt
        if f + 1 < nf:
            nxt = up_proj(f + 1)
        lo = f * FF_TILE
        acts.append((_silu(conv(ua, lo)) * conv(ub, D_FF + lo)).astype(BF16))
    mix = _dot(jnp.concatenate(acts, axis=1), wd_ref[...])
    o_ref[0] = x + g2_ref[0] * _rms(mix, gp_ref[...])


def _ffn(x, modflat, mod_rows, row_of, layer, g_pre, g_post, w_up, conv_w, w_down):
    bsz, seq, d = x.shape
    tm = 512 if seq % 512 == 0 else 256
    nt = seq // tm
    prev, nxt = _halo_specs(tm, seq)
    in_specs = [pl.BlockSpec((1, tm, d), lambda b, i: (b, i, 0)), prev, nxt,
                _mod_spec(mod_rows, layer, row_of, 4), _mod_spec(mod_rows, layer, row_of, 3),
                _mod_spec(mod_rows, layer, row_of, 5), _const_spec((1, d)), _const_spec((1, d)),
                _const_spec(w_up.shape), _const_spec(conv_w.shape), _const_spec(w_down.shape)]
    return pl.pallas_call(
        functools.partial(_ffn_body, nt), grid=(bsz, nt), in_specs=in_specs,
        out_specs=pl.BlockSpec((1, tm, d), lambda b, i: (b, i, 0)),
        out_shape=jax.ShapeDtypeStruct(x.shape, F32),
        name="conv_ffn", compiler_params=_params("parallel", "parallel"),
    )(x, x, x, modflat, modflat, modflat, g_pre, g_post, w_up, conv_w, w_down)


def _rope_tables(seq):
    t = jnp.arange(seq)
    row = (t // GRID_W).astype(F32)
    col = (t % GRID_W).astype(F32)
    n_freq = A_HEAD_DIM // 4
    inv_freq = ROPE_BASE ** (-jnp.arange(n_freq, dtype=F32) / n_freq)
    ang = jnp.concatenate([row[:, None] * inv_freq, col[:, None] * inv_freq], axis=-1)
    cos, sin = jnp.cos(ang), jnp.sin(ang)
    return (jnp.concatenate([cos, cos, cos, cos], axis=-1), jnp.concatenate([-sin, sin, -sin, sin], axis=-1))


def _lane_vec(p):
    v = jnp.zeros((2, 2, B_HEADS), F32).at[:, 0, :].set(p.astype(F32)).reshape(1, 4 * B_HEADS)
    return jnp.pad(v, ((0, 0), (0, LANES - 4 * B_HEADS)))


def _even_layer(x, layer, j, prm, mod, rope_tabs, ctx):
    modflat, mod_rows, row_of = mod
    w_in = jnp.pad(prm["ev_w_in"][j], ((0, 0), (0, EVEN_IN_PAD - prm["ev_w_in"].shape[-1]))).astype(BF16)
    qa, ka, va, qkvb, gate, gbv, gbt = _even_in(
        x, modflat, mod_rows, row_of, layer, prm["norm_pre"][layer, 0][None], w_in, prm["ev_conv"][j],
        _lane_vec(prm["ev_a_log"][j]), _lane_vec(prm["ev_dt_bias"][j]), rope_tabs)
    if ctx is None:
        oa = _even_attn(qa, ka, va, None, prm["ev_sink"][j])
        o_f, o_b, s_f, s_b = _delta(qkvb, gbv, gbt, None)
    else:
        k_ctx, v_ctx, s0_f, s0_b = ctx
        flat = lambda t: t.reshape(t.shape[0], t.shape[1], A_KV)
        oa = _even_attn(qa, ka, va, (flat(k_ctx), flat(v_ctx)), prm["ev_sink"][j])
        o_f, o_b, s_f, s_b = _delta(qkvb, gbv, gbt, (s0_f, s0_b))
    x = _mixer_out([oa, o_f, o_b, gate], x, modflat, mod_rows, row_of, layer,
                   prm["ev_w_out"][j].astype(BF16), prm["norm_post"][layer, 0][None],
                   prm["ev_out_norm"][j][None])
    return x, (ka, va, s_f, s_b)


def _odd_layer(x, layer, j, prm, mod, rope_tabs, ctx):
    modflat, mod_rows, row_of = mod
    w_in = jnp.pad(prm["od_w_in"][j], ((0, 0), (0, ODD_IN_PAD - prm["od_w_in"].shape[-1]))).astype(BF16)
    w_q = prm["od_w_q_up"][j].reshape(C_Q_LORA, C_HEADS, C_NOPE + C_ROPE)
    w_q = jnp.pad(w_q, ((0, 0), (0, 0), (0, C_QK_PAD - C_NOPE - C_ROPE)))
    w_q = w_q.reshape(C_Q_LORA, C_HEADS * C_QK_PAD).astype(BF16)
    w_kv = prm["od_w_kv_up"][j].reshape(C_KV_LORA, C_HEADS, 2, C_NOPE)
    w_kv = w_kv.transpose(0, 2, 1, 3).reshape(C_KV_LORA, 2 * C_HEADS * C_NOPE).astype(BF16)
    q, ckv, kpe, kn, vv = _odd_in(
        x, modflat, mod_rows, row_of, layer, prm["norm_pre"][layer, 0][None], w_in,
        prm["od_q_norm"][j][None], prm["od_kv_norm"][j][None], w_q, w_kv, rope_tabs)
    if ctx is None:
        o = _mla_attn(q, kn, vv, None)
    else:
        ckv_ctx, kpe_ctx = ctx
        kpe_c = jnp.pad(kpe_ctx, ((0, 0), (0, 0), (0, LANES - C_ROPE)))
        o = _mla_attn(q, kn, vv, _kv_up(ckv_ctx, kpe_c, w_kv))
    x = _mixer_out([o], x, modflat, mod_rows, row_of, layer, prm["od_w_out"][j].astype(BF16),
                   prm["norm_post"][layer, 0][None])
    return x, (ckv, kpe[..., :C_ROPE])


def _trunk(x, prm, mod, rope_tabs, ctxs):
    depth = prm["w_mod"].shape[0]
    modflat, mod_rows, row_of = mod
    new_ctx = []
    for layer in range(depth):
        j = layer // 2
        ctx = None if ctxs is None else ctxs[layer]
        if layer % 2 == 0:
            x, nc = _even_layer(x, layer, j, prm, mod, rope_tabs, ctx)
        else:
            x, nc = _odd_layer(x, layer, j, prm, mod, rope_tabs, ctx)
        new_ctx.append(nc)
        x = _ffn(x, modflat, mod_rows, row_of, layer, prm["norm_pre"][layer, 1][None],
                 prm["norm_post"][layer, 1][None], prm["ffn_w_up"][layer].astype(BF16),
                 prm["ffn_conv"][layer], prm["ffn_w_down"][layer].astype(BF16))
    return x, new_ctx


def kernel(x_prompt, x_sample, cache_attn_k, cache_attn_v, state_delta_fwd, state_delta_bwd,
           cache_mla_ckv, cache_mla_kpe, c, c_ctx, w_mod, b_mod, norm_pre, norm_post,
           ffn_w_up, ffn_conv, ffn_w_down, ev_w_in, ev_conv, ev_a_log, ev_dt_bias, ev_sink,
           ev_out_norm, ev_w_out, od_w_in, od_q_norm, od_kv_norm, od_w_q_up, od_w_kv_up, od_w_out):
    prm = {
        "w_mod": w_mod, "b_mod": b_mod, "norm_pre": norm_pre, "norm_post": norm_post,
        "ffn_w_up": ffn_w_up, "ffn_conv": ffn_conv, "ffn_w_down": ffn_w_down,
        "ev_w_in": ev_w_in, "ev_conv": ev_conv, "ev_a_log": ev_a_log, "ev_dt_bias": ev_dt_bias,
        "ev_sink": ev_sink, "ev_out_norm": ev_out_norm, "ev_w_out": ev_w_out,
        "od_w_in": od_w_in, "od_q_norm": od_q_norm, "od_kv_norm": od_kv_norm,
        "od_w_q_up": od_w_q_up, "od_w_kv_up": od_w_kv_up, "od_w_out": od_w_out,
    }
    depth = w_mod.shape[0]
    n_c = c.shape[0]
    mod_rows = -(-(n_c + 1) // SUBLANES) * SUBLANES
    cvec = jnp.concatenate([c, c_ctx[None], jnp.zeros((mod_rows - n_c - 1, c.shape[1]), F32)], axis=0)
    modflat = _mod_all(cvec, w_mod, b_mod).reshape(depth * mod_rows * N_MOD, 1, D_MODEL)

    y_prompt, pctx = _trunk(x_prompt, prm, (modflat, mod_rows, lambda b: n_c), None, None)

    ctxs = []
    for layer in range(depth):
        j = layer // 2
        if layer % 2 == 0:
            ctxs.append((cache_attn_k[:, j], cache_attn_v[:, j], state_delta_fwd[:, j], state_delta_bwd[:, j]))
        else:
            ctxs.append((cache_mla_ckv[:, j], cache_mla_kpe[:, j]))
    y_sample, _ = _trunk(x_sample, prm, (modflat, mod_rows, lambda b: b), _rope_tables(x_sample.shape[1]), ctxs)

    bsz, seq = x_prompt.shape[:2]
    even, odd = pctx[0::2], pctx[1::2]
    kv_shape = (bsz, len(even), seq, A_KV_HEADS, A_HEAD_DIM)
    new_attn_k = jnp.stack([e[0] for e in even], axis=1).reshape(kv_shape)
    new_attn_v = jnp.stack([e[1] for e in even], axis=1).reshape(kv_shape)
    new_delta_fwd = jnp.stack([e[2] for e in even], axis=1)
    new_delta_bwd = jnp.stack([e[3] for e in even], axis=1)
    new_mla_ckv = jnp.stack([o[0] for o in odd], axis=1)
    new_mla_kpe = jnp.stack([o[1] for o in odd], axis=1)
    return (y_prompt, y_sample, new_attn_k, new_attn_v, new_delta_fwd, new_delta_bwd, new_mla_ckv, new_mla_kpe)
```

```python
import functools
import math

import jax
import jax.numpy as jnp
from jax import lax
from jax.experimental import pallas as pl
from jax.experimental.pallas import tpu as pltpu

F32 = jnp.float32
BF16 = jnp.bfloat16

D_MODEL = 1024
GRID_W = 64
A_HEADS = 8
A_KV_HEADS = 2
A_HEAD_DIM = 64
A_REP = A_HEADS // A_KV_HEADS
WINDOW = 128
Q_BLOCK = 128
B_HEADS = 4
B_DK = 128
B_DV = 128
CHUNK = 64
C_HEADS = 8
C_Q_LORA = 384
C_KV_LORA = 256
C_NOPE = 128
C_ROPE = 64
C_V = 128
C_QK_PAD = 256
C_V_PAD = 256
D_FF = 2816
ROPE_BASE = 10000.0
EPS = 1e-6
N_MOD = 6

A_Q = A_HEADS * A_HEAD_DIM
A_KV = A_KV_HEADS * A_HEAD_DIM
B_QK = B_HEADS * B_DK
B_VW = B_HEADS * B_DV
B_CONV_CH = 2 * B_QK + B_VW
KA0 = A_Q
VA0 = KA0 + A_KV
QKVB0 = VA0 + A_KV
GATE0 = QKVB0 + B_CONV_CH
AB0 = GATE0 + B_VW
LANES = 128
SUBLANES = 8
EVEN_IN_PAD = AB0 + LANES
ODD_IN_PAD = C_Q_LORA + C_KV_LORA + LANES
FF_TILE = 256
MLA_KEY_CHUNK = 512
DELTA_PREP_CHUNKS = 2
VMEM_LIMIT_BYTES = 56 * 1024 * 1024


def _params(*sem):
    return pltpu.CompilerParams(dimension_semantics=sem, vmem_limit_bytes=VMEM_LIMIT_BYTES)


def _dot(a, b):
    return jnp.dot(a, b, preferred_element_type=F32)


def _dot_nt(a, b):
    return lax.dot_general(a, b, (((1,), (1,)), ((), ())), preferred_element_type=F32)


def _sigmoid(x):
    return 1.0 / (1.0 + jnp.exp(-x))


def _silu(x):
    return x * _sigmoid(x)


def _softplus(x):
    return jnp.maximum(x, 0.0) + jnp.log(1.0 + jnp.exp(-jnp.abs(x)))


def _rms(x, g):
    return x * lax.rsqrt(jnp.mean(x * x, axis=-1, keepdims=True) + EPS) * g


def _norm_mod(x, g, scale, shift):
    return _rms(x, g) * (1.0 + scale) + shift


def _rope(x, cos_t, sin_t):
    lane = lax.broadcasted_iota(jnp.int32, x.shape, 1)
    swapped = jnp.where((lane & 63) < 32, pltpu.roll(x, LANES - 32, 1), pltpu.roll(x, 32, 1))
    return x * cos_t + swapped * sin_t


def _dwconv3(u, up, dn, cw):
    tm = u.shape[0]
    row = lax.broadcasted_iota(jnp.int32, u.shape, 0)
    u_prev = jnp.where(row == 0, up, pltpu.roll(u, 1, 0))
    u_next = jnp.where(row == tm - 1, dn, pltpu.roll(u, tm - 1, 0))
    return cw[0:1] * u_prev + cw[1:2] * u + cw[2:3] * u_next


def _halo_specs(tm, seq):
    per = tm // SUBLANES
    last = seq // SUBLANES - 1
    prev = pl.BlockSpec((1, SUBLANES, D_MODEL), lambda b, i, *_: (b, jnp.maximum(i * per - 1, 0), 0))
    nxt = pl.BlockSpec((1, SUBLANES, D_MODEL), lambda b, i, *_: (b, jnp.minimum((i + 1) * per, last), 0))
    return prev, nxt


def _mod_spec(mod_rows, layer, row_of, k):
    return pl.BlockSpec((1, 1, D_MODEL),
                        lambda b, *_: ((layer * mod_rows + row_of(b)) * N_MOD + k, 0, 0))


def _const_spec(shape):
    nd = len(shape)
    return pl.BlockSpec(shape, lambda *_: (0,) * nd)


def _resident_spec(shape):
    nd = len(shape)
    return pl.BlockSpec(shape, lambda *_: (0,) * nd, pipeline_mode=pl.Buffered(1))


def _mod_body(c_ref, w_ref, b_ref, o_ref):
    s = _silu(c_ref[...]).astype(BF16)
    o_ref[0] = _dot(s, w_ref[0].astype(BF16)) + b_ref[0]


def _mod_all(cvec, w_mod, b_mod):
    depth, d, n = w_mod.shape
    rows = cvec.shape[0]
    tn = n // 4
    return pl.pallas_call(
        _mod_body, grid=(depth, n // tn),
        in_specs=[pl.BlockSpec((rows, d), lambda l, j: (0, 0)),
                  pl.BlockSpec((1, d, tn), lambda l, j: (l, 0, j)),
                  pl.BlockSpec((1, 1, tn), lambda l, j: (l, 0, j))],
        out_specs=pl.BlockSpec((1, rows, tn), lambda l, j: (l, 0, j)),
        out_shape=jax.ShapeDtypeStruct((depth, rows, n), F32),
        name="mod_vectors", compiler_params=_params("parallel", "parallel"),
    )(cvec, w_mod, b_mod.reshape(depth, 1, n))


def _even_in_body(rope, n_tiles, x_ref, xp_ref, xn_ref, sc_ref, sh_ref, g_ref, w_ref, cw_ref,
                  alog_ref, dt_ref, *rest):
    if rope:
        cos_ref, sin_ref = rest[:2]
        rest = rest[2:]
    qa_ref, ka_ref, va_ref, qkv_ref, gate_ref, gb_ref, gbt_ref = rest
    i = pl.program_id(1)
    tm = x_ref.shape[1]
    g, sc, sh = g_ref[...], sc_ref[0], sh_ref[0]
    h = _norm_mod(x_ref[0], g, sc, sh).astype(BF16)
    y = _dot(h, w_ref[...])
    xh = jnp.concatenate([xp_ref[0], xn_ref[0]], axis=0)
    hh = _norm_mod(xh, g, sc, sh).astype(BF16)
    yh = _dot(hh, w_ref[:, QKVB0:GATE0])
    up = jnp.where(i > 0, yh[SUBLANES - 1:SUBLANES], 0.0)
    dn = jnp.where(i < n_tiles - 1, yh[SUBLANES:SUBLANES + 1], 0.0)

    qa = y[:, :A_Q]
    ka = y[:, KA0:VA0]
    if rope:
        cos_t, sin_t = cos_ref[...], sin_ref[...]
        qa = jnp.concatenate([_rope(qa[:, j * LANES:(j + 1) * LANES], cos_t, sin_t)
                              for j in range(A_Q // LANES)], axis=1)
        ka = _rope(ka, cos_t, sin_t)
    qa_ref[0] = qa
    ka_ref[0] = ka
    va_ref[0] = y[:, VA0:QKVB0]
    gate_ref[0] = y[:, GATE0:AB0]

    s = _silu(_dwconv3(y[:, QKVB0:GATE0], up, dn, cw_ref[...]))
    for j in range(B_CONV_CH // LANES):
        seg = s[:, j * LANES:(j + 1) * LANES]
        if j < 2 * B_HEADS:
            seg = seg * lax.rsqrt(jnp.sum(seg * seg, axis=-1, keepdims=True) + EPS)
        qkv_ref[0, :, j * LANES:(j + 1) * LANES] = seg

    ab = y[:, AB0:EVEN_IN_PAD]
    lane = lax.broadcasted_iota(jnp.int32, ab.shape, 1)
    gdec = -jnp.exp(alog_ref[...]) * _softplus(ab + dt_ref[...])
    gb = jnp.where((lane & B_HEADS) == 0, gdec, _sigmoid(ab))
    gb_ref[0] = gb
    gbt = gb.T
    for c in range(tm // CHUNK):
        gbt_ref[0, c] = gbt[:4 * B_HEADS, c * CHUNK:(c + 1) * CHUNK]


def _even_in(x, modflat, mod_rows, row_of, layer, g_pre, w_in, conv_w, alog_vec, dt_vec, rope_tabs):
    bsz, seq, d = x.shape
    tm = 512 if seq % 512 == 0 else 256
    nt = seq // tm
    rope = rope_tabs is not None
    prev, nxt = _halo_specs(tm, seq)
    in_specs = [pl.BlockSpec((1, tm, d), lambda b, i: (b, i, 0)), prev, nxt,
                _mod_spec(mod_rows, layer, row_of, 1), _mod_spec(mod_rows, layer, row_of, 0),
                _const_spec((1, d)), _const_spec(w_in.shape), _const_spec(conv_w.shape),
                _const_spec((1, LANES)), _const_spec((1, LANES))]
    args = [x, x, x, modflat, modflat, g_pre, w_in, conv_w, alog_vec, dt_vec]
    if rope:
        in_specs += [pl.BlockSpec((tm, LANES), lambda b, i: (i, 0))] * 2
        args += list(rope_tabs)

    def out(width, dtype=F32):
        return (pl.BlockSpec((1, tm, width), lambda b, i: (b, i, 0)),
                jax.ShapeDtypeStruct((bsz, seq, width), dtype))

    outs = [out(A_Q), out(A_KV), out(A_KV), out(B_CONV_CH), out(B_VW), out(LANES),
            (pl.BlockSpec((1, tm // CHUNK, 4 * B_HEADS, CHUNK), lambda b, i: (b, i, 0, 0)),
             jax.ShapeDtypeStruct((bsz, seq // CHUNK, 4 * B_HEADS, CHUNK), F32))]
    return pl.pallas_call(
        functools.partial(_even_in_body, rope, nt), grid=(bsz, nt), in_specs=in_specs,
        out_specs=[o[0] for o in outs], out_shape=[o[1] for o in outs],
        name="even_in", compiler_params=_params("parallel", "parallel"),
    )(*args)


def _even_attn_body(windowed, q_ref, k_ref, v_ref, *rest):
    if windowed:
        kc_ref, vc_ref, sink_ref, o_ref = rest
    else:
        sink_ref, o_ref = rest
    i = pl.program_id(1)
    seq = k_ref.shape[1]
    q_all = q_ref[0] * (A_HEAD_DIM ** -0.5)
    rows = A_REP * Q_BLOCK
    if windowed:
        band = Q_BLOCK + 2 * WINDOW
        k0 = pl.multiple_of(jnp.clip(i * Q_BLOCK - WINDOW, 0, seq - band), Q_BLOCK)
        k_all = k_ref[0, pl.ds(k0, band), :].astype(BF16)
        v_all = v_ref[0, pl.ds(k0, band), :].astype(BF16)
        kc_all = kc_ref[0].astype(BF16)
        vc_all = vc_ref[0].astype(BF16)
        qpos = i * Q_BLOCK + (lax.broadcasted_iota(jnp.int32, (rows, band), 0) & (Q_BLOCK - 1))
        kpos = k0 + lax.broadcasted_iota(jnp.int32, (rows, band), 1)
        valid = jnp.abs(qpos - kpos) <= WINDOW
    else:
        k_all = k_ref[0].astype(BF16)
        v_all = v_ref[0].astype(BF16)
    rid = lax.broadcasted_iota(jnp.int32, (rows, 1), 0) // Q_BLOCK
    groups = range(A_KV_HEADS)
    cols = [slice(g * A_HEAD_DIM, (g + 1) * A_HEAD_DIM) for g in groups]
    q, sink = [], []
    for g in groups:
        heads = range(g * A_REP, (g + 1) * A_REP)
        q.append(jnp.concatenate([q_all[:, h * A_HEAD_DIM:(h + 1) * A_HEAD_DIM] for h in heads],
                                 axis=0).astype(BF16))
        sk = jnp.zeros((rows, 1), F32)
        for r, h in enumerate(heads):
            sk = jnp.where(rid == r, sink_ref[h], sk)
        sink.append(sk)
    s = [_dot_nt(q[g], k_all[:, cols[g]]) for g in groups]
    if windowed:
        s = [jnp.where(valid, s[g], -jnp.inf) for g in groups]
        s_c = [_dot_nt(q[g], kc_all[:, cols[g]]) for g in groups]
        m = [jnp.maximum(jnp.max(s_c[g], axis=-1, keepdims=True), sink[g]) for g in groups]
    else:
        m = sink
    m = [jnp.maximum(jnp.max(s[g], axis=-1, keepdims=True), m[g]) for g in groups]
    e = [jnp.exp(s[g] - m[g]) for g in groups]
    den = [jnp.sum(e[g], axis=-1, keepdims=True) + jnp.exp(sink[g] - m[g]) for g in groups]
    o = [_dot(e[g].astype(BF16), v_all[:, cols[g]]) for g in groups]
    if windowed:
        e_c = [jnp.exp(s_c[g] - m[g]) for g in groups]
        den = [den[g] + jnp.sum(e_c[g], axis=-1, keepdims=True) for g in groups]
        o = [o[g] + _dot(e_c[g].astype(BF16), vc_all[:, cols[g]]) for g in groups]
    outs = []
    for g in groups:
        og = o[g] / den[g]
        outs += [og[r * Q_BLOCK:(r + 1) * Q_BLOCK] for r in range(A_REP)]
    o_ref[0] = jnp.concatenate(outs, axis=1).astype(o_ref.dtype)


def _even_attn(qa, ka, va, ctx, sink):
    bsz, seq, _ = qa.shape
    windowed = ctx is not None
    in_specs = [pl.BlockSpec((1, Q_BLOCK, A_Q), lambda b, i: (b, i, 0)),
                pl.BlockSpec((1, seq, A_KV), lambda b, i: (b, 0, 0)),
                pl.BlockSpec((1, seq, A_KV), lambda b, i: (b, 0, 0))]
    args = [qa, ka, va]
    if windowed:
        past = ctx[0].shape[1]
        in_specs += [pl.BlockSpec((1, past, A_KV), lambda b, i: (b, 0, 0))] * 2
        args += list(ctx)
    in_specs.append(pl.BlockSpec(memory_space=pltpu.SMEM))
    args.append(sink)
    return pl.pallas_call(
        functools.partial(_even_attn_body, windowed), grid=(bsz, seq // Q_BLOCK), in_specs=in_specs,
        out_specs=pl.BlockSpec((1, Q_BLOCK, A_Q), lambda b, i: (b, i, 0)),
        out_shape=jax.ShapeDtypeStruct((bsz, seq, A_Q), BF16),
        name="even_attn", compiler_params=_params("parallel", "parallel"),
    )(*args)


def _bf16_terms(x, n):
    terms = []
    for _ in range(n):
        t = x.astype(BF16).astype(F32)
        terms.append(t)
        x = x - t
    return terms


def _dot_split(a, b):
    ah, al = _bf16_terms(a, 2)
    bh, bl = _bf16_terms(b, 2)
    return _dot(jnp.concatenate([ah, al, ah, al], axis=1).astype(BF16),
                jnp.concatenate([bh, bh, bl, bl], axis=0).astype(BF16))


def _mask_dot(mask, x, mask_first):
    terms = _bf16_terms(x, 3)
    m16 = mask.astype(BF16)
    if mask_first:
        return _dot(jnp.concatenate([m16] * 3, axis=1), jnp.concatenate(terms, axis=0).astype(BF16))
    return _dot(jnp.concatenate(terms, axis=1).astype(BF16), jnp.concatenate([m16] * 3, axis=0))


def _delta_chunk_terms(chains, eye):
    n = range(len(chains))
    q, k, v, gcol, grow, bcol, incl, strict, g_last = zip(*chains)
    decay = [jnp.where(incl[i], jnp.exp(jnp.where(incl[i], gcol[i] - grow[i], 0.0)), 0.0) for i in n]
    qs = [q[i] * (B_DK ** -0.5) for i in n]
    kbeta = [k[i] * bcol[i] for i in n]
    both = [_dot_nt(jnp.concatenate([kbeta[i], qs[i]], axis=0).astype(BF16), k[i].astype(BF16)) for i in n]
    qk = [jnp.where(incl[i], both[i][CHUNK:] * decay[i], 0.0).astype(BF16) for i in n]
    p = [-jnp.where(strict[i], both[i][:CHUNK] * decay[i], 0.0) for i in n]
    t_inv = [eye + p[i] for i in n]
    for _ in range(int(math.log2(CHUNK)) - 1):
        p = [_dot_split(p[i], p[i]) for i in n]
        t_inv = [t_inv[i] + _dot_split(t_inv[i], p[i]) for i in n]
    eg = [jnp.exp(gcol[i]) for i in n]
    rhs = [jnp.concatenate([v[i] * bcol[i], kbeta[i] * eg[i]], axis=1).astype(BF16) for i in n]
    uw = [_dot(t_inv[i].astype(BF16), rhs[i]).astype(BF16) for i in n]
    kd = [(k[i] * jnp.exp(g_last[i] - gcol[i])).T.astype(BF16) for i in n]
    qk_uw = [_dot(qk[i], uw[i]) for i in n]
    cp = [_dot(kd[i], uw[i]) for i in n]
    out = []
    for i in n:
        q_eff = qs[i] * eg[i] - qk_uw[i][:, B_DV:]
        pq = jnp.concatenate([cp[i][:, B_DV:], q_eff], axis=0).astype(BF16)
        out.append((pq, cp[i][:, :B_DV], qk_uw[i][:, :B_DV], jnp.exp(g_last[i])))
    return out


def _delta_body(zero_init, qf_ref, qb_ref, gf_ref, gb_ref, gtf_ref, gtb_ref, *rest):
    if not zero_init:
        s0f_ref, s0b_ref = rest[:2]
        rest = rest[2:]
    of_ref, ob_ref, sfo_ref, sbo_ref, s_scr, pq_scr, c_scr, oi_scr, dec_scr = rest
    n = pl.program_id(1)
    n_chunks = gtf_ref.shape[1]
    n_chains = 2 * B_HEADS

    @pl.when(n == 0)
    def _():
        if zero_init:
            s_scr[...] = jnp.zeros_like(s_scr)
        else:
            s_scr[:B_HEADS] = s0f_ref[0]
            s_scr[B_HEADS:] = s0b_ref[0]

    ii = lax.broadcasted_iota(jnp.int32, (CHUNK, CHUNK), 0)
    jj = lax.broadcasted_iota(jnp.int32, (CHUNK, CHUNK), 1)
    lower, upper = ii >= jj, ii <= jj
    strict_lower, strict_upper = ii > jj, ii < jj
    tri_l, tri_u = lower.astype(F32), upper.astype(F32)
    eye = (ii == jj).astype(F32)

    def chunk_chains(c):
        rows = pl.ds(pl.multiple_of(c * CHUNK, CHUNK), CHUNK)
        gf, gb = gf_ref[0, rows, :], gb_ref[0, rows, :]
        gcf, gcb = _mask_dot(tri_l, gf, True), _mask_dot(tri_u, gb, True)
        gcf_t, gcb_t = _mask_dot(tri_u, gtf_ref[0, c], False), _mask_dot(tri_l, gtb_ref[0, c], False)
        chains = []
        for h in range(B_HEADS):
            cq = slice(h * B_DK, (h + 1) * B_DK)
            ck = slice(B_QK + h * B_DK, B_QK + (h + 1) * B_DK)
            cv = slice(2 * B_QK + h * B_DV, 2 * B_QK + (h + 1) * B_DV)
            lg, lb = h, B_HEADS + h
            gcol = gcf[:, lg:lg + 1]
            chains.append((qf_ref[0, rows, cq], qf_ref[0, rows, ck], qf_ref[0, rows, cv], gcol,
                           gcf_t[lg:lg + 1, :], gf[:, lb:lb + 1], lower, strict_lower, gcol[CHUNK - 1:CHUNK]))
        for h in range(B_HEADS):
            cq = slice(h * B_DK, (h + 1) * B_DK)
            ck = slice(B_QK + h * B_DK, B_QK + (h + 1) * B_DK)
            cv = slice(2 * B_QK + h * B_DV, 2 * B_QK + (h + 1) * B_DV)
            lg, lb = 2 * B_HEADS + h, 3 * B_HEADS + h
            gcol = gcb[:, lg:lg + 1]
            chains.append((qb_ref[0, rows, cq], qb_ref[0, rows, ck], qb_ref[0, rows, cv], gcol,
                           gcb_t[lg:lg + 1, :], gb[:, lb:lb + 1], upper, strict_upper, gcol[0:1]))
        return chains

    def prepare(pair, carry):
        cs = [DELTA_PREP_CHUNKS * pair + k for k in range(DELTA_PREP_CHUNKS)]
        chains = [ch for c in cs for ch in chunk_chains(c)]
        for j, (pq, cc, oi, dec) in enumerate(_delta_chunk_terms(chains, eye)):
            c, idx = cs[j // n_chains], j % n_chains
            pq_scr[c, idx] = pq
            c_scr[c, idx] = cc
            oi_scr[c, idx] = oi
            dec_scr[c, idx] = jnp.broadcast_to(dec, (SUBLANES, LANES))
        return carry

    lax.fori_loop(0, n_chunks // DELTA_PREP_CHUNKS, prepare, 0)

    def advance(t, carry):
        cf, cb = t, n_chunks - 1 - t
        outs, states = [], []
        for idx in range(n_chains):
            c = cf if idx < B_HEADS else cb
            s = s_scr[idx]
            r = _dot(pq_scr[c, idx], s.astype(BF16))
            outs.append(r[B_DK:] + oi_scr[c, idx])
            states.append(dec_scr[c, idx][0:1, :] * s + c_scr[c, idx] - r[:B_DK])
        of_ref[0, pl.ds(pl.multiple_of(cf * CHUNK, CHUNK), CHUNK), :] = jnp.concatenate(outs[:B_HEADS], axis=1)
        ob_ref[0, pl.ds(pl.multiple_of(cb * CHUNK, CHUNK), CHUNK), :] = jnp.concatenate(outs[B_HEADS:], axis=1)
        for idx in range(n_chains):
            s_scr[idx] = states[idx]
        return carry

    lax.fori_loop(0, n_chunks, advance, 0)

    @pl.when(n == pl.num_programs(1) - 1)
    def _():
        sfo_ref[0] = s_scr[:B_HEADS]
        sbo_ref[0] = s_scr[B_HEADS:]


def _delta(qkvb, gbv, gbt, states):
    bsz, seq, _ = qkvb.shape
    per = 4
    tb = per * CHUNK
    nb = seq // tb
    zero_init = states is None
    n_chains = 2 * B_HEADS

    def fwd(*tail):
        return lambda b, n: (b, n) + tail

    def bwd(*tail):
        return lambda b, n: (b, nb - 1 - n) + tail

    st_spec = pl.BlockSpec((1, B_HEADS, B_DK, B_DV), lambda b, n: (b, 0, 0, 0))
    in_specs = [pl.BlockSpec((1, tb, B_CONV_CH), fwd(0)), pl.BlockSpec((1, tb, B_CONV_CH), bwd(0)),
                pl.BlockSpec((1, tb, LANES), fwd(0)), pl.BlockSpec((1, tb, LANES), bwd(0)),
                pl.BlockSpec((1, per, 4 * B_HEADS, CHUNK), fwd(0, 0)),
                pl.BlockSpec((1, per, 4 * B_HEADS, CHUNK), bwd(0, 0))]
    args = [qkvb, qkvb, gbv, gbv, gbt, gbt]
    if not zero_init:
        in_specs += [st_spec, st_spec]
        args += list(states)
    st_shape = jax.ShapeDtypeStruct((bsz, B_HEADS, B_DK, B_DV), F32)
    o_shape = jax.ShapeDtypeStruct((bsz, seq, B_VW), F32)
    return pl.pallas_call(
        functools.partial(_delta_body, zero_init), grid=(bsz, nb), in_specs=in_specs,
        out_specs=[pl.BlockSpec((1, tb, B_VW), fwd(0)), pl.BlockSpec((1, tb, B_VW), bwd(0)),
                   st_spec, st_spec],
        out_shape=[o_shape, o_shape, st_shape, st_shape],
        scratch_shapes=[pltpu.VMEM((n_chains, B_DK, B_DV), F32),
                        pltpu.VMEM((per, n_chains, B_DK + CHUNK, B_DV), BF16),
                        pltpu.VMEM((per, n_chains, B_DK, B_DV), F32),
                        pltpu.VMEM((per, n_chains, CHUNK, B_DV), F32),
                        pltpu.VMEM((per, n_chains, SUBLANES, LANES), F32)],
        name="delta_rule", compiler_params=_params("parallel", "arbitrary"),
    )(*args)


def _sub_tiles(rows):
    step = min(rows, 512)
    return [slice(lo, lo + step) for lo in range(0, rows, step)]


def _even_out_body(oa_ref, of_ref, ob_ref, gate_ref, x_ref, g1_ref, on_ref, w_ref, gp_ref, o_ref):
    on = on_ref[...]
    for r in _sub_tiles(x_ref.shape[1]):
        ob = of_ref[0, r] + ob_ref[0, r]
        segs = [_rms(ob[:, h * B_DV:(h + 1) * B_DV], on) for h in range(B_HEADS)]
        obg = (jnp.concatenate(segs, axis=1) * _silu(gate_ref[0, r])).astype(BF16)
        mix = _dot(jnp.concatenate([oa_ref[0, r], obg], axis=1), w_ref[...])
        o_ref[0, r] = x_ref[0, r] + g1_ref[0] * _rms(mix, gp_ref[...])


def _odd_out_body(a_ref, x_ref, g1_ref, w_ref, gp_ref, o_ref):
    for r in _sub_tiles(x_ref.shape[1]):
        mix = _dot(a_ref[0, r], w_ref[...])
        o_ref[0, r] = x_ref[0, r] + g1_ref[0] * _rms(mix, gp_ref[...])


def _mixer_out(acts, x, modflat, mod_rows, row_of, layer, w_out, g_post, out_norm=None):
    bsz, seq, d = x.shape
    tm = next(t for t in (1024, 512, 256) if seq % t == 0)

    def row_spec(width):
        return pl.BlockSpec((1, tm, width), lambda b, i: (b, i, 0))

    in_specs = [row_spec(a.shape[-1]) for a in acts] + [row_spec(d), _mod_spec(mod_rows, layer, row_of, 2)]
    args = list(acts) + [x, modflat]
    if out_norm is not None:
        in_specs.append(_const_spec(out_norm.shape))
        args.append(out_norm)
    in_specs += [_const_spec(w_out.shape), _const_spec((1, d))]
    args += [w_out, g_post]
    return pl.pallas_call(
        _even_out_body if out_norm is not None else _odd_out_body,
        grid=(bsz, seq // tm), in_specs=in_specs, out_specs=row_spec(d),
        out_shape=jax.ShapeDtypeStruct(x.shape, F32),
        name="mixer_out", compiler_params=_params("parallel", "parallel"),
    )(*args)


def _write_keys_values(kv, kpe, k_ref, v_ref):
    kpe16 = kpe.astype(BF16)
    lane = lax.broadcasted_iota(jnp.int32, (kv.shape[0], C_V_PAD - C_V), 1)
    ones_col = jnp.where(lane == 0, 1.0, 0.0).astype(BF16)
    for hd in range(C_HEADS):
        base = hd * C_QK_PAD
        k_ref[0, :, base:base + C_NOPE] = kv[:, hd * C_NOPE:(hd + 1) * C_NOPE].astype(BF16)
        k_ref[0, :, base + C_NOPE:base + C_QK_PAD] = kpe16
        vbase = hd * C_V_PAD
        vcol = C_HEADS * C_NOPE + hd * C_V
        v_ref[0, :, vbase:vbase + C_V] = kv[:, vcol:vcol + C_V].astype(BF16)
        v_ref[0, :, vbase + C_V:vbase + C_V_PAD] = ones_col


def _odd_in_body(rope, x_ref, sc_ref, sh_ref, g_ref, w_ref, qn_ref, kvn_ref, wq_ref, wkv_ref, *rest):
    if rope:
        cos_ref, sin_ref = rest[:2]
        rest = rest[2:]
    q_ref, ckv_ref, kpe_ref, kn_ref, vv_ref = rest
    h = _norm_mod(x_ref[0], g_ref[...], sc_ref[0], sh_ref[0]).astype(BF16)
    y = _dot(h, w_ref[...])
    cq = _rms(y[:, :C_Q_LORA], qn_ref[...]).astype(BF16)
    q = _dot(cq, wq_ref[...])
    ckv = _rms(y[:, C_Q_LORA:C_Q_LORA + C_KV_LORA], kvn_ref[...])
    kpe = y[:, C_Q_LORA + C_KV_LORA:]
    scale = (C_NOPE + C_ROPE) ** -0.5
    if rope:
        cos_t, sin_t = cos_ref[...], sin_ref[...]
        kpe = _rope(kpe, cos_t, sin_t)
    for hd in range(C_HEADS):
        base = hd * C_QK_PAD
        q_ref[0, :, base:base + C_NOPE] = (q[:, base:base + C_NOPE] * scale).astype(BF16)
        qpe = q[:, base + C_NOPE:base + C_QK_PAD]
        if rope:
            qpe = _rope(qpe, cos_t, sin_t)
        q_ref[0, :, base + C_NOPE:base + C_QK_PAD] = (qpe * scale).astype(BF16)
    ckv_ref[0] = ckv
    kpe_ref[0] = kpe
    _write_keys_values(_dot(ckv.astype(BF16), wkv_ref[...]), kpe, kn_ref, vv_ref)


def _odd_in(x, modflat, mod_rows, row_of, layer, g_pre, w_in, q_norm, kv_norm, w_q, w_kv, rope_tabs):
    bsz, seq, d = x.shape
    tm = 512 if seq % 512 == 0 else 256
    rope = rope_tabs is not None
    in_specs = [pl.BlockSpec((1, tm, d), lambda b, i: (b, i, 0)),
                _mod_spec(mod_rows, layer, row_of, 1), _mod_spec(mod_rows, layer, row_of, 0),
                _const_spec((1, d)), _const_spec(w_in.shape), _const_spec(q_norm.shape),
                _const_spec(kv_norm.shape), _const_spec(w_q.shape), _const_spec(w_kv.shape)]
    args = [x, modflat, modflat, g_pre, w_in, q_norm, kv_norm, w_q, w_kv]
    if rope:
        in_specs += [pl.BlockSpec((tm, LANES), lambda b, i: (i, 0))] * 2
        args += list(rope_tabs)

    def out(width, dtype):
        return (pl.BlockSpec((1, tm, width), lambda b, i: (b, i, 0)),
                jax.ShapeDtypeStruct((bsz, seq, width), dtype))

    outs = [out(C_HEADS * C_QK_PAD, BF16), out(C_KV_LORA, F32), out(LANES, F32),
            out(C_HEADS * C_QK_PAD, BF16), out(C_HEADS * C_V_PAD, BF16)]
    return pl.pallas_call(
        functools.partial(_odd_in_body, rope), grid=(bsz, seq // tm), in_specs=in_specs,
        out_specs=[o[0] for o in outs], out_shape=[o[1] for o in outs],
        name="odd_in", compiler_params=_params("parallel", "parallel"),
    )(*args)


def _kv_up_body(ckv_ref, kpe_ref, wkv_ref, kn_ref, vv_ref):
    _write_keys_values(_dot(ckv_ref[0].astype(BF16), wkv_ref[...]), kpe_ref[0], kn_ref, vv_ref)


def _kv_up(ckv, kpe, w_kv):
    bsz, n, r = ckv.shape

    def rows(width):
        return pl.BlockSpec((1, n, width), lambda b: (b, 0, 0))

    return pl.pallas_call(
        _kv_up_body, grid=(bsz,),
        in_specs=[rows(r), rows(LANES), _const_spec(w_kv.shape)],
        out_specs=[rows(C_HEADS * C_QK_PAD), rows(C_HEADS * C_V_PAD)],
        out_shape=[jax.ShapeDtypeStruct((bsz, n, C_HEADS * C_QK_PAD), BF16),
                   jax.ShapeDtypeStruct((bsz, n, C_HEADS * C_V_PAD), BF16)],
        name="kv_up", compiler_params=_params("parallel"),
    )(ckv, kpe, w_kv)


def _mla_attn_body(has_ctx, q_ref, k_ref, v_ref, *rest):
    if has_ctx:
        kc_ref, vc_ref, o_ref = rest
        sources = [(kc_ref, vc_ref), (k_ref, v_ref)]
    else:
        (o_ref,) = rest
        sources = [(k_ref, v_ref)]
    q = q_ref[0]
    chunks = []
    for kr, vr in sources:
        n = kr.shape[1]
        step = min(MLA_KEY_CHUNK, n)
        chunks += [(kr, vr, lo, step) for lo in range(0, n, step)]

    def scores(j):
        kr, _, lo, step = chunks[j]
        return _dot_nt(q, kr[0, lo:lo + step, :])

    nxt = scores(0)
    for j, (_, vr, lo, step) in enumerate(chunks):
        s = nxt
        if j + 1 < len(chunks):
            nxt = scores(j + 1)
        top = jnp.max(s, axis=-1, keepdims=True)
        m_new = top if j == 0 else jnp.maximum(m, top)
        e = jnp.exp((s - m_new).astype(BF16))
        part = _dot(e, vr[0, lo:lo + step, :])
        o = part if j == 0 else jnp.exp(m - m_new) * o + part
        m = m_new
    o_ref[0] = (o[:, :C_V] / o[:, C_V:C_V + 1]).astype(o_ref.dtype)


def _mla_attn(q, kc, vv, ctx):
    bsz, seq, _ = q.shape
    tq = 512 if seq % 512 == 0 else 256
    has_ctx = ctx is not None

    def keys(n, width):
        return pl.BlockSpec((1, n, width), lambda b, h, i: (b, 0, h))

    in_specs = [pl.BlockSpec((1, tq, C_QK_PAD), lambda b, h, i: (b, i, h)),
                keys(seq, C_QK_PAD), keys(seq, C_V_PAD)]
    args = [q, kc, vv]
    if has_ctx:
        past = ctx[0].shape[1]
        in_specs += [keys(past, C_QK_PAD), keys(past, C_V_PAD)]
        args += list(ctx)
    return pl.pallas_call(
        functools.partial(_mla_attn_body, has_ctx), grid=(bsz, C_HEADS, seq // tq), in_specs=in_specs,
        out_specs=pl.BlockSpec((1, tq, C_V), lambda b, h, i: (b, i, h)),
        out_shape=jax.ShapeDtypeStruct((bsz, seq, C_HEADS * C_V), BF16),
        name="mla_attn", compiler_params=_params("parallel", "parallel", "parallel"),
    )(*args)


def _ffn_rows(x, xh, has_up, has_dn, g, sc, sh, g2, gp, wu_ref, cw_ref, wd_ref):
    rows = x.shape[0]
    h = jnp.concatenate([_norm_mod(x, g, sc, sh).astype(BF16), _norm_mod(xh, g, sc, sh).astype(BF16)], axis=0)

    def up_proj(f):
        lo = f * FF_TILE
        return _dot(h, wu_ref[:, lo:lo + FF_TILE]), _dot(h, wu_ref[:, D_FF + lo:D_FF + lo + FF_TILE])

    def conv(u, lo):
        up = jnp.where(has_up, u[rows + SUBLANES - 1:rows + SUBLANES], 0.0)
        dn = jnp.where(has_dn, u[rows + SUBLANES:rows + SUBLANES + 1], 0.0)
        return _dwconv3(u[:rows], up, dn, cw_ref[:, lo:lo + FF_TILE])

    nf = D_FF // FF_TILE
    acts = []
    nxt = up_proj(0)
    for f in range(nf):
        ua, ub = nxt
        if f + 1 < nf:
            nxt = up_proj(f + 1)
        lo = f * FF_TILE
        acts.append((_silu(conv(ua, lo)) * conv(ub, D_FF + lo)).astype(BF16))
    mix = _dot(jnp.concatenate(acts, axis=1), wd_ref[...])
    return x + g2 * _rms(mix, gp)


def _ffn_body(n_tiles, n_sub, x_ref, xp_ref, xn_ref, sc_ref, sh_ref, g2_ref, g_ref, gp_ref,
              wu_ref, cw_ref, wd_ref, o_ref):
    i = pl.program_id(1)
    rs = x_ref.shape[1] // n_sub
    for j in range(n_sub):
        lo, hi = j * rs, (j + 1) * rs
        before = xp_ref[0] if j == 0 else x_ref[0, lo - SUBLANES:lo]
        after = xn_ref[0] if j == n_sub - 1 else x_ref[0, hi:hi + SUBLANES]
        has_up = i > 0 if j == 0 else True
        has_dn = i < n_tiles - 1 if j == n_sub - 1 else True
        o_ref[0, lo:hi] = _ffn_rows(
            x_ref[0, lo:hi], jnp.concatenate([before, after], axis=0), has_up, has_dn,
            g_ref[...], sc_ref[0], sh_ref[0], g2_ref[0], gp_ref[...], wu_ref, cw_ref, wd_ref)


def _ffn(x, modflat, mod_rows, row_of, layer, g_pre, g_post, w_up, conv_w, w_down):
    bsz, seq, d = x.shape
    tm = next(t for t in (1024, 512, 256) if seq % t == 0)
    n_sub = max(tm // 512, 1)
    nt = seq // tm
    prev, nxt = _halo_specs(tm, seq)
    in_specs = [pl.BlockSpec((1, tm, d), lambda b, i: (b, i, 0)), prev, nxt,
                _mod_spec(mod_rows, layer, row_of, 4), _mod_spec(mod_rows, layer, row_of, 3),
                _mod_spec(mod_rows, layer, row_of, 5), _const_spec((1, d)), _const_spec((1, d)),
                _resident_spec(w_up.shape), _const_spec(conv_w.shape), _resident_spec(w_down.shape)]
    return pl.pallas_call(
        functools.partial(_ffn_body, nt, n_sub), grid=(bsz, nt), in_specs=in_specs,
        out_specs=pl.BlockSpec((1, tm, d), lambda b, i: (b, i, 0)),
        out_shape=jax.ShapeDtypeStruct(x.shape, F32),
        name="conv_ffn", compiler_params=_params("parallel", "parallel"),
    )(x, x, x, modflat, modflat, modflat, g_pre, g_post, w_up, conv_w, w_down)


def _rope_tables(seq):
    t = jnp.arange(seq)
    row = (t // GRID_W).astype(F32)
    col = (t % GRID_W).astype(F32)
    n_freq = A_HEAD_DIM // 4
    inv_freq = ROPE_BASE ** (-jnp.arange(n_freq, dtype=F32) / n_freq)
    ang = jnp.concatenate([row[:, None] * inv_freq, col[:, None] * inv_freq], axis=-1)
    cos, sin = jnp.cos(ang), jnp.sin(ang)
    return (jnp.concatenate([cos, cos, cos, cos], axis=-1), jnp.concatenate([-sin, sin, -sin, sin], axis=-1))


def _lane_vec(p):
    v = jnp.zeros((2, 2, B_HEADS), F32).at[:, 0, :].set(p.astype(F32)).reshape(1, 4 * B_HEADS)
    return jnp.pad(v, ((0, 0), (0, LANES - 4 * B_HEADS)))


def _even_layer(x, layer, j, prm, mod, rope_tabs, ctx):
    modflat, mod_rows, row_of = mod
    w_in = jnp.pad(prm["ev_w_in"][j], ((0, 0), (0, EVEN_IN_PAD - prm["ev_w_in"].shape[-1]))).astype(BF16)
    qa, ka, va, qkvb, gate, gbv, gbt = _even_in(
        x, modflat, mod_rows, row_of, layer, prm["norm_pre"][layer, 0][None], w_in, prm["ev_conv"][j],
        _lane_vec(prm["ev_a_log"][j]), _lane_vec(prm["ev_dt_bias"][j]), rope_tabs)
    if ctx is None:
        oa = _even_attn(qa, ka, va, None, prm["ev_sink"][j])
        o_f, o_b, s_f, s_b = _delta(qkvb, gbv, gbt, None)
    else:
        k_ctx, v_ctx, s0_f, s0_b = ctx
        flat = lambda t: t.reshape(t.shape[0], t.shape[1], A_KV)
        oa = _even_attn(qa, ka, va, (flat(k_ctx), flat(v_ctx)), prm["ev_sink"][j])
        o_f, o_b, s_f, s_b = _delta(qkvb, gbv, gbt, (s0_f, s0_b))
    x = _mixer_out([oa, o_f, o_b, gate], x, modflat, mod_rows, row_of, layer,
                   prm["ev_w_out"][j].astype(BF16), prm["norm_post"][layer, 0][None],
                   prm["ev_out_norm"][j][None])
    return x, (ka, va, s_f, s_b)


def _odd_layer(x, layer, j, prm, mod, rope_tabs, ctx):
    modflat, mod_rows, row_of = mod
    w_in = jnp.pad(prm["od_w_in"][j], ((0, 0), (0, ODD_IN_PAD - prm["od_w_in"].shape[-1]))).astype(BF16)
    w_q = prm["od_w_q_up"][j].reshape(C_Q_LORA, C_HEADS, C_NOPE + C_ROPE)
    w_q = jnp.pad(w_q, ((0, 0), (0, 0), (0, C_QK_PAD - C_NOPE - C_ROPE)))
    w_q = w_q.reshape(C_Q_LORA, C_HEADS * C_QK_PAD).astype(BF16)
    w_kv = prm["od_w_kv_up"][j].reshape(C_KV_LORA, C_HEADS, 2, C_NOPE)
    w_kv = w_kv.transpose(0, 2, 1, 3).reshape(C_KV_LORA, 2 * C_HEADS * C_NOPE).astype(BF16)
    q, ckv, kpe, kn, vv = _odd_in(
        x, modflat, mod_rows, row_of, layer, prm["norm_pre"][layer, 0][None], w_in,
        prm["od_q_norm"][j][None], prm["od_kv_norm"][j][None], w_q, w_kv, rope_tabs)
    if ctx is None:
        o = _mla_attn(q, kn, vv, None)
    else:
        ckv_ctx, kpe_ctx = ctx
        kpe_c = jnp.pad(kpe_ctx, ((0, 0), (0, 0), (0, LANES - C_ROPE)))
        o = _mla_attn(q, kn, vv, _kv_up(ckv_ctx, kpe_c, w_kv))
    x = _mixer_out([o], x, modflat, mod_rows, row_of, layer, prm["od_w_out"][j].astype(BF16),
                   prm["norm_post"][layer, 0][None])
    return x, (ckv, kpe[..., :C_ROPE])


def _trunk(x, prm, mod, rope_tabs, ctxs):
    depth = prm["w_mod"].shape[0]
    modflat, mod_rows, row_of = mod
    new_ctx = []
    for layer in range(depth):
        j = layer // 2
        ctx = None if ctxs is None else ctxs[layer]
        if layer % 2 == 0:
            x, nc = _even_layer(x, layer, j, prm, mod, rope_tabs, ctx)
        else:
            x, nc = _odd_layer(x, layer, j, prm, mod, rope_tabs, ctx)
        new_ctx.append(nc)
        x = _ffn(x, modflat, mod_rows, row_of, layer, prm["norm_pre"][layer, 1][None],
                 prm["norm_post"][layer, 1][None], prm["ffn_w_up"][layer].astype(BF16),
                 prm["ffn_conv"][layer], prm["ffn_w_down"][layer].astype(BF16))
    return x, new_ctx


def kernel(x_prompt, x_sample, cache_attn_k, cache_attn_v, state_delta_fwd, state_delta_bwd,
           cache_mla_ckv, cache_mla_kpe, c, c_ctx, w_mod, b_mod, norm_pre, norm_post,
           ffn_w_up, ffn_conv, ffn_w_down, ev_w_in, ev_conv, ev_a_log, ev_dt_bias, ev_sink,
           ev_out_norm, ev_w_out, od_w_in, od_q_norm, od_kv_norm, od_w_q_up, od_w_kv_up, od_w_out):
    prm = {
        "w_mod": w_mod, "b_mod": b_mod, "norm_pre": norm_pre, "norm_post": norm_post,
        "ffn_w_up": ffn_w_up, "ffn_conv": ffn_conv, "ffn_w_down": ffn_w_down,
        "ev_w_in": ev_w_in, "ev_conv": ev_conv, "ev_a_log": ev_a_log, "ev_dt_bias": ev_dt_bias,
        "ev_sink": ev_sink, "ev_out_norm": ev_out_norm, "ev_w_out": ev_w_out,
        "od_w_in": od_w_in, "od_q_norm": od_q_norm, "od_kv_norm": od_kv_norm,
        "od_w_q_up": od_w_q_up, "od_w_kv_up": od_w_kv_up, "od_w_out": od_w_out,
    }
    depth = w_mod.shape[0]
    n_c = c.shape[0]
    mod_rows = -(-(n_c + 1) // SUBLANES) * SUBLANES
    cvec = jnp.concatenate([c, c_ctx[None], jnp.zeros((mod_rows - n_c - 1, c.shape[1]), F32)], axis=0)
    modflat = _mod_all(cvec, w_mod, b_mod).reshape(depth * mod_rows * N_MOD, 1, D_MODEL)

    y_prompt, pctx = _trunk(x_prompt, prm, (modflat, mod_rows, lambda b: n_c), None, None)

    ctxs = []
    for layer in range(depth):
        j = layer // 2
        if layer % 2 == 0:
            ctxs.append((cache_attn_k[:, j], cache_attn_v[:, j], state_delta_fwd[:, j], state_delta_bwd[:, j]))
        else:
            ctxs.append((cache_mla_ckv[:, j], cache_mla_kpe[:, j]))
    y_sample, _ = _trunk(x_sample, prm, (modflat, mod_rows, lambda b: b), _rope_tables(x_sample.shape[1]), ctxs)

    bsz, seq = x_prompt.shape[:2]
    even, odd = pctx[0::2], pctx[1::2]
    kv_shape = (bsz, len(even), seq, A_KV_HEADS, A_HEAD_DIM)
    new_attn_k = jnp.stack([e[0] for e in even], axis=1).reshape(kv_shape)
    new_attn_v = jnp.stack([e[1] for e in even], axis=1).reshape(kv_shape)
    new_delta_fwd = jnp.stack([e[2] for e in even], axis=1)
    new_delta_bwd = jnp.stack([e[3] for e in even], axis=1)
    new_mla_ckv = jnp.stack([o[0] for o in odd], axis=1)
    new_mla_kpe = jnp.stack([o[1] for o in odd], axis=1)
    return (y_prompt, y_sample, new_attn_k, new_attn_v, new_delta_fwd, new_delta_bwd, new_mla_ckv, new_mla_kpe)
```

```python
import functools
import math

import jax
import jax.numpy as jnp
from jax import lax
from jax.experimental import pallas as pl
from jax.experimental.pallas import tpu as pltpu

F32 = jnp.float32
BF16 = jnp.bfloat16

D_MODEL = 1024
GRID_W = 64
A_HEADS = 8
A_KV_HEADS = 2
A_HEAD_DIM = 64
A_REP = A_HEADS // A_KV_HEADS
WINDOW = 128
Q_BLOCK = 128
B_HEADS = 4
B_DK = 128
B_DV = 128
CHUNK = 64
C_HEADS = 8
C_Q_LORA = 384
C_KV_LORA = 256
C_NOPE = 128
C_ROPE = 64
C_V = 128
C_QK_PAD = 256
C_V_PAD = 256
D_FF = 2816
ROPE_BASE = 10000.0
EPS = 1e-6
N_MOD = 6

A_Q = A_HEADS * A_HEAD_DIM
A_KV = A_KV_HEADS * A_HEAD_DIM
B_QK = B_HEADS * B_DK
B_VW = B_HEADS * B_DV
B_CONV_CH = 2 * B_QK + B_VW
KA0 = A_Q
VA0 = KA0 + A_KV
QKVB0 = VA0 + A_KV
GATE0 = QKVB0 + B_CONV_CH
AB0 = GATE0 + B_VW
LANES = 128
SUBLANES = 8
EVEN_IN_PAD = AB0 + LANES
ODD_IN_PAD = C_Q_LORA + C_KV_LORA + LANES
FF_TILE = 256
MLA_KEY_CHUNK = 512
DELTA_PREP_CHUNKS = 2
VMEM_LIMIT_BYTES = 56 * 1024 * 1024


def _params(*sem):
    return pltpu.CompilerParams(dimension_semantics=sem, vmem_limit_bytes=VMEM_LIMIT_BYTES)


def _dot(a, b):
    return jnp.dot(a, b, preferred_element_type=F32)


def _dot_nt(a, b):
    return lax.dot_general(a, b, (((1,), (1,)), ((), ())), preferred_element_type=F32)


def _sigmoid(x):
    return 1.0 / (1.0 + jnp.exp(-x))


def _silu(x):
    return x * _sigmoid(x)


def _softplus(x):
    return jnp.maximum(x, 0.0) + jnp.log(1.0 + jnp.exp(-jnp.abs(x)))


def _rms(x, g):
    return x * lax.rsqrt(jnp.mean(x * x, axis=-1, keepdims=True) + EPS) * g


def _norm_mod(x, g, scale, shift):
    return _rms(x, g) * (1.0 + scale) + shift


def _rope(x, cos_t, sin_t):
    lane = lax.broadcasted_iota(jnp.int32, x.shape, 1)
    swapped = jnp.where((lane & 63) < 32, pltpu.roll(x, LANES - 32, 1), pltpu.roll(x, 32, 1))
    return x * cos_t + swapped * sin_t


def _dwconv3(u, up, dn, cw):
    tm = u.shape[0]
    row = lax.broadcasted_iota(jnp.int32, u.shape, 0)
    u_prev = jnp.where(row == 0, up, pltpu.roll(u, 1, 0))
    u_next = jnp.where(row == tm - 1, dn, pltpu.roll(u, tm - 1, 0))
    return cw[0:1] * u_prev + cw[1:2] * u + cw[2:3] * u_next


def _halo_specs(tm, seq):
    per = tm // SUBLANES
    last = seq // SUBLANES - 1
    prev = pl.BlockSpec((1, SUBLANES, D_MODEL), lambda b, i, *_: (b, jnp.maximum(i * per - 1, 0), 0))
    nxt = pl.BlockSpec((1, SUBLANES, D_MODEL), lambda b, i, *_: (b, jnp.minimum((i + 1) * per, last), 0))
    return prev, nxt


def _mod_spec(mod_rows, layer, row_of, k):
    return pl.BlockSpec((1, 1, D_MODEL),
                        lambda b, *_: ((layer * mod_rows + row_of(b)) * N_MOD + k, 0, 0))


def _const_spec(shape):
    nd = len(shape)
    return pl.BlockSpec(shape, lambda *_: (0,) * nd)


def _resident_spec(shape):
    nd = len(shape)
    return pl.BlockSpec(shape, lambda *_: (0,) * nd, pipeline_mode=pl.Buffered(1))


def _mod_body(c_ref, w_ref, b_ref, o_ref):
    s = _silu(c_ref[...]).astype(BF16)
    o_ref[0] = _dot(s, w_ref[0].astype(BF16)) + b_ref[0]


def _mod_all(cvec, w_mod, b_mod):
    depth, d, n = w_mod.shape
    rows = cvec.shape[0]
    tn = n // 4
    return pl.pallas_call(
        _mod_body, grid=(depth, n // tn),
        in_specs=[pl.BlockSpec((rows, d), lambda l, j: (0, 0)),
                  pl.BlockSpec((1, d, tn), lambda l, j: (l, 0, j)),
                  pl.BlockSpec((1, 1, tn), lambda l, j: (l, 0, j))],
        out_specs=pl.BlockSpec((1, rows, tn), lambda l, j: (l, 0, j)),
        out_shape=jax.ShapeDtypeStruct((depth, rows, n), F32),
        name="mod_vectors", compiler_params=_params("parallel", "parallel"),
    )(cvec, w_mod, b_mod.reshape(depth, 1, n))


def _even_in_body(rope, n_tiles, x_ref, xp_ref, xn_ref, sc_ref, sh_ref, g_ref, w_ref, cw_ref,
                  alog_ref, dt_ref, *rest):
    if rope:
        cos_ref, sin_ref = rest[:2]
        rest = rest[2:]
    qa_ref, ka_ref, va_ref, qkv_ref, gate_ref, gb_ref, gbt_ref = rest
    i = pl.program_id(1)
    tm = x_ref.shape[1]
    g, sc, sh = g_ref[...], sc_ref[0], sh_ref[0]
    h = _norm_mod(x_ref[0], g, sc, sh).astype(BF16)
    xh = jnp.concatenate([xp_ref[0], xn_ref[0]], axis=0)
    hh = _norm_mod(xh, g, sc, sh).astype(BF16)
    yb = _dot(jnp.concatenate([h, hh], axis=0), w_ref[:, QKVB0:GATE0])
    ya = _dot(h, w_ref[:, :QKVB0])
    up = jnp.where(i > 0, yb[tm + SUBLANES - 1:tm + SUBLANES], 0.0)
    dn = jnp.where(i < n_tiles - 1, yb[tm + SUBLANES:tm + SUBLANES + 1], 0.0)
    s = _silu(_dwconv3(yb[:tm], up, dn, cw_ref[...]))
    for j in range(B_CONV_CH // LANES):
        seg = s[:, j * LANES:(j + 1) * LANES]
        if j < 2 * B_HEADS:
            seg = seg * lax.rsqrt(jnp.sum(seg * seg, axis=-1, keepdims=True) + EPS)
        qkv_ref[0, :, j * LANES:(j + 1) * LANES] = seg

    yc = _dot(h, w_ref[:, GATE0:])
    qa = ya[:, :A_Q]
    ka = ya[:, KA0:VA0]
    if rope:
        cos_t, sin_t = cos_ref[...], sin_ref[...]
        qa = jnp.concatenate([_rope(qa[:, j * LANES:(j + 1) * LANES], cos_t, sin_t)
                              for j in range(A_Q // LANES)], axis=1)
        ka = _rope(ka, cos_t, sin_t)
    qa_ref[0] = qa
    ka_ref[0] = ka
    va_ref[0] = ya[:, VA0:QKVB0]
    gate_ref[0] = yc[:, :B_VW]

    ab = yc[:, B_VW:]
    lane = lax.broadcasted_iota(jnp.int32, ab.shape, 1)
    gdec = -jnp.exp(alog_ref[...]) * _softplus(ab + dt_ref[...])
    gb = jnp.where((lane & B_HEADS) == 0, gdec, _sigmoid(ab))
    gb_ref[0] = gb
    gbt = gb.T
    for c in range(tm // CHUNK):
        gbt_ref[0, c] = gbt[:4 * B_HEADS, c * CHUNK:(c + 1) * CHUNK]


def _even_in(x, modflat, mod_rows, row_of, layer, g_pre, w_in, conv_w, alog_vec, dt_vec, rope_tabs):
    bsz, seq, d = x.shape
    tm = 512 if seq % 512 == 0 else 256
    nt = seq // tm
    rope = rope_tabs is not None
    prev, nxt = _halo_specs(tm, seq)
    in_specs = [pl.BlockSpec((1, tm, d), lambda b, i: (b, i, 0)), prev, nxt,
                _mod_spec(mod_rows, layer, row_of, 1), _mod_spec(mod_rows, layer, row_of, 0),
                _const_spec((1, d)), _const_spec(w_in.shape), _const_spec(conv_w.shape),
                _const_spec((1, LANES)), _const_spec((1, LANES))]
    args = [x, x, x, modflat, modflat, g_pre, w_in, conv_w, alog_vec, dt_vec]
    if rope:
        in_specs += [pl.BlockSpec((tm, LANES), lambda b, i: (i, 0))] * 2
        args += list(rope_tabs)

    def out(width, dtype=F32):
        return (pl.BlockSpec((1, tm, width), lambda b, i: (b, i, 0)),
                jax.ShapeDtypeStruct((bsz, seq, width), dtype))

    outs = [out(A_Q), out(A_KV), out(A_KV), out(B_CONV_CH), out(B_VW), out(LANES),
            (pl.BlockSpec((1, tm // CHUNK, 4 * B_HEADS, CHUNK), lambda b, i: (b, i, 0, 0)),
             jax.ShapeDtypeStruct((bsz, seq // CHUNK, 4 * B_HEADS, CHUNK), F32))]
    return pl.pallas_call(
        functools.partial(_even_in_body, rope, nt), grid=(bsz, nt), in_specs=in_specs,
        out_specs=[o[0] for o in outs], out_shape=[o[1] for o in outs],
        name="even_in", compiler_params=_params("parallel", "parallel"),
    )(*args)


def _even_attn_body(windowed, q_ref, k_ref, v_ref, *rest):
    if windowed:
        kc_ref, vc_ref, sink_ref, o_ref = rest
    else:
        sink_ref, o_ref = rest
    i = pl.program_id(1)
    seq = k_ref.shape[1]
    q_all = q_ref[0] * (A_HEAD_DIM ** -0.5)
    rows = A_REP * Q_BLOCK
    if windowed:
        band = Q_BLOCK + 2 * WINDOW
        k0 = pl.multiple_of(jnp.clip(i * Q_BLOCK - WINDOW, 0, seq - band), Q_BLOCK)
        k_all = k_ref[0, pl.ds(k0, band), :].astype(BF16)
        v_all = v_ref[0, pl.ds(k0, band), :].astype(BF16)
        kc_all = kc_ref[0].astype(BF16)
        vc_all = vc_ref[0].astype(BF16)
        qpos = i * Q_BLOCK + (lax.broadcasted_iota(jnp.int32, (rows, band), 0) & (Q_BLOCK - 1))
        kpos = k0 + lax.broadcasted_iota(jnp.int32, (rows, band), 1)
        valid = jnp.abs(qpos - kpos) <= WINDOW
    else:
        k_all = k_ref[0].astype(BF16)
        v_all = v_ref[0].astype(BF16)
    rid = lax.broadcasted_iota(jnp.int32, (rows, 1), 0) // Q_BLOCK
    groups = range(A_KV_HEADS)
    cols = [slice(g * A_HEAD_DIM, (g + 1) * A_HEAD_DIM) for g in groups]
    q, sink = [], []
    for g in groups:
        heads = range(g * A_REP, (g + 1) * A_REP)
        q.append(jnp.concatenate([q_all[:, h * A_HEAD_DIM:(h + 1) * A_HEAD_DIM] for h in heads],
                                 axis=0).astype(BF16))
        sk = jnp.zeros((rows, 1), F32)
        for r, h in enumerate(heads):
            sk = jnp.where(rid == r, sink_ref[h], sk)
        sink.append(sk)
    s = [_dot_nt(q[g], k_all[:, cols[g]]) for g in groups]
    if windowed:
        s = [jnp.where(valid, s[g], -jnp.inf) for g in groups]
        s_c = [_dot_nt(q[g], kc_all[:, cols[g]]) for g in groups]
        m = [jnp.maximum(jnp.max(s_c[g], axis=-1, keepdims=True), sink[g]) for g in groups]
    else:
        m = sink
    m = [jnp.maximum(jnp.max(s[g], axis=-1, keepdims=True), m[g]) for g in groups]

    def with_ones(v):
        lane = lax.broadcasted_iota(jnp.int32, (v.shape[0], LANES - A_HEAD_DIM), 1)
        return jnp.concatenate([v, jnp.where(lane == 0, 1.0, 0.0).astype(BF16)], axis=1)

    e = [jnp.exp((s[g] - m[g]).astype(BF16)) for g in groups]
    o = [_dot(e[g], with_ones(v_all[:, cols[g]])) for g in groups]
    if windowed:
        e_c = [jnp.exp((s_c[g] - m[g]).astype(BF16)) for g in groups]
        o = [o[g] + _dot(e_c[g], with_ones(vc_all[:, cols[g]])) for g in groups]
    outs = []
    for g in groups:
        den = o[g][:, A_HEAD_DIM:A_HEAD_DIM + 1] + jnp.exp(sink[g] - m[g])
        og = o[g][:, :A_HEAD_DIM] / den
        outs += [og[r * Q_BLOCK:(r + 1) * Q_BLOCK] for r in range(A_REP)]
    o_ref[0] = jnp.concatenate(outs, axis=1).astype(o_ref.dtype)


def _even_attn(qa, ka, va, ctx, sink):
    bsz, seq, _ = qa.shape
    windowed = ctx is not None
    in_specs = [pl.BlockSpec((1, Q_BLOCK, A_Q), lambda b, i: (b, i, 0)),
                pl.BlockSpec((1, seq, A_KV), lambda b, i: (b, 0, 0)),
                pl.BlockSpec((1, seq, A_KV), lambda b, i: (b, 0, 0))]
    args = [qa, ka, va]
    if windowed:
        past = ctx[0].shape[1]
        in_specs += [pl.BlockSpec((1, past, A_KV), lambda b, i: (b, 0, 0))] * 2
        args += list(ctx)
    in_specs.append(pl.BlockSpec(memory_space=pltpu.SMEM))
    args.append(sink)
    return pl.pallas_call(
        functools.partial(_even_attn_body, windowed), grid=(bsz, seq // Q_BLOCK), in_specs=in_specs,
        out_specs=pl.BlockSpec((1, Q_BLOCK, A_Q), lambda b, i: (b, i, 0)),
        out_shape=jax.ShapeDtypeStruct((bsz, seq, A_Q), BF16),
        name="even_attn", compiler_params=_params("parallel", "parallel"),
    )(*args)


def _bf16_terms(x, n):
    terms = []
    for _ in range(n):
        t = x.astype(BF16).astype(F32)
        terms.append(t)
        x = x - t
    return terms


def _dot_split(a, b):
    ah, al = _bf16_terms(a, 2)
    bh, bl = _bf16_terms(b, 2)
    return _dot(jnp.concatenate([ah, al, ah, al], axis=1).astype(BF16),
                jnp.concatenate([bh, bh, bl, bl], axis=0).astype(BF16))


def _mask_dot(mask, x, mask_first):
    terms = _bf16_terms(x, 3)
    m16 = mask.astype(BF16)
    if mask_first:
        return _dot(jnp.concatenate([m16] * 3, axis=1), jnp.concatenate(terms, axis=0).astype(BF16))
    return _dot(jnp.concatenate(terms, axis=1).astype(BF16), jnp.concatenate([m16] * 3, axis=0))


def _delta_chunk_terms(chains, eye, interleaved=()):
    interleaved = list(interleaved)
    n = range(len(chains))
    q, k, v, gcol, grow, bcol, incl, strict, g_last = zip(*chains)
    decay = [jnp.where(incl[i], jnp.exp(jnp.where(incl[i], gcol[i] - grow[i], 0.0)), 0.0) for i in n]
    qs = [q[i] * (B_DK ** -0.5) for i in n]
    kbeta = [k[i] * bcol[i] for i in n]
    both = [_dot_nt(jnp.concatenate([kbeta[i], qs[i]], axis=0).astype(BF16), k[i].astype(BF16)) for i in n]
    qk = [jnp.where(incl[i], both[i][CHUNK:] * decay[i], 0.0).astype(BF16) for i in n]
    p = [-jnp.where(strict[i], both[i][:CHUNK] * decay[i], 0.0) for i in n]
    t_inv = [eye + p[i] for i in n]
    for _ in range(int(math.log2(CHUNK)) - 1):
        if interleaved:
            interleaved.pop(0)()
        p = [_dot_split(p[i], p[i]) for i in n]
        t_inv = [t_inv[i] + _dot_split(t_inv[i], p[i]) for i in n]
    while interleaved:
        interleaved.pop(0)()
    eg = [jnp.exp(gcol[i]) for i in n]
    rhs = [jnp.concatenate([v[i] * bcol[i], kbeta[i] * eg[i]], axis=1).astype(BF16) for i in n]
    uw = [_dot(t_inv[i].astype(BF16), rhs[i]).astype(BF16) for i in n]
    kd = [(k[i] * jnp.exp(g_last[i] - gcol[i])).T.astype(BF16) for i in n]
    qk_uw = [_dot(qk[i], uw[i]) for i in n]
    cp = [_dot(kd[i], uw[i]) for i in n]
    out = []
    for i in n:
        q_eff = qs[i] * eg[i] - qk_uw[i][:, B_DV:]
        pq = jnp.concatenate([cp[i][:, B_DV:], q_eff], axis=0).astype(BF16)
        out.append((pq, cp[i][:, :B_DV], qk_uw[i][:, :B_DV], jnp.exp(g_last[i])))
    return out


def _delta_body(zero_init, qf_ref, qb_ref, gf_ref, gb_ref, gtf_ref, gtb_ref, *rest):
    if not zero_init:
        s0f_ref, s0b_ref = rest[:2]
        rest = rest[2:]
    of_ref, ob_ref, sfo_ref, sbo_ref, s_scr = rest
    n = pl.program_id(1)
    n_chunks = gtf_ref.shape[1]
    n_chains = 2 * B_HEADS

    @pl.when(n == 0)
    def _():
        if zero_init:
            s_scr[...] = jnp.zeros_like(s_scr)
        else:
            s_scr[:B_HEADS] = s0f_ref[0]
            s_scr[B_HEADS:] = s0b_ref[0]

    ii = lax.broadcasted_iota(jnp.int32, (CHUNK, CHUNK), 0)
    jj = lax.broadcasted_iota(jnp.int32, (CHUNK, CHUNK), 1)
    lower, upper = ii >= jj, ii <= jj
    strict_lower, strict_upper = ii > jj, ii < jj
    tri_l, tri_u = lower.astype(F32), upper.astype(F32)
    eye = (ii == jj).astype(F32)

    def chains_of(q_ref, g_ref, gt_ref, c, lane0, tri_col, tri_row, incl, strict, last):
        rows = slice(c * CHUNK, (c + 1) * CHUNK)
        g = g_ref[0, rows, :]
        gc, gc_t = _mask_dot(tri_col, g, True), _mask_dot(tri_row, gt_ref[0, c], False)
        chains = []
        for h in range(B_HEADS):
            cq = slice(h * B_DK, (h + 1) * B_DK)
            ck = slice(B_QK + h * B_DK, B_QK + (h + 1) * B_DK)
            cv = slice(2 * B_QK + h * B_DV, 2 * B_QK + (h + 1) * B_DV)
            lg, lb = lane0 + h, lane0 + B_HEADS + h
            gcol = gc[:, lg:lg + 1]
            chains.append((q_ref[0, rows, cq], q_ref[0, rows, ck], q_ref[0, rows, cv], gcol,
                           gc_t[lg:lg + 1, :], g[:, lb:lb + 1], incl, strict, gcol[last:last + 1]))
        return chains

    states = [s_scr[idx] for idx in range(n_chains)]

    def scan_step(terms, cf, cb):
        def step():
            outs = []
            for idx, (pq, cc, oi, dec) in enumerate(terms):
                s = states[idx]
                r = _dot(pq, s.astype(BF16))
                outs.append(r[B_DK:] + oi)
                states[idx] = dec * s + cc - r[:B_DK]
            of_ref[0, cf * CHUNK:(cf + 1) * CHUNK, :] = jnp.concatenate(outs[:B_HEADS], axis=1)
            ob_ref[0, cb * CHUNK:(cb + 1) * CHUNK, :] = jnp.concatenate(outs[B_HEADS:], axis=1)
        return step

    pending = []
    for p in range(n_chunks // DELTA_PREP_CHUNKS):
        order = [(DELTA_PREP_CHUNKS * p + k, n_chunks - 1 - DELTA_PREP_CHUNKS * p - k)
                 for k in range(DELTA_PREP_CHUNKS)]
        chains = []
        for cf, cb in order:
            chains += chains_of(qf_ref, gf_ref, gtf_ref, cf, 0, tri_l, tri_u, lower, strict_lower, CHUNK - 1)
            chains += chains_of(qb_ref, gb_ref, gtb_ref, cb, 2 * B_HEADS, tri_u, tri_l, upper, strict_upper, 0)
        terms = _delta_chunk_terms(chains, eye, pending)
        pending = [scan_step(terms[k * n_chains:(k + 1) * n_chains], cf, cb) for k, (cf, cb) in enumerate(order)]
    for step in pending:
        step()
    for idx in range(n_chains):
        s_scr[idx] = states[idx]

    @pl.when(n == pl.num_programs(1) - 1)
    def _():
        sfo_ref[0] = s_scr[:B_HEADS]
        sbo_ref[0] = s_scr[B_HEADS:]


def _delta(qkvb, gbv, gbt, states):
    bsz, seq, _ = qkvb.shape
    per = 4
    tb = per * CHUNK
    nb = seq // tb
    zero_init = states is None
    n_chains = 2 * B_HEADS

    def fwd(*tail):
        return lambda b, n: (b, n) + tail

    def bwd(*tail):
        return lambda b, n: (b, nb - 1 - n) + tail

    st_spec = pl.BlockSpec((1, B_HEADS, B_DK, B_DV), lambda b, n: (b, 0, 0, 0))
    in_specs = [pl.BlockSpec((1, tb, B_CONV_CH), fwd(0)), pl.BlockSpec((1, tb, B_CONV_CH), bwd(0)),
                pl.BlockSpec((1, tb, LANES), fwd(0)), pl.BlockSpec((1, tb, LANES), bwd(0)),
                pl.BlockSpec((1, per, 4 * B_HEADS, CHUNK), fwd(0, 0)),
                pl.BlockSpec((1, per, 4 * B_HEADS, CHUNK), bwd(0, 0))]
    args = [qkvb, qkvb, gbv, gbv, gbt, gbt]
    if not zero_init:
        in_specs += [st_spec, st_spec]
        args += list(states)
    st_shape = jax.ShapeDtypeStruct((bsz, B_HEADS, B_DK, B_DV), F32)
    o_shape = jax.ShapeDtypeStruct((bsz, seq, B_VW), F32)
    return pl.pallas_call(
        functools.partial(_delta_body, zero_init), grid=(bsz, nb), in_specs=in_specs,
        out_specs=[pl.BlockSpec((1, tb, B_VW), fwd(0)), pl.BlockSpec((1, tb, B_VW), bwd(0)),
                   st_spec, st_spec],
        out_shape=[o_shape, o_shape, st_shape, st_shape],
        scratch_shapes=[pltpu.VMEM((n_chains, B_DK, B_DV), F32)],
        name="delta_rule", compiler_params=_params("parallel", "arbitrary"),
    )(*args)


def _sub_tiles(rows):
    step = min(rows, 512)
    return [slice(lo, lo + step) for lo in range(0, rows, step)]


def _even_out_body(oa_ref, of_ref, ob_ref, gate_ref, x_ref, g1_ref, on_ref, w_ref, gp_ref, o_ref):
    on = on_ref[...]
    for r in _sub_tiles(x_ref.shape[1]):
        ob = of_ref[0, r] + ob_ref[0, r]
        segs = [_rms(ob[:, h * B_DV:(h + 1) * B_DV], on) for h in range(B_HEADS)]
        obg = (jnp.concatenate(segs, axis=1) * _silu(gate_ref[0, r])).astype(BF16)
        mix = _dot(jnp.concatenate([oa_ref[0, r], obg], axis=1), w_ref[...])
        o_ref[0, r] = x_ref[0, r] + g1_ref[0] * _rms(mix, gp_ref[...])


def _odd_out_body(a_ref, x_ref, g1_ref, w_ref, gp_ref, o_ref):
    for r in _sub_tiles(x_ref.shape[1]):
        mix = _dot(a_ref[0, r], w_ref[...])
        o_ref[0, r] = x_ref[0, r] + g1_ref[0] * _rms(mix, gp_ref[...])


def _mixer_out(acts, x, modflat, mod_rows, row_of, layer, w_out, g_post, out_norm=None):
    bsz, seq, d = x.shape
    tm = next(t for t in (1024, 512, 256) if seq % t == 0)

    def row_spec(width):
        return pl.BlockSpec((1, tm, width), lambda b, i: (b, i, 0))

    in_specs = [row_spec(a.shape[-1]) for a in acts] + [row_spec(d), _mod_spec(mod_rows, layer, row_of, 2)]
    args = list(acts) + [x, modflat]
    if out_norm is not None:
        in_specs.append(_const_spec(out_norm.shape))
        args.append(out_norm)
    in_specs += [_const_spec(w_out.shape), _const_spec((1, d))]
    args += [w_out, g_post]
    return pl.pallas_call(
        _even_out_body if out_norm is not None else _odd_out_body,
        grid=(bsz, seq // tm), in_specs=in_specs, out_specs=row_spec(d),
        out_shape=jax.ShapeDtypeStruct(x.shape, F32),
        name="mixer_out", compiler_params=_params("parallel", "parallel"),
    )(*args)


def _write_keys_values(kv, kpe, k_ref, v_ref):
    kpe16 = kpe.astype(BF16)
    lane = lax.broadcasted_iota(jnp.int32, (kv.shape[0], C_V_PAD - C_V), 1)
    ones_col = jnp.where(lane == 0, 1.0, 0.0).astype(BF16)
    for hd in range(C_HEADS):
        base = hd * C_QK_PAD
        k_ref[0, :, base:base + C_NOPE] = kv[:, hd * C_NOPE:(hd + 1) * C_NOPE].astype(BF16)
        k_ref[0, :, base + C_NOPE:base + C_QK_PAD] = kpe16
        vbase = hd * C_V_PAD
        vcol = C_HEADS * C_NOPE + hd * C_V
        v_ref[0, :, vbase:vbase + C_V] = kv[:, vcol:vcol + C_V].astype(BF16)
        v_ref[0, :, vbase + C_V:vbase + C_V_PAD] = ones_col


def _odd_in_body(rope, x_ref, sc_ref, sh_ref, g_ref, w_ref, qn_ref, kvn_ref, wq_ref, wkv_ref, *rest):
    if rope:
        cos_ref, sin_ref = rest[:2]
        rest = rest[2:]
    q_ref, ckv_ref, kpe_ref, kn_ref, vv_ref = rest
    h = _norm_mod(x_ref[0], g_ref[...], sc_ref[0], sh_ref[0]).astype(BF16)
    y = _dot(h, w_ref[...])
    cq = _rms(y[:, :C_Q_LORA], qn_ref[...]).astype(BF16)
    ckv = _rms(y[:, C_Q_LORA:C_Q_LORA + C_KV_LORA], kvn_ref[...])
    q = _dot(cq, wq_ref[...])
    kv = _dot(ckv.astype(BF16), wkv_ref[...])
    kpe = y[:, C_Q_LORA + C_KV_LORA:]
    scale = (C_NOPE + C_ROPE) ** -0.5
    if rope:
        cos_t, sin_t = cos_ref[...], sin_ref[...]
        kpe = _rope(kpe, cos_t, sin_t)
    for hd in range(C_HEADS):
        base = hd * C_QK_PAD
        q_ref[0, :, base:base + C_NOPE] = (q[:, base:base + C_NOPE] * scale).astype(BF16)
        qpe = q[:, base + C_NOPE:base + C_QK_PAD]
        if rope:
            qpe = _rope(qpe, cos_t, sin_t)
        q_ref[0, :, base + C_NOPE:base + C_QK_PAD] = (qpe * scale).astype(BF16)
    ckv_ref[0] = ckv
    kpe_ref[0] = kpe
    _write_keys_values(kv, kpe, kn_ref, vv_ref)


def _odd_in(x, modflat, mod_rows, row_of, layer, g_pre, w_in, q_norm, kv_norm, w_q, w_kv, rope_tabs):
    bsz, seq, d = x.shape
    tm = 512 if seq % 512 == 0 else 256
    rope = rope_tabs is not None
    in_specs = [pl.BlockSpec((1, tm, d), lambda b, i: (b, i, 0)),
                _mod_spec(mod_rows, layer, row_of, 1), _mod_spec(mod_rows, layer, row_of, 0),
                _const_spec((1, d)), _const_spec(w_in.shape), _const_spec(q_norm.shape),
                _const_spec(kv_norm.shape), _const_spec(w_q.shape), _const_spec(w_kv.shape)]
    args = [x, modflat, modflat, g_pre, w_in, q_norm, kv_norm, w_q, w_kv]
    if rope:
        in_specs += [pl.BlockSpec((tm, LANES), lambda b, i: (i, 0))] * 2
        args += list(rope_tabs)

    def out(width, dtype):
        return (pl.BlockSpec((1, tm, width), lambda b, i: (b, i, 0)),
                jax.ShapeDtypeStruct((bsz, seq, width), dtype))

    outs = [out(C_HEADS * C_QK_PAD, BF16), out(C_KV_LORA, F32), out(LANES, F32),
            out(C_HEADS * C_QK_PAD, BF16), out(C_HEADS * C_V_PAD, BF16)]
    return pl.pallas_call(
        functools.partial(_odd_in_body, rope), grid=(bsz, seq // tm), in_specs=in_specs,
        out_specs=[o[0] for o in outs], out_shape=[o[1] for o in outs],
        name="odd_in", compiler_params=_params("parallel", "parallel"),
    )(*args)


def _kv_up_body(ckv_ref, kpe_ref, wkv_ref, kn_ref, vv_ref):
    _write_keys_values(_dot(ckv_ref[0].astype(BF16), wkv_ref[...]), kpe_ref[0], kn_ref, vv_ref)


def _kv_up(ckv, kpe, w_kv):
    bsz, n, r = ckv.shape

    def rows(width):
        return pl.BlockSpec((1, n, width), lambda b: (b, 0, 0))

    return pl.pallas_call(
        _kv_up_body, grid=(bsz,),
        in_specs=[rows(r), rows(LANES), _const_spec(w_kv.shape)],
        out_specs=[rows(C_HEADS * C_QK_PAD), rows(C_HEADS * C_V_PAD)],
        out_shape=[jax.ShapeDtypeStruct((bsz, n, C_HEADS * C_QK_PAD), BF16),
                   jax.ShapeDtypeStruct((bsz, n, C_HEADS * C_V_PAD), BF16)],
        name="kv_up", compiler_params=_params("parallel"),
    )(ckv, kpe, w_kv)


def _mla_attn_body(has_ctx, q_ref, k_ref, v_ref, *rest):
    if has_ctx:
        kc_ref, vc_ref, o_ref = rest
        sources = [(kc_ref, vc_ref), (k_ref, v_ref)]
    else:
        (o_ref,) = rest
        sources = [(k_ref, v_ref)]
    q = q_ref[0]
    chunks = []
    for kr, vr in sources:
        n = kr.shape[1]
        step = min(MLA_KEY_CHUNK, n)
        chunks += [(kr, vr, lo, step) for lo in range(0, n, step)]

    def scores(j):
        kr, _, lo, step = chunks[j]
        return _dot_nt(q, kr[0, lo:lo + step, :])

    nxt = scores(0)
    for j, (_, vr, lo, step) in enumerate(chunks):
        s = nxt
        if j + 1 < len(chunks):
            nxt = scores(j + 1)
        top = jnp.max(s, axis=-1, keepdims=True)
        m_new = top if j == 0 else jnp.maximum(m, top)
        e = jnp.exp((s - m_new).astype(BF16))
        part = _dot(e, vr[0, lo:lo + step, :])
        o = part if j == 0 else jnp.exp(m - m_new) * o + part
        m = m_new
    o_ref[0] = (o[:, :C_V] / o[:, C_V:C_V + 1]).astype(o_ref.dtype)


def _mla_attn(q, kc, vv, ctx):
    bsz, seq, _ = q.shape
    tq = 512 if seq % 512 == 0 else 256
    has_ctx = ctx is not None

    def keys(n, width):
        return pl.BlockSpec((1, n, width), lambda b, h, i: (b, 0, h))

    in_specs = [pl.BlockSpec((1, tq, C_QK_PAD), lambda b, h, i: (b, i, h)),
                keys(seq, C_QK_PAD), keys(seq, C_V_PAD)]
    args = [q, kc, vv]
    if has_ctx:
        past = ctx[0].shape[1]
        in_specs += [keys(past, C_QK_PAD), keys(past, C_V_PAD)]
        args += list(ctx)
    return pl.pallas_call(
        functools.partial(_mla_attn_body, has_ctx), grid=(bsz, C_HEADS, seq // tq), in_specs=in_specs,
        out_specs=pl.BlockSpec((1, tq, C_V), lambda b, h, i: (b, i, h)),
        out_shape=jax.ShapeDtypeStruct((bsz, seq, C_HEADS * C_V), BF16),
        name="mla_attn", compiler_params=_params("parallel", "parallel", "parallel"),
    )(*args)


def _ffn_rows(x, xh, has_up, has_dn, g, sc, sh, g2, gp, wu_ref, cw_ref, wd_ref):
    rows = x.shape[0]
    h = jnp.concatenate([_norm_mod(x, g, sc, sh).astype(BF16), _norm_mod(xh, g, sc, sh).astype(BF16)], axis=0)

    def up_proj(f):
        lo = f * FF_TILE
        return _dot(h, wu_ref[:, lo:lo + FF_TILE]), _dot(h, wu_ref[:, D_FF + lo:D_FF + lo + FF_TILE])

    def conv(u, lo):
        up = jnp.where(has_up, u[rows + SUBLANES - 1:rows + SUBLANES], 0.0)
        dn = jnp.where(has_dn, u[rows + SUBLANES:rows + SUBLANES + 1], 0.0)
        return _dwconv3(u[:rows], up, dn, cw_ref[:, lo:lo + FF_TILE])

    nf = D_FF // FF_TILE
    acts = []
    nxt = up_proj(0)
    for f in range(nf):
        ua, ub = nxt
        if f + 1 < nf:
            nxt = up_proj(f + 1)
        lo = f * FF_TILE
        acts.append((_silu(conv(ua, lo)) * conv(ub, D_FF + lo)).astype(BF16))
    mix = _dot(jnp.concatenate(acts, axis=1), wd_ref[...])
    return x + g2 * _rms(mix, gp)


def _ffn_body(n_tiles, n_sub, x_ref, xp_ref, xn_ref, sc_ref, sh_ref, g2_ref, g_ref, gp_ref,
              wu_ref, cw_ref, wd_ref, o_ref):
    i = pl.program_id(1)
    rs = x_ref.shape[1] // n_sub
    for j in range(n_sub):
        lo, hi = j * rs, (j + 1) * rs
        before = xp_ref[0] if j == 0 else x_ref[0, lo - SUBLANES:lo]
        after = xn_ref[0] if j == n_sub - 1 else x_ref[0, hi:hi + SUBLANES]
        has_up = i > 0 if j == 0 else True
        has_dn = i < n_tiles - 1 if j == n_sub - 1 else True
        o_ref[0, lo:hi] = _ffn_rows(
            x_ref[0, lo:hi], jnp.concatenate([before, after], axis=0), has_up, has_dn,
            g_ref[...], sc_ref[0], sh_ref[0], g2_ref[0], gp_ref[...], wu_ref, cw_ref, wd_ref)


def _ffn(x, modflat, mod_rows, row_of, layer, g_pre, g_post, w_up, conv_w, w_down):
    bsz, seq, d = x.shape
    tm = next(t for t in (1024, 512, 256) if seq % t == 0)
    n_sub = max(tm // 512, 1)
    nt = seq // tm
    prev, nxt = _halo_specs(tm, seq)
    in_specs = [pl.BlockSpec((1, tm, d), lambda b, i: (b, i, 0)), prev, nxt,
                _mod_spec(mod_rows, layer, row_of, 4), _mod_spec(mod_rows, layer, row_of, 3),
                _mod_spec(mod_rows, layer, row_of, 5), _const_spec((1, d)), _const_spec((1, d)),
                _resident_spec(w_up.shape), _const_spec(conv_w.shape), _resident_spec(w_down.shape)]
    return pl.pallas_call(
        functools.partial(_ffn_body, nt, n_sub), grid=(bsz, nt), in_specs=in_specs,
        out_specs=pl.BlockSpec((1, tm, d), lambda b, i: (b, i, 0)),
        out_shape=jax.ShapeDtypeStruct(x.shape, F32),
        name="conv_ffn", compiler_params=_params("parallel", "parallel"),
    )(x, x, x, modflat, modflat, modflat, g_pre, g_post, w_up, conv_w, w_down)


def _rope_tables(seq):
    t = jnp.arange(seq)
    row = (t // GRID_W).astype(F32)
    col = (t % GRID_W).astype(F32)
    n_freq = A_HEAD_DIM // 4
    inv_freq = ROPE_BASE ** (-jnp.arange(n_freq, dtype=F32) / n_freq)
    ang = jnp.concatenate([row[:, None] * inv_freq, col[:, None] * inv_freq], axis=-1)
    cos, sin = jnp.cos(ang), jnp.sin(ang)
    return (jnp.concatenate([cos, cos, cos, cos], axis=-1), jnp.concatenate([-sin, sin, -sin, sin], axis=-1))


def _lane_vec(p):
    v = jnp.zeros((2, 2, B_HEADS), F32).at[:, 0, :].set(p.astype(F32)).reshape(1, 4 * B_HEADS)
    return jnp.pad(v, ((0, 0), (0, LANES - 4 * B_HEADS)))


def _even_layer(x, layer, j, prm, mod, rope_tabs, ctx):
    modflat, mod_rows, row_of = mod
    w_in = jnp.pad(prm["ev_w_in"][j], ((0, 0), (0, EVEN_IN_PAD - prm["ev_w_in"].shape[-1]))).astype(BF16)
    qa, ka, va, qkvb, gate, gbv, gbt = _even_in(
        x, modflat, mod_rows, row_of, layer, prm["norm_pre"][layer, 0][None], w_in, prm["ev_conv"][j],
        _lane_vec(prm["ev_a_log"][j]), _lane_vec(prm["ev_dt_bias"][j]), rope_tabs)
    if ctx is None:
        oa = _even_attn(qa, ka, va, None, prm["ev_sink"][j])
        o_f, o_b, s_f, s_b = _delta(qkvb, gbv, gbt, None)
    else:
        k_ctx, v_ctx, s0_f, s0_b = ctx
        flat = lambda t: t.reshape(t.shape[0], t.shape[1], A_KV)
        oa = _even_attn(qa, ka, va, (flat(k_ctx), flat(v_ctx)), prm["ev_sink"][j])
        o_f, o_b, s_f, s_b = _delta(qkvb, gbv, gbt, (s0_f, s0_b))
    x = _mixer_out([oa, o_f, o_b, gate], x, modflat, mod_rows, row_of, layer,
                   prm["ev_w_out"][j].astype(BF16), prm["norm_post"][layer, 0][None],
                   prm["ev_out_norm"][j][None])
    return x, (ka, va, s_f, s_b)


def _odd_layer(x, layer, j, prm, mod, rope_tabs, ctx):
    modflat, mod_rows, row_of = mod
    w_in = jnp.pad(prm["od_w_in"][j], ((0, 0), (0, ODD_IN_PAD - prm["od_w_in"].shape[-1]))).astype(BF16)
    w_q = prm["od_w_q_up"][j].reshape(C_Q_LORA, C_HEADS, C_NOPE + C_ROPE)
    w_q = jnp.pad(w_q, ((0, 0), (0, 0), (0, C_QK_PAD - C_NOPE - C_ROPE)))
    w_q = w_q.reshape(C_Q_LORA, C_HEADS * C_QK_PAD).astype(BF16)
    w_kv = prm["od_w_kv_up"][j].reshape(C_KV_LORA, C_HEADS, 2, C_NOPE)
    w_kv = w_kv.transpose(0, 2, 1, 3).reshape(C_KV_LORA, 2 * C_HEADS * C_NOPE).astype(BF16)
    q, ckv, kpe, kn, vv = _odd_in(
        x, modflat, mod_rows, row_of, layer, prm["norm_pre"][layer, 0][None], w_in,
        prm["od_q_norm"][j][None], prm["od_kv_norm"][j][None], w_q, w_kv, rope_tabs)
    if ctx is None:
        o = _mla_attn(q, kn, vv, None)
    else:
        ckv_ctx, kpe_ctx = ctx
        kpe_c = jnp.pad(kpe_ctx, ((0, 0), (0, 0), (0, LANES - C_ROPE)))
        o = _mla_attn(q, kn, vv, _kv_up(ckv_ctx, kpe_c, w_kv))
    x = _mixer_out([o], x, modflat, mod_rows, row_of, layer, prm["od_w_out"][j].astype(BF16),
                   prm["norm_post"][layer, 0][None])
    return x, (ckv, kpe[..., :C_ROPE])


def _trunk(x, prm, mod, rope_tabs, ctxs):
    depth = prm["w_mod"].shape[0]
    modflat, mod_rows, row_of = mod
    new_ctx = []
    for layer in range(depth):
        j = layer // 2
        ctx = None if ctxs is None else ctxs[layer]
        if layer % 2 == 0:
            x, nc = _even_layer(x, layer, j, prm, mod, rope_tabs, ctx)
        else:
            x, nc = _odd_layer(x, layer, j, prm, mod, rope_tabs, ctx)
        new_ctx.append(nc)
        x = _ffn(x, modflat, mod_rows, row_of, layer, prm["norm_pre"][layer, 1][None],
                 prm["norm_post"][layer, 1][None], prm["ffn_w_up"][layer].astype(BF16),
                 prm["ffn_conv"][layer], prm["ffn_w_down"][layer].astype(BF16))
    return x, new_ctx


def kernel(x_prompt, x_sample, cache_attn_k, cache_attn_v, state_delta_fwd, state_delta_bwd,
           cache_mla_ckv, cache_mla_kpe, c, c_ctx, w_mod, b_mod, norm_pre, norm_post,
           ffn_w_up, ffn_conv, ffn_w_down, ev_w_in, ev_conv, ev_a_log, ev_dt_bias, ev_sink,
           ev_out_norm, ev_w_out, od_w_in, od_q_norm, od_kv_norm, od_w_q_up, od_w_kv_up, od_w_out):
    prm = {
        "w_mod": w_mod, "b_mod": b_mod, "norm_pre": norm_pre, "norm_post": norm_post,
        "ffn_w_up": ffn_w_up, "ffn_conv": ffn_conv, "ffn_w_down": ffn_w_down,
        "ev_w_in": ev_w_in, "ev_conv": ev_conv, "ev_a_log": ev_a_log, "ev_dt_bias": ev_dt_bias,
        "ev_sink": ev_sink, "ev_out_norm": ev_out_norm, "ev_w_out": ev_w_out,
        "od_w_in": od_w_in, "od_q_norm": od_q_norm, "od_kv_norm": od_kv_norm,
        "od_w_q_up": od_w_q_up, "od_w_kv_up": od_w_kv_up, "od_w_out": od_w_out,
    }
    depth = w_mod.shape[0]
    n_c = c.shape[0]
    mod_rows = -(-(n_c + 1) // SUBLANES) * SUBLANES
    cvec = jnp.concatenate([c, c_ctx[None], jnp.zeros((mod_rows - n_c - 1, c.shape[1]), F32)], axis=0)
    modflat = _mod_all(cvec, w_mod, b_mod).reshape(depth * mod_rows * N_MOD, 1, D_MODEL)

    y_prompt, pctx = _trunk(x_prompt, prm, (modflat, mod_rows, lambda b: n_c), None, None)

    ctxs = []
    for layer in range(depth):
        j = layer // 2
        if layer % 2 == 0:
            ctxs.append((cache_attn_k[:, j], cache_attn_v[:, j], state_delta_fwd[:, j], state_delta_bwd[:, j]))
        else:
            ctxs.append((cache_mla_ckv[:, j], cache_mla_kpe[:, j]))
    y_sample, _ = _trunk(x_sample, prm, (modflat, mod_rows, lambda b: b), _rope_tables(x_sample.shape[1]), ctxs)

    bsz, seq = x_prompt.shape[:2]
    even, odd = pctx[0::2], pctx[1::2]
    kv_shape = (bsz, len(even), seq, A_KV_HEADS, A_HEAD_DIM)
    new_attn_k = jnp.stack([e[0] for e in even], axis=1).reshape(kv_shape)
    new_attn_v = jnp.stack([e[1] for e in even], axis=1).reshape(kv_shape)
    new_delta_fwd = jnp.stack([e[2] for e in even], axis=1)
    new_delta_bwd = jnp.stack([e[3] for e in even], axis=1)
    new_mla_ckv = jnp.stack([o[0] for o in odd], axis=1)
    new_mla_kpe = jnp.stack([o[1] for o in odd], axis=1)
    return (y_prompt, y_sample, new_attn_k, new_attn_v, new_delta_fwd, new_delta_bwd, new_mla_ckv, new_mla_kpe)
```

```python
import functools
import math

import jax
import jax.numpy as jnp
from jax import lax
from jax.experimental import pallas as pl
from jax.experimental.pallas import tpu as pltpu

F32 = jnp.float32
BF16 = jnp.bfloat16

D_MODEL = 1024
GRID_W = 64
A_HEADS = 8
A_KV_HEADS = 2
A_HEAD_DIM = 64
A_REP = A_HEADS // A_KV_HEADS
WINDOW = 128
Q_BLOCK = 128
B_HEADS = 4
B_DK = 128
B_DV = 128
CHUNK = 64
C_HEADS = 8
C_Q_LORA = 384
C_KV_LORA = 256
C_NOPE = 128
C_ROPE = 64
C_V = 128
C_QK_PAD = 256
C_V_PAD = 256
D_FF = 2816
ROPE_BASE = 10000.0
EPS = 1e-6
N_MOD = 6

A_Q = A_HEADS * A_HEAD_DIM
A_KV = A_KV_HEADS * A_HEAD_DIM
B_QK = B_HEADS * B_DK
B_VW = B_HEADS * B_DV
B_CONV_CH = 2 * B_QK + B_VW
KA0 = A_Q
VA0 = KA0 + A_KV
QKVB0 = VA0 + A_KV
GATE0 = QKVB0 + B_CONV_CH
AB0 = GATE0 + B_VW
LANES = 128
SUBLANES = 8
EVEN_IN_PAD = AB0 + LANES
ODD_IN_PAD = C_Q_LORA + C_KV_LORA + LANES
FF_TILE = 256
MLA_KEY_CHUNK = 1024
DELTA_PREP_CHUNKS = 2
VMEM_LIMIT_BYTES = 56 * 1024 * 1024


def _params(*sem):
    return pltpu.CompilerParams(dimension_semantics=sem, vmem_limit_bytes=VMEM_LIMIT_BYTES)


def _dot(a, b):
    return jnp.dot(a, b, preferred_element_type=F32)


def _dot_nt(a, b):
    return lax.dot_general(a, b, (((1,), (1,)), ((), ())), preferred_element_type=F32)


def _sigmoid(x):
    return 1.0 / (1.0 + jnp.exp(-x))


def _silu(x):
    return x * _sigmoid(x)


def _softplus(x):
    return jnp.maximum(x, 0.0) + jnp.log(1.0 + jnp.exp(-jnp.abs(x)))


def _rms(x, g):
    return x * lax.rsqrt(jnp.mean(x * x, axis=-1, keepdims=True) + EPS) * g


def _norm_mod(x, g, scale, shift):
    return _rms(x, g) * (1.0 + scale) + shift


def _rope(x, cos_t, sin_t):
    lane = lax.broadcasted_iota(jnp.int32, x.shape, 1)
    swapped = jnp.where((lane & 63) < 32, pltpu.roll(x, LANES - 32, 1), pltpu.roll(x, 32, 1))
    return x * cos_t + swapped * sin_t


def _dwconv3(u, up, dn, cw):
    tm = u.shape[0]
    row = lax.broadcasted_iota(jnp.int32, u.shape, 0)
    u_prev = jnp.where(row == 0, up, pltpu.roll(u, 1, 0))
    u_next = jnp.where(row == tm - 1, dn, pltpu.roll(u, tm - 1, 0))
    return cw[0:1] * u_prev + cw[1:2] * u + cw[2:3] * u_next


def _halo_specs(tm, seq):
    per = tm // SUBLANES
    last = seq // SUBLANES - 1
    prev = pl.BlockSpec((1, SUBLANES, D_MODEL), lambda b, i, *_: (b, jnp.maximum(i * per - 1, 0), 0))
    nxt = pl.BlockSpec((1, SUBLANES, D_MODEL), lambda b, i, *_: (b, jnp.minimum((i + 1) * per, last), 0))
    return prev, nxt


def _mod_spec(mod_rows, layer, row_of, k):
    return pl.BlockSpec((1, 1, D_MODEL),
                        lambda b, *_: ((layer * mod_rows + row_of(b)) * N_MOD + k, 0, 0))


def _const_spec(shape):
    nd = len(shape)
    return pl.BlockSpec(shape, lambda *_: (0,) * nd)


def _resident_spec(shape):
    nd = len(shape)
    return pl.BlockSpec(shape, lambda *_: (0,) * nd, pipeline_mode=pl.Buffered(1))


def _mod_body(c_ref, w_ref, b_ref, o_ref):
    s = _silu(c_ref[...]).astype(BF16)
    o_ref[0] = _dot(s, w_ref[0].astype(BF16)) + b_ref[0]


def _mod_all(cvec, w_mod, b_mod):
    depth, d, n = w_mod.shape
    rows = cvec.shape[0]
    tn = n // 4
    return pl.pallas_call(
        _mod_body, grid=(depth, n // tn),
        in_specs=[pl.BlockSpec((rows, d), lambda l, j: (0, 0)),
                  pl.BlockSpec((1, d, tn), lambda l, j: (l, 0, j)),
                  pl.BlockSpec((1, 1, tn), lambda l, j: (l, 0, j))],
        out_specs=pl.BlockSpec((1, rows, tn), lambda l, j: (l, 0, j)),
        out_shape=jax.ShapeDtypeStruct((depth, rows, n), F32),
        name="mod_vectors", compiler_params=_params("parallel", "parallel"),
    )(cvec, w_mod, b_mod.reshape(depth, 1, n))


def _even_in_body(rope, n_tiles, x_ref, xp_ref, xn_ref, sc_ref, sh_ref, g_ref, w_ref, cw_ref,
                  alog_ref, dt_ref, *rest):
    if rope:
        cos_ref, sin_ref = rest[:2]
        rest = rest[2:]
    qa_ref, ka_ref, va_ref, qkv_ref, gate_ref, gb_ref, gbt_ref = rest
    i = pl.program_id(1)
    g, sc, sh = g_ref[...], sc_ref[0], sh_ref[0]
    subs = _sub_tiles(x_ref.shape[1])
    for n, r in enumerate(subs):
        first, last = n == 0, n == len(subs) - 1
        rs = r.stop - r.start
        h = _norm_mod(x_ref[0, r], g, sc, sh).astype(BF16)
        before = xp_ref[0] if first else x_ref[0, r.start - SUBLANES:r.start]
        after = xn_ref[0] if last else x_ref[0, r.stop:r.stop + SUBLANES]
        hh = _norm_mod(jnp.concatenate([before, after], axis=0), g, sc, sh).astype(BF16)
        yb = _dot(jnp.concatenate([h, hh], axis=0), w_ref[:, QKVB0:GATE0])
        ya = _dot(h, w_ref[:, :QKVB0])
        up = jnp.where(i > 0 if first else True, yb[rs + SUBLANES - 1:rs + SUBLANES], 0.0)
        dn = jnp.where(i < n_tiles - 1 if last else True, yb[rs + SUBLANES:rs + SUBLANES + 1], 0.0)
        s = _silu(_dwconv3(yb[:rs], up, dn, cw_ref[...]))
        for j in range(B_CONV_CH // LANES):
            seg = s[:, j * LANES:(j + 1) * LANES]
            if j < 2 * B_HEADS:
                seg = seg * lax.rsqrt(jnp.sum(seg * seg, axis=-1, keepdims=True) + EPS)
            qkv_ref[0, r, j * LANES:(j + 1) * LANES] = seg

        yc = _dot(h, w_ref[:, GATE0:])
        qa = ya[:, :A_Q]
        ka = ya[:, KA0:VA0]
        if rope:
            cos_t, sin_t = cos_ref[r, :], sin_ref[r, :]
            qa = jnp.concatenate([_rope(qa[:, j * LANES:(j + 1) * LANES], cos_t, sin_t)
                                  for j in range(A_Q // LANES)], axis=1)
            ka = _rope(ka, cos_t, sin_t)
        qa_ref[0, r] = qa
        ka_ref[0, r] = ka
        va_ref[0, r] = ya[:, VA0:QKVB0]
        gate_ref[0, r] = yc[:, :B_VW]

        ab = yc[:, B_VW:]
        lane = lax.broadcasted_iota(jnp.int32, ab.shape, 1)
        gdec = -jnp.exp(alog_ref[...]) * _softplus(ab + dt_ref[...])
        gb = jnp.where((lane & B_HEADS) == 0, gdec, _sigmoid(ab))
        gb_ref[0, r] = gb
        gbt = gb.T
        for c in range(rs // CHUNK):
            gbt_ref[0, r.start // CHUNK + c] = gbt[:4 * B_HEADS, c * CHUNK:(c + 1) * CHUNK]


def _even_in(x, modflat, mod_rows, row_of, layer, g_pre, w_in, conv_w, alog_vec, dt_vec, rope_tabs):
    bsz, seq, d = x.shape
    tm = next(t for t in (1024, 512, 256) if seq % t == 0)
    nt = seq // tm
    rope = rope_tabs is not None
    prev, nxt = _halo_specs(tm, seq)
    in_specs = [pl.BlockSpec((1, tm, d), lambda b, i: (b, i, 0)), prev, nxt,
                _mod_spec(mod_rows, layer, row_of, 1), _mod_spec(mod_rows, layer, row_of, 0),
                _const_spec((1, d)), _resident_spec(w_in.shape), _const_spec(conv_w.shape),
                _const_spec((1, LANES)), _const_spec((1, LANES))]
    args = [x, x, x, modflat, modflat, g_pre, w_in, conv_w, alog_vec, dt_vec]
    if rope:
        in_specs += [pl.BlockSpec((tm, LANES), lambda b, i: (i, 0))] * 2
        args += list(rope_tabs)

    def out(width, dtype=F32):
        return (pl.BlockSpec((1, tm, width), lambda b, i: (b, i, 0)),
                jax.ShapeDtypeStruct((bsz, seq, width), dtype))

    outs = [out(A_Q), out(A_KV), out(A_KV), out(B_CONV_CH), out(B_VW), out(LANES),
            (pl.BlockSpec((1, tm // CHUNK, 4 * B_HEADS, CHUNK), lambda b, i: (b, i, 0, 0)),
             jax.ShapeDtypeStruct((bsz, seq // CHUNK, 4 * B_HEADS, CHUNK), F32))]
    return pl.pallas_call(
        functools.partial(_even_in_body, rope, nt), grid=(bsz, nt), in_specs=in_specs,
        out_specs=[o[0] for o in outs], out_shape=[o[1] for o in outs],
        name="even_in", compiler_params=_params("parallel", "parallel"),
    )(*args)


def _even_attn_body(windowed, q_ref, k_ref, v_ref, *rest):
    if windowed:
        kc_ref, vc_ref, sink_ref, o_ref = rest
    else:
        sink_ref, o_ref = rest
    i = pl.program_id(1)
    seq = k_ref.shape[1]
    q_all = q_ref[0] * (A_HEAD_DIM ** -0.5)
    rows = A_REP * Q_BLOCK
    if windowed:
        band = Q_BLOCK + 2 * WINDOW
        k0 = pl.multiple_of(jnp.clip(i * Q_BLOCK - WINDOW, 0, seq - band), Q_BLOCK)
        k_all = k_ref[0, pl.ds(k0, band), :].astype(BF16)
        v_all = v_ref[0, pl.ds(k0, band), :].astype(BF16)
        kc_all = kc_ref[0].astype(BF16)
        vc_all = vc_ref[0].astype(BF16)
        qpos = i * Q_BLOCK + (lax.broadcasted_iota(jnp.int32, (rows, band), 0) & (Q_BLOCK - 1))
        kpos = k0 + lax.broadcasted_iota(jnp.int32, (rows, band), 1)
        valid = jnp.abs(qpos - kpos) <= WINDOW
    else:
        k_all = k_ref[0].astype(BF16)
        v_all = v_ref[0].astype(BF16)
    rid = lax.broadcasted_iota(jnp.int32, (rows, 1), 0) // Q_BLOCK
    groups = range(A_KV_HEADS)
    cols = [slice(g * A_HEAD_DIM, (g + 1) * A_HEAD_DIM) for g in groups]
    q, sink = [], []
    for g in groups:
        heads = range(g * A_REP, (g + 1) * A_REP)
        q.append(jnp.concatenate([q_all[:, h * A_HEAD_DIM:(h + 1) * A_HEAD_DIM] for h in heads],
                                 axis=0).astype(BF16))
        sk = jnp.zeros((rows, 1), F32)
        for r, h in enumerate(heads):
            sk = jnp.where(rid == r, sink_ref[h], sk)
        sink.append(sk)
    s = [_dot_nt(q[g], k_all[:, cols[g]]) for g in groups]
    if windowed:
        s = [jnp.where(valid, s[g], -jnp.inf) for g in groups]
        s_c = [_dot_nt(q[g], kc_all[:, cols[g]]) for g in groups]
        m = [jnp.maximum(jnp.max(s_c[g], axis=-1, keepdims=True), sink[g]) for g in groups]
    else:
        m = sink
    m = [jnp.maximum(jnp.max(s[g], axis=-1, keepdims=True), m[g]) for g in groups]

    def with_ones(v):
        lane = lax.broadcasted_iota(jnp.int32, (v.shape[0], LANES - A_HEAD_DIM), 1)
        return jnp.concatenate([v, jnp.where(lane == 0, 1.0, 0.0).astype(BF16)], axis=1)

    e = [jnp.exp((s[g] - m[g]).astype(BF16)) for g in groups]
    o = [_dot(e[g], with_ones(v_all[:, cols[g]])) for g in groups]
    if windowed:
        e_c = [jnp.exp((s_c[g] - m[g]).astype(BF16)) for g in groups]
        o = [o[g] + _dot(e_c[g], with_ones(vc_all[:, cols[g]])) for g in groups]
    outs = []
    for g in groups:
        den = o[g][:, A_HEAD_DIM:A_HEAD_DIM + 1] + jnp.exp(sink[g] - m[g])
        og = o[g][:, :A_HEAD_DIM] / den
        outs += [og[r * Q_BLOCK:(r + 1) * Q_BLOCK] for r in range(A_REP)]
    o_ref[0] = jnp.concatenate(outs, axis=1).astype(o_ref.dtype)


def _even_attn(qa, ka, va, ctx, sink):
    bsz, seq, _ = qa.shape
    windowed = ctx is not None
    in_specs = [pl.BlockSpec((1, Q_BLOCK, A_Q), lambda b, i: (b, i, 0)),
                pl.BlockSpec((1, seq, A_KV), lambda b, i: (b, 0, 0)),
                pl.BlockSpec((1, seq, A_KV), lambda b, i: (b, 0, 0))]
    args = [qa, ka, va]
    if windowed:
        past = ctx[0].shape[1]
        in_specs += [pl.BlockSpec((1, past, A_KV), lambda b, i: (b, 0, 0))] * 2
        args += list(ctx)
    in_specs.append(pl.BlockSpec(memory_space=pltpu.SMEM))
    args.append(sink)
    return pl.pallas_call(
        functools.partial(_even_attn_body, windowed), grid=(bsz, seq // Q_BLOCK), in_specs=in_specs,
        out_specs=pl.BlockSpec((1, Q_BLOCK, A_Q), lambda b, i: (b, i, 0)),
        out_shape=jax.ShapeDtypeStruct((bsz, seq, A_Q), BF16),
        name="even_attn", compiler_params=_params("parallel", "parallel"),
    )(*args)


def _bf16_terms(x, n):
    terms = []
    for _ in range(n):
        t = x.astype(BF16).astype(F32)
        terms.append(t)
        x = x - t
    return terms


def _dot_split(a, b):
    ah, al = _bf16_terms(a, 2)
    bh, bl = _bf16_terms(b, 2)
    return _dot(jnp.concatenate([ah, al, ah, al], axis=1).astype(BF16),
                jnp.concatenate([bh, bh, bl, bl], axis=0).astype(BF16))


def _mask_dot(mask, x, mask_first):
    terms = _bf16_terms(x, 3)
    m16 = mask.astype(BF16)
    if mask_first:
        return _dot(jnp.concatenate([m16] * 3, axis=1), jnp.concatenate(terms, axis=0).astype(BF16))
    return _dot(jnp.concatenate(terms, axis=1).astype(BF16), jnp.concatenate([m16] * 3, axis=0))


def _delta_chunk_terms(chains, eye, interleaved=()):
    interleaved = list(interleaved)
    n = range(len(chains))
    q, k, v, gcol, grow, bcol, incl, strict, g_last = zip(*chains)
    decay = [jnp.where(incl[i], jnp.exp(jnp.where(incl[i], gcol[i] - grow[i], 0.0)), 0.0) for i in n]
    qs = [q[i] * (B_DK ** -0.5) for i in n]
    kbeta = [k[i] * bcol[i] for i in n]
    both = [_dot_nt(jnp.concatenate([kbeta[i], qs[i]], axis=0).astype(BF16), k[i].astype(BF16)) for i in n]
    qk = [jnp.where(incl[i], both[i][CHUNK:] * decay[i], 0.0).astype(BF16) for i in n]
    p = [-jnp.where(strict[i], both[i][:CHUNK] * decay[i], 0.0) for i in n]
    t_inv = [eye + p[i] for i in n]
    for _ in range(int(math.log2(CHUNK)) - 1):
        if interleaved:
            interleaved.pop(0)()
        p = [_dot_split(p[i], p[i]) for i in n]
        t_inv = [t_inv[i] + _dot_split(t_inv[i], p[i]) for i in n]
    while interleaved:
        interleaved.pop(0)()
    eg = [jnp.exp(gcol[i]) for i in n]
    rhs = [jnp.concatenate([v[i] * bcol[i], kbeta[i] * eg[i]], axis=1).astype(BF16) for i in n]
    uw = [_dot(t_inv[i].astype(BF16), rhs[i]).astype(BF16) for i in n]
    kd = [(k[i] * jnp.exp(g_last[i] - gcol[i])).T.astype(BF16) for i in n]
    qk_uw = [_dot(qk[i], uw[i]) for i in n]
    cp = [_dot(kd[i], uw[i]) for i in n]
    out = []
    for i in n:
        q_eff = qs[i] * eg[i] - qk_uw[i][:, B_DV:]
        pq = jnp.concatenate([cp[i][:, B_DV:], q_eff], axis=0).astype(BF16)
        out.append((pq, cp[i][:, :B_DV], qk_uw[i][:, :B_DV], jnp.exp(g_last[i])))
    return out


def _delta_body(zero_init, qf_ref, qb_ref, gf_ref, gb_ref, gtf_ref, gtb_ref, *rest):
    if not zero_init:
        s0f_ref, s0b_ref = rest[:2]
        rest = rest[2:]
    of_ref, ob_ref, sfo_ref, sbo_ref, s_scr = rest
    n = pl.program_id(1)
    n_chunks = gtf_ref.shape[1]
    n_chains = 2 * B_HEADS

    @pl.when(n == 0)
    def _():
        if zero_init:
            s_scr[...] = jnp.zeros_like(s_scr)
        else:
            s_scr[:B_HEADS] = s0f_ref[0]
            s_scr[B_HEADS:] = s0b_ref[0]

    ii = lax.broadcasted_iota(jnp.int32, (CHUNK, CHUNK), 0)
    jj = lax.broadcasted_iota(jnp.int32, (CHUNK, CHUNK), 1)
    lower, upper = ii >= jj, ii <= jj
    strict_lower, strict_upper = ii > jj, ii < jj
    tri_l, tri_u = lower.astype(F32), upper.astype(F32)
    eye = (ii == jj).astype(F32)

    def chains_of(q_ref, g_ref, gt_ref, c, lane0, tri_col, tri_row, incl, strict, last):
        rows = slice(c * CHUNK, (c + 1) * CHUNK)
        g = g_ref[0, rows, :]
        gc, gc_t = _mask_dot(tri_col, g, True), _mask_dot(tri_row, gt_ref[0, c], False)
        chains = []
        for h in range(B_HEADS):
            cq = slice(h * B_DK, (h + 1) * B_DK)
            ck = slice(B_QK + h * B_DK, B_QK + (h + 1) * B_DK)
            cv = slice(2 * B_QK + h * B_DV, 2 * B_QK + (h + 1) * B_DV)
            lg, lb = lane0 + h, lane0 + B_HEADS + h
            gcol = gc[:, lg:lg + 1]
            chains.append((q_ref[0, rows, cq], q_ref[0, rows, ck], q_ref[0, rows, cv], gcol,
                           gc_t[lg:lg + 1, :], g[:, lb:lb + 1], incl, strict, gcol[last:last + 1]))
        return chains

    states = [s_scr[idx] for idx in range(n_chains)]

    def scan_step(terms, cf, cb):
        def step():
            outs = []
            for idx, (pq, cc, oi, dec) in enumerate(terms):
                s = states[idx]
                r = _dot(pq, s.astype(BF16))
                outs.append(r[B_DK:] + oi)
                states[idx] = dec * s + cc - r[:B_DK]
            of_ref[0, cf * CHUNK:(cf + 1) * CHUNK, :] = jnp.concatenate(outs[:B_HEADS], axis=1)
            ob_ref[0, cb * CHUNK:(cb + 1) * CHUNK, :] = jnp.concatenate(outs[B_HEADS:], axis=1)
        return step

    pending = []
    for p in range(n_chunks // DELTA_PREP_CHUNKS):
        order = [(DELTA_PREP_CHUNKS * p + k, n_chunks - 1 - DELTA_PREP_CHUNKS * p - k)
                 for k in range(DELTA_PREP_CHUNKS)]
        chains = []
        for cf, cb in order:
            chains += chains_of(qf_ref, gf_ref, gtf_ref, cf, 0, tri_l, tri_u, lower, strict_lower, CHUNK - 1)
            chains += chains_of(qb_ref, gb_ref, gtb_ref, cb, 2 * B_HEADS, tri_u, tri_l, upper, strict_upper, 0)
        terms = _delta_chunk_terms(chains, eye, pending)
        pending = [scan_step(terms[k * n_chains:(k + 1) * n_chains], cf, cb) for k, (cf, cb) in enumerate(order)]
    for step in pending:
        step()
    for idx in range(n_chains):
        s_scr[idx] = states[idx]

    @pl.when(n == pl.num_programs(1) - 1)
    def _():
        sfo_ref[0] = s_scr[:B_HEADS]
        sbo_ref[0] = s_scr[B_HEADS:]


def _delta(qkvb, gbv, gbt, states):
    bsz, seq, _ = qkvb.shape
    per = 4
    tb = per * CHUNK
    nb = seq // tb
    zero_init = states is None
    n_chains = 2 * B_HEADS

    def fwd(*tail):
        return lambda b, n: (b, n) + tail

    def bwd(*tail):
        return lambda b, n: (b, nb - 1 - n) + tail

    st_spec = pl.BlockSpec((1, B_HEADS, B_DK, B_DV), lambda b, n: (b, 0, 0, 0))
    in_specs = [pl.BlockSpec((1, tb, B_CONV_CH), fwd(0)), pl.BlockSpec((1, tb, B_CONV_CH), bwd(0)),
                pl.BlockSpec((1, tb, LANES), fwd(0)), pl.BlockSpec((1, tb, LANES), bwd(0)),
                pl.BlockSpec((1, per, 4 * B_HEADS, CHUNK), fwd(0, 0)),
                pl.BlockSpec((1, per, 4 * B_HEADS, CHUNK), bwd(0, 0))]
    args = [qkvb, qkvb, gbv, gbv, gbt, gbt]
    if not zero_init:
        in_specs += [st_spec, st_spec]
        args += list(states)
    st_shape = jax.ShapeDtypeStruct((bsz, B_HEADS, B_DK, B_DV), F32)
    o_shape = jax.ShapeDtypeStruct((bsz, seq, B_VW), F32)
    return pl.pallas_call(
        functools.partial(_delta_body, zero_init), grid=(bsz, nb), in_specs=in_specs,
        out_specs=[pl.BlockSpec((1, tb, B_VW), fwd(0)), pl.BlockSpec((1, tb, B_VW), bwd(0)),
                   st_spec, st_spec],
        out_shape=[o_shape, o_shape, st_shape, st_shape],
        scratch_shapes=[pltpu.VMEM((n_chains, B_DK, B_DV), F32)],
        name="delta_rule", compiler_params=_params("parallel", "arbitrary"),
    )(*args)


def _sub_tiles(rows):
    step = min(rows, 512)
    return [slice(lo, lo + step) for lo in range(0, rows, step)]


def _even_out_body(oa_ref, of_ref, ob_ref, gate_ref, x_ref, g1_ref, on_ref, w_ref, gp_ref, o_ref):
    on = on_ref[...]
    for r in _sub_tiles(x_ref.shape[1]):
        ob = of_ref[0, r] + ob_ref[0, r]
        segs = [_rms(ob[:, h * B_DV:(h + 1) * B_DV], on) for h in range(B_HEADS)]
        obg = (jnp.concatenate(segs, axis=1) * _silu(gate_ref[0, r])).astype(BF16)
        mix = _dot(jnp.concatenate([oa_ref[0, r], obg], axis=1), w_ref[...])
        o_ref[0, r] = x_ref[0, r] + g1_ref[0] * _rms(mix, gp_ref[...])


def _odd_out_body(a_ref, x_ref, g1_ref, w_ref, gp_ref, o_ref):
    for r in _sub_tiles(x_ref.shape[1]):
        mix = _dot(a_ref[0, r], w_ref[...])
        o_ref[0, r] = x_ref[0, r] + g1_ref[0] * _rms(mix, gp_ref[...])


def _mixer_out(acts, x, modflat, mod_rows, row_of, layer, w_out, g_post, out_norm=None):
    bsz, seq, d = x.shape
    tm = next(t for t in (1024, 512, 256) if seq % t == 0)

    def row_spec(width):
        return pl.BlockSpec((1, tm, width), lambda b, i: (b, i, 0))

    in_specs = [row_spec(a.shape[-1]) for a in acts] + [row_spec(d), _mod_spec(mod_rows, layer, row_of, 2)]
    args = list(acts) + [x, modflat]
    if out_norm is not None:
        in_specs.append(_const_spec(out_norm.shape))
        args.append(out_norm)
    in_specs += [_const_spec(w_out.shape), _const_spec((1, d))]
    args += [w_out, g_post]
    return pl.pallas_call(
        _even_out_body if out_norm is not None else _odd_out_body,
        grid=(bsz, seq // tm), in_specs=in_specs, out_specs=row_spec(d),
        out_shape=jax.ShapeDtypeStruct(x.shape, F32),
        name="mixer_out", compiler_params=_params("parallel", "parallel"),
    )(*args)


def _write_keys_values(kv, kpe, k_ref, v_ref):
    kpe16 = kpe.astype(BF16)
    lane = lax.broadcasted_iota(jnp.int32, (kv.shape[0], C_V_PAD - C_V), 1)
    ones_col = jnp.where(lane == 0, 1.0, 0.0).astype(BF16)
    for hd in range(C_HEADS):
        base = hd * C_QK_PAD
        k_ref[0, :, base:base + C_NOPE] = kv[:, hd * C_NOPE:(hd + 1) * C_NOPE].astype(BF16)
        k_ref[0, :, base + C_NOPE:base + C_QK_PAD] = kpe16
        vbase = hd * C_V_PAD
        vcol = C_HEADS * C_NOPE + hd * C_V
        v_ref[0, :, vbase:vbase + C_V] = kv[:, vcol:vcol + C_V].astype(BF16)
        v_ref[0, :, vbase + C_V:vbase + C_V_PAD] = ones_col


def _odd_in_body(rope, x_ref, sc_ref, sh_ref, g_ref, w_ref, qn_ref, kvn_ref, wq_ref, wkv_ref, *rest):
    if rope:
        cos_ref, sin_ref = rest[:2]
        rest = rest[2:]
    q_ref, ckv_ref, kpe_ref, kn_ref, vv_ref = rest
    h = _norm_mod(x_ref[0], g_ref[...], sc_ref[0], sh_ref[0]).astype(BF16)
    y = _dot(h, w_ref[...])
    cq = _rms(y[:, :C_Q_LORA], qn_ref[...]).astype(BF16)
    ckv = _rms(y[:, C_Q_LORA:C_Q_LORA + C_KV_LORA], kvn_ref[...])
    q = _dot(cq, wq_ref[...])
    kv = _dot(ckv.astype(BF16), wkv_ref[...])
    kpe = y[:, C_Q_LORA + C_KV_LORA:]
    scale = (C_NOPE + C_ROPE) ** -0.5
    if rope:
        cos_t, sin_t = cos_ref[...], sin_ref[...]
        kpe = _rope(kpe, cos_t, sin_t)
    for hd in range(C_HEADS):
        base = hd * C_QK_PAD
        q_ref[0, :, base:base + C_NOPE] = (q[:, base:base + C_NOPE] * scale).astype(BF16)
        qpe = q[:, base + C_NOPE:base + C_QK_PAD]
        if rope:
            qpe = _rope(qpe, cos_t, sin_t)
        q_ref[0, :, base + C_NOPE:base + C_QK_PAD] = (qpe * scale).astype(BF16)
    ckv_ref[0] = ckv
    kpe_ref[0] = kpe
    _write_keys_values(kv, kpe, kn_ref, vv_ref)


def _odd_in(x, modflat, mod_rows, row_of, layer, g_pre, w_in, q_norm, kv_norm, w_q, w_kv, rope_tabs):
    bsz, seq, d = x.shape
    tm = 512 if seq % 512 == 0 else 256
    rope = rope_tabs is not None
    in_specs = [pl.BlockSpec((1, tm, d), lambda b, i: (b, i, 0)),
                _mod_spec(mod_rows, layer, row_of, 1), _mod_spec(mod_rows, layer, row_of, 0),
                _const_spec((1, d)), _const_spec(w_in.shape), _const_spec(q_norm.shape),
                _const_spec(kv_norm.shape), _const_spec(w_q.shape), _const_spec(w_kv.shape)]
    args = [x, modflat, modflat, g_pre, w_in, q_norm, kv_norm, w_q, w_kv]
    if rope:
        in_specs += [pl.BlockSpec((tm, LANES), lambda b, i: (i, 0))] * 2
        args += list(rope_tabs)

    def out(width, dtype):
        return (pl.BlockSpec((1, tm, width), lambda b, i: (b, i, 0)),
                jax.ShapeDtypeStruct((bsz, seq, width), dtype))

    outs = [out(C_HEADS * C_QK_PAD, BF16), out(C_KV_LORA, F32), out(LANES, F32),
            out(C_HEADS * C_QK_PAD, BF16), out(C_HEADS * C_V_PAD, BF16)]
    return pl.pallas_call(
        functools.partial(_odd_in_body, rope), grid=(bsz, seq // tm), in_specs=in_specs,
        out_specs=[o[0] for o in outs], out_shape=[o[1] for o in outs],
        name="odd_in", compiler_params=_params("parallel", "parallel"),
    )(*args)


def _kv_up_body(ckv_ref, kpe_ref, wkv_ref, kn_ref, vv_ref):
    _write_keys_values(_dot(ckv_ref[0].astype(BF16), wkv_ref[...]), kpe_ref[0], kn_ref, vv_ref)


def _kv_up(ckv, kpe, w_kv):
    bsz, n, r = ckv.shape

    def rows(width):
        return pl.BlockSpec((1, n, width), lambda b: (b, 0, 0))

    return pl.pallas_call(
        _kv_up_body, grid=(bsz,),
        in_specs=[rows(r), rows(LANES), _const_spec(w_kv.shape)],
        out_specs=[rows(C_HEADS * C_QK_PAD), rows(C_HEADS * C_V_PAD)],
        out_shape=[jax.ShapeDtypeStruct((bsz, n, C_HEADS * C_QK_PAD), BF16),
                   jax.ShapeDtypeStruct((bsz, n, C_HEADS * C_V_PAD), BF16)],
        name="kv_up", compiler_params=_params("parallel"),
    )(ckv, kpe, w_kv)


def _mla_attn_body(has_ctx, q_ref, k_ref, v_ref, *rest):
    if has_ctx:
        kc_ref, vc_ref, o_ref = rest
        sources = [(kc_ref, vc_ref), (k_ref, v_ref)]
    else:
        (o_ref,) = rest
        sources = [(k_ref, v_ref)]
    q = q_ref[0]
    chunks = []
    for kr, vr in sources:
        n = kr.shape[1]
        step = min(MLA_KEY_CHUNK, n)
        chunks += [(kr, vr, lo, step) for lo in range(0, n, step)]

    def scores(j):
        kr, _, lo, step = chunks[j]
        return _dot_nt(q, kr[0, lo:lo + step, :])

    nxt = scores(0)
    for j, (_, vr, lo, step) in enumerate(chunks):
        s = nxt
        if j + 1 < len(chunks):
            nxt = scores(j + 1)
        top = jnp.max(s, axis=-1, keepdims=True)
        m_new = top if j == 0 else jnp.maximum(m, top)
        e = jnp.exp((s - m_new).astype(BF16))
        part = _dot(e, vr[0, lo:lo + step, :])
        o = part if j == 0 else jnp.exp(m - m_new) * o + part
        m = m_new
    o_ref[0] = (o[:, :C_V] / o[:, C_V:C_V + 1]).astype(o_ref.dtype)


def _mla_attn(q, kc, vv, ctx):
    bsz, seq, _ = q.shape
    tq = next(t for t in (1024, 512, 256) if seq % t == 0)
    has_ctx = ctx is not None

    def keys(n, width):
        return pl.BlockSpec((1, n, width), lambda b, h, i: (b, 0, h))

    in_specs = [pl.BlockSpec((1, tq, C_QK_PAD), lambda b, h, i: (b, i, h)),
                keys(seq, C_QK_PAD), keys(seq, C_V_PAD)]
    args = [q, kc, vv]
    if has_ctx:
        past = ctx[0].shape[1]
        in_specs += [keys(past, C_QK_PAD), keys(past, C_V_PAD)]
        args += list(ctx)
    return pl.pallas_call(
        functools.partial(_mla_attn_body, has_ctx), grid=(bsz, C_HEADS, seq // tq), in_specs=in_specs,
        out_specs=pl.BlockSpec((1, tq, C_V), lambda b, h, i: (b, i, h)),
        out_shape=jax.ShapeDtypeStruct((bsz, seq, C_HEADS * C_V), BF16),
        name="mla_attn", compiler_params=_params("parallel", "parallel", "parallel"),
    )(*args)


def _ffn_rows(x, xh, has_up, has_dn, g, sc, sh, g2, gp, wu_ref, cw_ref, wd_ref):
    rows = x.shape[0]
    h = jnp.concatenate([_norm_mod(x, g, sc, sh).astype(BF16), _norm_mod(xh, g, sc, sh).astype(BF16)], axis=0)

    def up_proj(f):
        lo = f * FF_TILE
        return _dot(h, wu_ref[:, lo:lo + FF_TILE]), _dot(h, wu_ref[:, D_FF + lo:D_FF + lo + FF_TILE])

    def conv(u, lo):
        up = jnp.where(has_up, u[rows + SUBLANES - 1:rows + SUBLANES], 0.0)
        dn = jnp.where(has_dn, u[rows + SUBLANES:rows + SUBLANES + 1], 0.0)
        return _dwconv3(u[:rows], up, dn, cw_ref[:, lo:lo + FF_TILE])

    nf = D_FF // FF_TILE
    acts = []
    nxt = up_proj(0)
    for f in range(nf):
        ua, ub = nxt
        if f + 1 < nf:
            nxt = up_proj(f + 1)
        lo = f * FF_TILE
        acts.append((_silu(conv(ua, lo)) * conv(ub, D_FF + lo)).astype(BF16))
    mix = _dot(jnp.concatenate(acts, axis=1), wd_ref[...])
    return x + g2 * _rms(mix, gp)


def _ffn_body(n_tiles, n_sub, x_ref, xp_ref, xn_ref, sc_ref, sh_ref, g2_ref, g_ref, gp_ref,
              wu_ref, cw_ref, wd_ref, o_ref):
    i = pl.program_id(1)
    rs = x_ref.shape[1] // n_sub
    for j in range(n_sub):
        lo, hi = j * rs, (j + 1) * rs
        before = xp_ref[0] if j == 0 else x_ref[0, lo - SUBLANES:lo]
        after = xn_ref[0] if j == n_sub - 1 else x_ref[0, hi:hi + SUBLANES]
        has_up = i > 0 if j == 0 else True
        has_dn = i < n_tiles - 1 if j == n_sub - 1 else True
        o_ref[0, lo:hi] = _ffn_rows(
            x_ref[0, lo:hi], jnp.concatenate([before, after], axis=0), has_up, has_dn,
            g_ref[...], sc_ref[0], sh_ref[0], g2_ref[0], gp_ref[...], wu_ref, cw_ref, wd_ref)


def _ffn(x, modflat, mod_rows, row_of, layer, g_pre, g_post, w_up, conv_w, w_down):
    bsz, seq, d = x.shape
    tm = next(t for t in (1024, 512, 256) if seq % t == 0)
    n_sub = max(tm // 512, 1)
    nt = seq // tm
    prev, nxt = _halo_specs(tm, seq)
    in_specs = [pl.BlockSpec((1, tm, d), lambda b, i: (b, i, 0)), prev, nxt,
                _mod_spec(mod_rows, layer, row_of, 4), _mod_spec(mod_rows, layer, row_of, 3),
                _mod_spec(mod_rows, layer, row_of, 5), _const_spec((1, d)), _const_spec((1, d)),
                _resident_spec(w_up.shape), _const_spec(conv_w.shape), _resident_spec(w_down.shape)]
    return pl.pallas_call(
        functools.partial(_ffn_body, nt, n_sub), grid=(bsz, nt), in_specs=in_specs,
        out_specs=pl.BlockSpec((1, tm, d), lambda b, i: (b, i, 0)),
        out_shape=jax.ShapeDtypeStruct(x.shape, F32),
        name="conv_ffn", compiler_params=_params("parallel", "parallel"),
    )(x, x, x, modflat, modflat, modflat, g_pre, g_post, w_up, conv_w, w_down)


def _rope_tables(seq):
    t = jnp.arange(seq)
    row = (t // GRID_W).astype(F32)
    col = (t % GRID_W).astype(F32)
    n_freq = A_HEAD_DIM // 4
    inv_freq = ROPE_BASE ** (-jnp.arange(n_freq, dtype=F32) / n_freq)
    ang = jnp.concatenate([row[:, None] * inv_freq, col[:, None] * inv_freq], axis=-1)
    cos, sin = jnp.cos(ang), jnp.sin(ang)
    return (jnp.concatenate([cos, cos, cos, cos], axis=-1), jnp.concatenate([-sin, sin, -sin, sin], axis=-1))


def _lane_vec(p):
    v = jnp.zeros((2, 2, B_HEADS), F32).at[:, 0, :].set(p.astype(F32)).reshape(1, 4 * B_HEADS)
    return jnp.pad(v, ((0, 0), (0, LANES - 4 * B_HEADS)))


def _even_layer(x, layer, j, prm, mod, rope_tabs, ctx):
    modflat, mod_rows, row_of = mod
    w_in = jnp.pad(prm["ev_w_in"][j], ((0, 0), (0, EVEN_IN_PAD - prm["ev_w_in"].shape[-1]))).astype(BF16)
    qa, ka, va, qkvb, gate, gbv, gbt = _even_in(
        x, modflat, mod_rows, row_of, layer, prm["norm_pre"][layer, 0][None], w_in, prm["ev_conv"][j],
        _lane_vec(prm["ev_a_log"][j]), _lane_vec(prm["ev_dt_bias"][j]), rope_tabs)
    if ctx is None:
        oa = _even_attn(qa, ka, va, None, prm["ev_sink"][j])
        o_f, o_b, s_f, s_b = _delta(qkvb, gbv, gbt, None)
    else:
        k_ctx, v_ctx, s0_f, s0_b = ctx
        flat = lambda t: t.reshape(t.shape[0], t.shape[1], A_KV)
        oa = _even_attn(qa, ka, va, (flat(k_ctx), flat(v_ctx)), prm["ev_sink"][j])
        o_f, o_b, s_f, s_b = _delta(qkvb, gbv, gbt, (s0_f, s0_b))
    x = _mixer_out([oa, o_f, o_b, gate], x, modflat, mod_rows, row_of, layer,
                   prm["ev_w_out"][j].astype(BF16), prm["norm_post"][layer, 0][None],
                   prm["ev_out_norm"][j][None])
    return x, (ka, va, s_f, s_b)


def _odd_layer(x, layer, j, prm, mod, rope_tabs, ctx):
    modflat, mod_rows, row_of = mod
    w_in = jnp.pad(prm["od_w_in"][j], ((0, 0), (0, ODD_IN_PAD - prm["od_w_in"].shape[-1]))).astype(BF16)
    w_q = prm["od_w_q_up"][j].reshape(C_Q_LORA, C_HEADS, C_NOPE + C_ROPE)
    w_q = jnp.pad(w_q, ((0, 0), (0, 0), (0, C_QK_PAD - C_NOPE - C_ROPE)))
    w_q = w_q.reshape(C_Q_LORA, C_HEADS * C_QK_PAD).astype(BF16)
    w_kv = prm["od_w_kv_up"][j].reshape(C_KV_LORA, C_HEADS, 2, C_NOPE)
    w_kv = w_kv.transpose(0, 2, 1, 3).reshape(C_KV_LORA, 2 * C_HEADS * C_NOPE).astype(BF16)
    q, ckv, kpe, kn, vv = _odd_in(
        x, modflat, mod_rows, row_of, layer, prm["norm_pre"][layer, 0][None], w_in,
        prm["od_q_norm"][j][None], prm["od_kv_norm"][j][None], w_q, w_kv, rope_tabs)
    if ctx is None:
        o = _mla_attn(q, kn, vv, None)
    else:
        ckv_ctx, kpe_ctx = ctx
        kpe_c = jnp.pad(kpe_ctx, ((0, 0), (0, 0), (0, LANES - C_ROPE)))
        o = _mla_attn(q, kn, vv, _kv_up(ckv_ctx, kpe_c, w_kv))
    x = _mixer_out([o], x, modflat, mod_rows, row_of, layer, prm["od_w_out"][j].astype(BF16),
                   prm["norm_post"][layer, 0][None])
    return x, (ckv, kpe[..., :C_ROPE])


def _trunk(x, prm, mod, rope_tabs, ctxs):
    depth = prm["w_mod"].shape[0]
    modflat, mod_rows, row_of = mod
    new_ctx = []
    for layer in range(depth):
        j = layer // 2
        ctx = None if ctxs is None else ctxs[layer]
        if layer % 2 == 0:
            x, nc = _even_layer(x, layer, j, prm, mod, rope_tabs, ctx)
        else:
            x, nc = _odd_layer(x, layer, j, prm, mod, rope_tabs, ctx)
        new_ctx.append(nc)
        x = _ffn(x, modflat, mod_rows, row_of, layer, prm["norm_pre"][layer, 1][None],
                 prm["norm_post"][layer, 1][None], prm["ffn_w_up"][layer].astype(BF16),
                 prm["ffn_conv"][layer], prm["ffn_w_down"][layer].astype(BF16))
    return x, new_ctx


def kernel(x_prompt, x_sample, cache_attn_k, cache_attn_v, state_delta_fwd, state_delta_bwd,
           cache_mla_ckv, cache_mla_kpe, c, c_ctx, w_mod, b_mod, norm_pre, norm_post,
           ffn_w_up, ffn_conv, ffn_w_down, ev_w_in, ev_conv, ev_a_log, ev_dt_bias, ev_sink,
           ev_out_norm, ev_w_out, od_w_in, od_q_norm, od_kv_norm, od_w_q_up, od_w_kv_up, od_w_out):
    prm = {
        "w_mod": w_mod, "b_mod": b_mod, "norm_pre": norm_pre, "norm_post": norm_post,
        "ffn_w_up": ffn_w_up, "ffn_conv": ffn_conv, "ffn_w_down": ffn_w_down,
        "ev_w_in": ev_w_in, "ev_conv": ev_conv, "ev_a_log": ev_a_log, "ev_dt_bias": ev_dt_bias,
        "ev_sink": ev_sink, "ev_out_norm": ev_out_norm, "ev_w_out": ev_w_out,
        "od_w_in": od_w_in, "od_q_norm": od_q_norm, "od_kv_norm": od_kv_norm,
        "od_w_q_up": od_w_q_up, "od_w_kv_up": od_w_kv_up, "od_w_out": od_w_out,
    }
    depth = w_mod.shape[0]
    n_c = c.shape[0]
    mod_rows = -(-(n_c + 1) // SUBLANES) * SUBLANES
    cvec = jnp.concatenate([c, c_ctx[None], jnp.zeros((mod_rows - n_c - 1, c.shape[1]), F32)], axis=0)
    modflat = _mod_all(cvec, w_mod, b_mod).reshape(depth * mod_rows * N_MOD, 1, D_MODEL)

    y_prompt, pctx = _trunk(x_prompt, prm, (modflat, mod_rows, lambda b: n_c), None, None)

    ctxs = []
    for layer in range(depth):
        j = layer // 2
        if layer % 2 == 0:
            ctxs.append((cache_attn_k[:, j], cache_attn_v[:, j], state_delta_fwd[:, j], state_delta_bwd[:, j]))
        else:
            ctxs.append((cache_mla_ckv[:, j], cache_mla_kpe[:, j]))
    y_sample, _ = _trunk(x_sample, prm, (modflat, mod_rows, lambda b: b), _rope_tables(x_sample.shape[1]), ctxs)

    bsz, seq = x_prompt.shape[:2]
    even, odd = pctx[0::2], pctx[1::2]
    kv_shape = (bsz, len(even), seq, A_KV_HEADS, A_HEAD_DIM)
    new_attn_k = jnp.stack([e[0] for e in even], axis=1).reshape(kv_shape)
    new_attn_v = jnp.stack([e[1] for e in even], axis=1).reshape(kv_shape)
    new_delta_fwd = jnp.stack([e[2] for e in even], axis=1)
    new_delta_bwd = jnp.stack([e[3] for e in even], axis=1)
    new_mla_ckv = jnp.stack([o[0] for o in odd], axis=1)
    new_mla_kpe = jnp.stack([o[1] for o in odd], axis=1)
    return (y_prompt, y_sample, new_attn_k, new_attn_v, new_delta_fwd, new_delta_bwd, new_mla_ckv, new_mla_kpe)
```

```python
import functools
import math

import jax
import jax.numpy as jnp
from jax import lax
from jax.experimental import pallas as pl
from jax.experimental.pallas import tpu as pltpu

F32 = jnp.float32
BF16 = jnp.bfloat16

D_MODEL = 1024
GRID_W = 64
A_HEADS = 8
A_KV_HEADS = 2
A_HEAD_DIM = 64
A_REP = A_HEADS // A_KV_HEADS
WINDOW = 128
Q_BLOCK = 128
B_HEADS = 4
B_DK = 128
B_DV = 128
CHUNK = 64
C_HEADS = 8
C_Q_LORA = 384
C_KV_LORA = 256
C_NOPE = 128
C_ROPE = 64
C_V = 128
C_QK_PAD = 256
C_V_PAD = 256
D_FF = 2816
ROPE_BASE = 10000.0
EPS = 1e-6
N_MOD = 6

A_Q = A_HEADS * A_HEAD_DIM
A_KV = A_KV_HEADS * A_HEAD_DIM
B_QK = B_HEADS * B_DK
B_VW = B_HEADS * B_DV
B_CONV_CH = 2 * B_QK + B_VW
KA0 = A_Q
VA0 = KA0 + A_KV
QKVB0 = VA0 + A_KV
GATE0 = QKVB0 + B_CONV_CH
AB0 = GATE0 + B_VW
LANES = 128
SUBLANES = 8
EVEN_IN_PAD = AB0 + LANES
ODD_IN_PAD = C_Q_LORA + C_KV_LORA + LANES
FF_TILE = 256
MLA_KEY_CHUNK = 1024
MAX_ROW_TILE = 1024
SUB_TILE = 512
MIN_ROW_TILE = 256
DELTA_STEP_CHUNKS = 4
DELTA_PREP_CHUNKS = 2
VMEM_LIMIT_BYTES = 56 * 1024 * 1024


def _params(*sem):
    return pltpu.CompilerParams(dimension_semantics=sem, vmem_limit_bytes=VMEM_LIMIT_BYTES)


def _dot(a, b):
    return jnp.dot(a, b, preferred_element_type=F32)


def _dot_nt(a, b):
    return lax.dot_general(a, b, (((1,), (1,)), ((), ())), preferred_element_type=F32)


def _sigmoid(x):
    return 1.0 / (1.0 + jnp.exp(-x))


def _silu(x):
    return x * _sigmoid(x)


def _softplus(x):
    return jnp.maximum(x, 0.0) + jnp.log(1.0 + jnp.exp(-jnp.abs(x)))


def _rms(x, g):
    return x * lax.rsqrt(jnp.mean(x * x, axis=-1, keepdims=True) + EPS) * g


def _norm_mod(x, g, scale, shift):
    return _rms(x, g) * (1.0 + scale) + shift


def _rope(x, cos_t, sin_t):
    lane = lax.broadcasted_iota(jnp.int32, x.shape, 1)
    swapped = jnp.where((lane & 63) < 32, pltpu.roll(x, LANES - 32, 1), pltpu.roll(x, 32, 1))
    return x * cos_t + swapped * sin_t


def _dwconv3(u, up, dn, cw):
    tm = u.shape[0]
    row = lax.broadcasted_iota(jnp.int32, u.shape, 0)
    u_prev = jnp.where(row == 0, up, pltpu.roll(u, 1, 0))
    u_next = jnp.where(row == tm - 1, dn, pltpu.roll(u, tm - 1, 0))
    return cw[0:1] * u_prev + cw[1:2] * u + cw[2:3] * u_next


def _halo_specs(tm, seq):
    per = tm // SUBLANES
    last = seq // SUBLANES - 1
    prev = pl.BlockSpec((1, SUBLANES, D_MODEL), lambda b, i, *_: (b, jnp.maximum(i * per - 1, 0), 0))
    nxt = pl.BlockSpec((1, SUBLANES, D_MODEL), lambda b, i, *_: (b, jnp.minimum((i + 1) * per, last), 0))
    return prev, nxt


def _mod_spec(mod_rows, layer, row_of, k):
    return pl.BlockSpec((1, 1, D_MODEL),
                        lambda b, *_: ((layer * mod_rows + row_of(b)) * N_MOD + k, 0, 0))


def _row_tile(seq, cap):
    tile = cap
    while tile > MIN_ROW_TILE and seq % tile:
        tile //= 2
    assert seq % tile == 0, (seq, tile)
    return tile


def _const_spec(shape):
    nd = len(shape)
    return pl.BlockSpec(shape, lambda *_: (0,) * nd)


def _resident_spec(shape):
    nd = len(shape)
    return pl.BlockSpec(shape, lambda *_: (0,) * nd, pipeline_mode=pl.Buffered(1))


def _mod_body(c_ref, w_ref, b_ref, o_ref):
    s = _silu(c_ref[...]).astype(BF16)
    o_ref[0] = _dot(s, w_ref[0].astype(BF16)) + b_ref[0]


def _mod_all(cvec, w_mod, b_mod):
    depth, d, n = w_mod.shape
    rows = cvec.shape[0]
    tn = n // 4
    return pl.pallas_call(
        _mod_body, grid=(depth, n // tn),
        in_specs=[pl.BlockSpec((rows, d), lambda l, j: (0, 0)),
                  pl.BlockSpec((1, d, tn), lambda l, j: (l, 0, j)),
                  pl.BlockSpec((1, 1, tn), lambda l, j: (l, 0, j))],
        out_specs=pl.BlockSpec((1, rows, tn), lambda l, j: (l, 0, j)),
        out_shape=jax.ShapeDtypeStruct((depth, rows, n), F32),
        name="mod_vectors", compiler_params=_params("parallel", "parallel"),
    )(cvec, w_mod, b_mod.reshape(depth, 1, n))


def _even_in_body(rope, n_tiles, x_ref, xp_ref, xn_ref, sc_ref, sh_ref, g_ref, w_ref, cw_ref,
                  alog_ref, dt_ref, *rest):
    if rope:
        cos_ref, sin_ref = rest[:2]
        rest = rest[2:]
    qa_ref, ka_ref, va_ref, qkv_ref, gate_ref, gb_ref, gbt_ref = rest
    i = pl.program_id(1)
    g, sc, sh = g_ref[...], sc_ref[0], sh_ref[0]
    subs = _sub_tiles(x_ref.shape[1])
    for n, r in enumerate(subs):
        first, last = n == 0, n == len(subs) - 1
        rs = r.stop - r.start
        h = _norm_mod(x_ref[0, r], g, sc, sh).astype(BF16)
        before = xp_ref[0] if first else x_ref[0, r.start - SUBLANES:r.start]
        after = xn_ref[0] if last else x_ref[0, r.stop:r.stop + SUBLANES]
        hh = _norm_mod(jnp.concatenate([before, after], axis=0), g, sc, sh).astype(BF16)
        yb = _dot(jnp.concatenate([h, hh], axis=0), w_ref[:, QKVB0:GATE0])
        ya = _dot(h, w_ref[:, :QKVB0])
        up = jnp.where(i > 0 if first else True, yb[rs + SUBLANES - 1:rs + SUBLANES], 0.0)
        dn = jnp.where(i < n_tiles - 1 if last else True, yb[rs + SUBLANES:rs + SUBLANES + 1], 0.0)
        s = _silu(_dwconv3(yb[:rs], up, dn, cw_ref[...]))
        for j in range(B_CONV_CH // LANES):
            seg = s[:, j * LANES:(j + 1) * LANES]
            if j < 2 * B_HEADS:
                seg = seg * lax.rsqrt(jnp.sum(seg * seg, axis=-1, keepdims=True) + EPS)
            qkv_ref[0, r, j * LANES:(j + 1) * LANES] = seg

        yc = _dot(h, w_ref[:, GATE0:])
        qa = ya[:, :A_Q]
        ka = ya[:, KA0:VA0]
        if rope:
            cos_t, sin_t = cos_ref[r, :], sin_ref[r, :]
            qa = jnp.concatenate([_rope(qa[:, j * LANES:(j + 1) * LANES], cos_t, sin_t)
                                  for j in range(A_Q // LANES)], axis=1)
            ka = _rope(ka, cos_t, sin_t)
        qa_ref[0, r] = qa
        ka_ref[0, r] = ka
        va_ref[0, r] = ya[:, VA0:QKVB0]
        gate_ref[0, r] = yc[:, :B_VW]

        ab = yc[:, B_VW:]
        lane = lax.broadcasted_iota(jnp.int32, ab.shape, 1)
        gdec = -jnp.exp(alog_ref[...]) * _softplus(ab + dt_ref[...])
        gb = jnp.where((lane & B_HEADS) == 0, gdec, _sigmoid(ab))
        gb_ref[0, r] = gb
        gbt = gb.T
        for c in range(rs // CHUNK):
            gbt_ref[0, r.start // CHUNK + c] = gbt[:4 * B_HEADS, c * CHUNK:(c + 1) * CHUNK]


def _even_in(x, modflat, mod_rows, row_of, layer, g_pre, w_in, conv_w, alog_vec, dt_vec, rope_tabs):
    bsz, seq, d = x.shape
    tm = _row_tile(seq, MAX_ROW_TILE)
    nt = seq // tm
    rope = rope_tabs is not None
    prev, nxt = _halo_specs(tm, seq)
    in_specs = [pl.BlockSpec((1, tm, d), lambda b, i: (b, i, 0)), prev, nxt,
                _mod_spec(mod_rows, layer, row_of, 1), _mod_spec(mod_rows, layer, row_of, 0),
                _const_spec((1, d)), _resident_spec(w_in.shape), _const_spec(conv_w.shape),
                _const_spec((1, LANES)), _const_spec((1, LANES))]
    args = [x, x, x, modflat, modflat, g_pre, w_in, conv_w, alog_vec, dt_vec]
    if rope:
        in_specs += [pl.BlockSpec((tm, LANES), lambda b, i: (i, 0))] * 2
        args += list(rope_tabs)

    def out(width, dtype=F32):
        return (pl.BlockSpec((1, tm, width), lambda b, i: (b, i, 0)),
                jax.ShapeDtypeStruct((bsz, seq, width), dtype))

    outs = [out(A_Q), out(A_KV), out(A_KV), out(B_CONV_CH), out(B_VW), out(LANES),
            (pl.BlockSpec((1, tm // CHUNK, 4 * B_HEADS, CHUNK), lambda b, i: (b, i, 0, 0)),
             jax.ShapeDtypeStruct((bsz, seq // CHUNK, 4 * B_HEADS, CHUNK), F32))]
    return pl.pallas_call(
        functools.partial(_even_in_body, rope, nt), grid=(bsz, nt), in_specs=in_specs,
        out_specs=[o[0] for o in outs], out_shape=[o[1] for o in outs],
        name="even_in", compiler_params=_params("parallel", "parallel"),
    )(*args)


def _even_attn_body(windowed, q_ref, k_ref, v_ref, *rest):
    if windowed:
        kc_ref, vc_ref, sink_ref, o_ref = rest
    else:
        sink_ref, o_ref = rest
    i = pl.program_id(1)
    seq = k_ref.shape[1]
    q_all = q_ref[0] * (A_HEAD_DIM ** -0.5)
    rows = A_REP * Q_BLOCK
    if windowed:
        band = Q_BLOCK + 2 * WINDOW
        k0 = pl.multiple_of(jnp.clip(i * Q_BLOCK - WINDOW, 0, seq - band), Q_BLOCK)
        k_all = k_ref[0, pl.ds(k0, band), :].astype(BF16)
        v_all = v_ref[0, pl.ds(k0, band), :].astype(BF16)
        kc_all = kc_ref[0].astype(BF16)
        vc_all = vc_ref[0].astype(BF16)
        qpos = i * Q_BLOCK + (lax.broadcasted_iota(jnp.int32, (rows, band), 0) & (Q_BLOCK - 1))
        kpos = k0 + lax.broadcasted_iota(jnp.int32, (rows, band), 1)
        valid = jnp.abs(qpos - kpos) <= WINDOW
    else:
        k_all = k_ref[0].astype(BF16)
        v_all = v_ref[0].astype(BF16)
    rid = lax.broadcasted_iota(jnp.int32, (rows, 1), 0) // Q_BLOCK
    groups = range(A_KV_HEADS)
    cols = [slice(g * A_HEAD_DIM, (g + 1) * A_HEAD_DIM) for g in groups]
    q, sink = [], []
    for g in groups:
        heads = range(g * A_REP, (g + 1) * A_REP)
        q.append(jnp.concatenate([q_all[:, h * A_HEAD_DIM:(h + 1) * A_HEAD_DIM] for h in heads],
                                 axis=0).astype(BF16))
        sk = jnp.zeros((rows, 1), F32)
        for r, h in enumerate(heads):
            sk = jnp.where(rid == r, sink_ref[h], sk)
        sink.append(sk)
    s = [_dot_nt(q[g], k_all[:, cols[g]]) for g in groups]
    if windowed:
        s = [jnp.where(valid, s[g], -jnp.inf) for g in groups]
        s_c = [_dot_nt(q[g], kc_all[:, cols[g]]) for g in groups]
        m = [jnp.maximum(jnp.max(s_c[g], axis=-1, keepdims=True), sink[g]) for g in groups]
    else:
        m = sink
    m = [jnp.maximum(jnp.max(s[g], axis=-1, keepdims=True), m[g]) for g in groups]

    def with_ones(v):
        lane = lax.broadcasted_iota(jnp.int32, (v.shape[0], LANES - A_HEAD_DIM), 1)
        return jnp.concatenate([v, jnp.where(lane == 0, 1.0, 0.0).astype(BF16)], axis=1)

    e = [jnp.exp((s[g] - m[g]).astype(BF16)) for g in groups]
    o = [_dot(e[g], with_ones(v_all[:, cols[g]])) for g in groups]
    if windowed:
        e_c = [jnp.exp((s_c[g] - m[g]).astype(BF16)) for g in groups]
        o = [o[g] + _dot(e_c[g], with_ones(vc_all[:, cols[g]])) for g in groups]
    outs = []
    for g in groups:
        den = o[g][:, A_HEAD_DIM:A_HEAD_DIM + 1] + jnp.exp(sink[g] - m[g])
        og = o[g][:, :A_HEAD_DIM] / den
        outs += [og[r * Q_BLOCK:(r + 1) * Q_BLOCK] for r in range(A_REP)]
    o_ref[0] = jnp.concatenate(outs, axis=1).astype(o_ref.dtype)


def _even_attn(qa, ka, va, ctx, sink):
    bsz, seq, _ = qa.shape
    windowed = ctx is not None
    in_specs = [pl.BlockSpec((1, Q_BLOCK, A_Q), lambda b, i: (b, i, 0)),
                pl.BlockSpec((1, seq, A_KV), lambda b, i: (b, 0, 0)),
                pl.BlockSpec((1, seq, A_KV), lambda b, i: (b, 0, 0))]
    args = [qa, ka, va]
    if windowed:
        past = ctx[0].shape[1]
        in_specs += [pl.BlockSpec((1, past, A_KV), lambda b, i: (b, 0, 0))] * 2
        args += list(ctx)
    in_specs.append(pl.BlockSpec(memory_space=pltpu.SMEM))
    args.append(sink)
    return pl.pallas_call(
        functools.partial(_even_attn_body, windowed), grid=(bsz, seq // Q_BLOCK), in_specs=in_specs,
        out_specs=pl.BlockSpec((1, Q_BLOCK, A_Q), lambda b, i: (b, i, 0)),
        out_shape=jax.ShapeDtypeStruct((bsz, seq, A_Q), BF16),
        name="even_attn", compiler_params=_params("parallel", "parallel"),
    )(*args)


def _bf16_terms(x, n):
    terms = []
    for _ in range(n):
        t = x.astype(BF16).astype(F32)
        terms.append(t)
        x = x - t
    return terms


def _dot_split(a, b):
    ah, al = _bf16_terms(a, 2)
    bh, bl = _bf16_terms(b, 2)
    return _dot(jnp.concatenate([ah, al, ah, al], axis=1).astype(BF16),
                jnp.concatenate([bh, bh, bl, bl], axis=0).astype(BF16))


def _mask_dot(mask, x, mask_first):
    terms = _bf16_terms(x, 3)
    m16 = mask.astype(BF16)
    if mask_first:
        return _dot(jnp.concatenate([m16] * 3, axis=1), jnp.concatenate(terms, axis=0).astype(BF16))
    return _dot(jnp.concatenate(terms, axis=1).astype(BF16), jnp.concatenate([m16] * 3, axis=0))


def _delta_chunk_terms(chains, eye, interleaved=()):
    interleaved = list(interleaved)
    n = range(len(chains))
    q, k, v, gcol, grow, bcol, incl, strict, g_last = zip(*chains)
    decay = [jnp.where(incl[i], jnp.exp(jnp.where(incl[i], gcol[i] - grow[i], 0.0)), 0.0) for i in n]
    qs = [q[i] * (B_DK ** -0.5) for i in n]
    kbeta = [k[i] * bcol[i] for i in n]
    both = [_dot_nt(jnp.concatenate([kbeta[i], qs[i]], axis=0).astype(BF16), k[i].astype(BF16)) for i in n]
    qk = [jnp.where(incl[i], both[i][CHUNK:] * decay[i], 0.0).astype(BF16) for i in n]
    p = [-jnp.where(strict[i], both[i][:CHUNK] * decay[i], 0.0) for i in n]
    t_inv = [eye + p[i] for i in n]
    for _ in range(int(math.log2(CHUNK)) - 1):
        if interleaved:
            interleaved.pop(0)()
        p = [_dot_split(p[i], p[i]) for i in n]
        t_inv = [t_inv[i] + _dot_split(t_inv[i], p[i]) for i in n]
    while interleaved:
        interleaved.pop(0)()
    eg = [jnp.exp(gcol[i]) for i in n]
    rhs = [jnp.concatenate([v[i] * bcol[i], kbeta[i] * eg[i]], axis=1).astype(BF16) for i in n]
    uw = [_dot(t_inv[i].astype(BF16), rhs[i]).astype(BF16) for i in n]
    kd = [(k[i] * jnp.exp(g_last[i] - gcol[i])).T.astype(BF16) for i in n]
    qk_uw = [_dot(qk[i], uw[i]) for i in n]
    cp = [_dot(kd[i], uw[i]) for i in n]
    out = []
    for i in n:
        q_eff = qs[i] * eg[i] - qk_uw[i][:, B_DV:]
        pq = jnp.concatenate([cp[i][:, B_DV:], q_eff], axis=0).astype(BF16)
        out.append((pq, cp[i][:, :B_DV], qk_uw[i][:, :B_DV], jnp.exp(g_last[i])))
    return out


def _delta_body(zero_init, qf_ref, qb_ref, gf_ref, gb_ref, gtf_ref, gtb_ref, *rest):
    if not zero_init:
        s0f_ref, s0b_ref = rest[:2]
        rest = rest[2:]
    of_ref, ob_ref, sfo_ref, sbo_ref, s_scr = rest
    n = pl.program_id(1)
    n_chunks = gtf_ref.shape[1]
    n_chains = 2 * B_HEADS

    @pl.when(n == 0)
    def _():
        if zero_init:
            s_scr[...] = jnp.zeros_like(s_scr)
        else:
            s_scr[:B_HEADS] = s0f_ref[0]
            s_scr[B_HEADS:] = s0b_ref[0]

    ii = lax.broadcasted_iota(jnp.int32, (CHUNK, CHUNK), 0)
    jj = lax.broadcasted_iota(jnp.int32, (CHUNK, CHUNK), 1)
    lower, upper = ii >= jj, ii <= jj
    strict_lower, strict_upper = ii > jj, ii < jj
    tri_l, tri_u = lower.astype(F32), upper.astype(F32)
    eye = (ii == jj).astype(F32)

    def chains_of(q_ref, g_ref, gt_ref, c, lane0, tri_col, tri_row, incl, strict, last):
        rows = slice(c * CHUNK, (c + 1) * CHUNK)
        g = g_ref[0, rows, :]
        gc, gc_t = _mask_dot(tri_col, g, True), _mask_dot(tri_row, gt_ref[0, c], False)
        chains = []
        for h in range(B_HEADS):
            cq = slice(h * B_DK, (h + 1) * B_DK)
            ck = slice(B_QK + h * B_DK, B_QK + (h + 1) * B_DK)
            cv = slice(2 * B_QK + h * B_DV, 2 * B_QK + (h + 1) * B_DV)
            lg, lb = lane0 + h, lane0 + B_HEADS + h
            gcol = gc[:, lg:lg + 1]
            chains.append((q_ref[0, rows, cq], q_ref[0, rows, ck], q_ref[0, rows, cv], gcol,
                           gc_t[lg:lg + 1, :], g[:, lb:lb + 1], incl, strict, gcol[last:last + 1]))
        return chains

    states = [s_scr[idx] for idx in range(n_chains)]

    def scan_step(terms, cf, cb):
        def step():
            outs = []
            for idx, (pq, cc, oi, dec) in enumerate(terms):
                s = states[idx]
                r = _dot(pq, s.astype(BF16))
                outs.append(r[B_DK:] + oi)
                states[idx] = dec * s + cc - r[:B_DK]
            of_ref[0, cf * CHUNK:(cf + 1) * CHUNK, :] = jnp.concatenate(outs[:B_HEADS], axis=1)
            ob_ref[0, cb * CHUNK:(cb + 1) * CHUNK, :] = jnp.concatenate(outs[B_HEADS:], axis=1)
        return step

    pending = []
    for p in range(n_chunks // DELTA_PREP_CHUNKS):
        order = [(DELTA_PREP_CHUNKS * p + k, n_chunks - 1 - DELTA_PREP_CHUNKS * p - k)
                 for k in range(DELTA_PREP_CHUNKS)]
        chains = []
        for cf, cb in order:
            chains += chains_of(qf_ref, gf_ref, gtf_ref, cf, 0, tri_l, tri_u, lower, strict_lower, CHUNK - 1)
            chains += chains_of(qb_ref, gb_ref, gtb_ref, cb, 2 * B_HEADS, tri_u, tri_l, upper, strict_upper, 0)
        terms = _delta_chunk_terms(chains, eye, pending)
        pending = [scan_step(terms[k * n_chains:(k + 1) * n_chains], cf, cb) for k, (cf, cb) in enumerate(order)]
    for step in pending:
        step()
    for idx in range(n_chains):
        s_scr[idx] = states[idx]

    @pl.when(n == pl.num_programs(1) - 1)
    def _():
        sfo_ref[0] = s_scr[:B_HEADS]
        sbo_ref[0] = s_scr[B_HEADS:]


def _delta(qkvb, gbv, gbt, states):
    bsz, seq, _ = qkvb.shape
    per = DELTA_STEP_CHUNKS
    tb = per * CHUNK
    nb = seq // tb
    zero_init = states is None
    n_chains = 2 * B_HEADS

    def fwd(*tail):
        return lambda b, n: (b, n) + tail

    def bwd(*tail):
        return lambda b, n: (b, nb - 1 - n) + tail

    st_spec = pl.BlockSpec((1, B_HEADS, B_DK, B_DV), lambda b, n: (b, 0, 0, 0))
    in_specs = [pl.BlockSpec((1, tb, B_CONV_CH), fwd(0)), pl.BlockSpec((1, tb, B_CONV_CH), bwd(0)),
                pl.BlockSpec((1, tb, LANES), fwd(0)), pl.BlockSpec((1, tb, LANES), bwd(0)),
                pl.BlockSpec((1, per, 4 * B_HEADS, CHUNK), fwd(0, 0)),
                pl.BlockSpec((1, per, 4 * B_HEADS, CHUNK), bwd(0, 0))]
    args = [qkvb, qkvb, gbv, gbv, gbt, gbt]
    if not zero_init:
        in_specs += [st_spec, st_spec]
        args += list(states)
    st_shape = jax.ShapeDtypeStruct((bsz, B_HEADS, B_DK, B_DV), F32)
    o_shape = jax.ShapeDtypeStruct((bsz, seq, B_VW), F32)
    return pl.pallas_call(
        functools.partial(_delta_body, zero_init), grid=(bsz, nb), in_specs=in_specs,
        out_specs=[pl.BlockSpec((1, tb, B_VW), fwd(0)), pl.BlockSpec((1, tb, B_VW), bwd(0)),
                   st_spec, st_spec],
        out_shape=[o_shape, o_shape, st_shape, st_shape],
        scratch_shapes=[pltpu.VMEM((n_chains, B_DK, B_DV), F32)],
        name="delta_rule", compiler_params=_params("parallel", "arbitrary"),
    )(*args)


def _sub_tiles(rows):
    step = min(rows, SUB_TILE)
    return [slice(lo, lo + step) for lo in range(0, rows, step)]


def _even_out_body(oa_ref, of_ref, ob_ref, gate_ref, x_ref, g1_ref, on_ref, w_ref, gp_ref, o_ref):
    on = on_ref[...]
    for r in _sub_tiles(x_ref.shape[1]):
        ob = of_ref[0, r] + ob_ref[0, r]
        segs = [_rms(ob[:, h * B_DV:(h + 1) * B_DV], on) for h in range(B_HEADS)]
        obg = (jnp.concatenate(segs, axis=1) * _silu(gate_ref[0, r])).astype(BF16)
        mix = _dot(jnp.concatenate([oa_ref[0, r], obg], axis=1), w_ref[...])
        o_ref[0, r] = x_ref[0, r] + g1_ref[0] * _rms(mix, gp_ref[...])


def _odd_out_body(a_ref, x_ref, g1_ref, w_ref, gp_ref, o_ref):
    for r in _sub_tiles(x_ref.shape[1]):
        mix = _dot(a_ref[0, r], w_ref[...])
        o_ref[0, r] = x_ref[0, r] + g1_ref[0] * _rms(mix, gp_ref[...])


def _mixer_out(acts, x, modflat, mod_rows, row_of, layer, w_out, g_post, out_norm=None):
    bsz, seq, d = x.shape
    tm = _row_tile(seq, MAX_ROW_TILE)

    def row_spec(width):
        return pl.BlockSpec((1, tm, width), lambda b, i: (b, i, 0))

    in_specs = [row_spec(a.shape[-1]) for a in acts] + [row_spec(d), _mod_spec(mod_rows, layer, row_of, 2)]
    args = list(acts) + [x, modflat]
    if out_norm is not None:
        in_specs.append(_const_spec(out_norm.shape))
        args.append(out_norm)
    in_specs += [_const_spec(w_out.shape), _const_spec((1, d))]
    args += [w_out, g_post]
    return pl.pallas_call(
        _even_out_body if out_norm is not None else _odd_out_body,
        grid=(bsz, seq // tm), in_specs=in_specs, out_specs=row_spec(d),
        out_shape=jax.ShapeDtypeStruct(x.shape, F32),
        name="mixer_out", compiler_params=_params("parallel", "parallel"),
    )(*args)


def _write_keys_values(kv, kpe, k_ref, v_ref):
    kpe16 = kpe.astype(BF16)
    lane = lax.broadcasted_iota(jnp.int32, (kv.shape[0], C_V_PAD - C_V), 1)
    ones_col = jnp.where(lane == 0, 1.0, 0.0).astype(BF16)
    for hd in range(C_HEADS):
        base = hd * C_QK_PAD
        k_ref[0, :, base:base + C_NOPE] = kv[:, hd * C_NOPE:(hd + 1) * C_NOPE].astype(BF16)
        k_ref[0, :, base + C_NOPE:base + C_QK_PAD] = kpe16
        vbase = hd * C_V_PAD
        vcol = C_HEADS * C_NOPE + hd * C_V
        v_ref[0, :, vbase:vbase + C_V] = kv[:, vcol:vcol + C_V].astype(BF16)
        v_ref[0, :, vbase + C_V:vbase + C_V_PAD] = ones_col


def _odd_in_body(rope, x_ref, sc_ref, sh_ref, g_ref, w_ref, qn_ref, kvn_ref, wq_ref, wkv_ref, *rest):
    if rope:
        cos_ref, sin_ref = rest[:2]
        rest = rest[2:]
    q_ref, ckv_ref, kpe_ref, kn_ref, vv_ref = rest
    h = _norm_mod(x_ref[0], g_ref[...], sc_ref[0], sh_ref[0]).astype(BF16)
    y = _dot(h, w_ref[...])
    cq = _rms(y[:, :C_Q_LORA], qn_ref[...]).astype(BF16)
    ckv = _rms(y[:, C_Q_LORA:C_Q_LORA + C_KV_LORA], kvn_ref[...])
    q = _dot(cq, wq_ref[...])
    kv = _dot(ckv.astype(BF16), wkv_ref[...])
    kpe = y[:, C_Q_LORA + C_KV_LORA:]
    scale = (C_NOPE + C_ROPE) ** -0.5
    if rope:
        cos_t, sin_t = cos_ref[...], sin_ref[...]
        kpe = _rope(kpe, cos_t, sin_t)
    for hd in range(C_HEADS):
        base = hd * C_QK_PAD
        q_ref[0, :, base:base + C_NOPE] = (q[:, base:base + C_NOPE] * scale).astype(BF16)
        qpe = q[:, base + C_NOPE:base + C_QK_PAD]
        if rope:
            qpe = _rope(qpe, cos_t, sin_t)
        q_ref[0, :, base + C_NOPE:base + C_QK_PAD] = (qpe * scale).astype(BF16)
    ckv_ref[0] = ckv
    kpe_ref[0] = kpe
    _write_keys_values(kv, kpe, kn_ref, vv_ref)


def _odd_in(x, modflat, mod_rows, row_of, layer, g_pre, w_in, q_norm, kv_norm, w_q, w_kv, rope_tabs):
    bsz, seq, d = x.shape
    tm = _row_tile(seq, SUB_TILE)
    rope = rope_tabs is not None
    in_specs = [pl.BlockSpec((1, tm, d), lambda b, i: (b, i, 0)),
                _mod_spec(mod_rows, layer, row_of, 1), _mod_spec(mod_rows, layer, row_of, 0),
                _const_spec((1, d)), _const_spec(w_in.shape), _const_spec(q_norm.shape),
                _const_spec(kv_norm.shape), _const_spec(w_q.shape), _const_spec(w_kv.shape)]
    args = [x, modflat, modflat, g_pre, w_in, q_norm, kv_norm, w_q, w_kv]
    if rope:
        in_specs += [pl.BlockSpec((tm, LANES), lambda b, i: (i, 0))] * 2
        args += list(rope_tabs)

    def out(width, dtype):
        return (pl.BlockSpec((1, tm, width), lambda b, i: (b, i, 0)),
                jax.ShapeDtypeStruct((bsz, seq, width), dtype))

    outs = [out(C_HEADS * C_QK_PAD, BF16), out(C_KV_LORA, F32), out(LANES, F32),
            out(C_HEADS * C_QK_PAD, BF16), out(C_HEADS * C_V_PAD, BF16)]
    return pl.pallas_call(
        functools.partial(_odd_in_body, rope), grid=(bsz, seq // tm), in_specs=in_specs,
        out_specs=[o[0] for o in outs], out_shape=[o[1] for o in outs],
        name="odd_in", compiler_params=_params("parallel", "parallel"),
    )(*args)


def _kv_up_body(ckv_ref, kpe_ref, wkv_ref, kn_ref, vv_ref):
    _write_keys_values(_dot(ckv_ref[0].astype(BF16), wkv_ref[...]), kpe_ref[0], kn_ref, vv_ref)


def _kv_up(ckv, kpe, w_kv):
    bsz, n, r = ckv.shape

    def rows(width):
        return pl.BlockSpec((1, n, width), lambda b: (b, 0, 0))

    return pl.pallas_call(
        _kv_up_body, grid=(bsz,),
        in_specs=[rows(r), rows(LANES), _const_spec(w_kv.shape)],
        out_specs=[rows(C_HEADS * C_QK_PAD), rows(C_HEADS * C_V_PAD)],
        out_shape=[jax.ShapeDtypeStruct((bsz, n, C_HEADS * C_QK_PAD), BF16),
                   jax.ShapeDtypeStruct((bsz, n, C_HEADS * C_V_PAD), BF16)],
        name="kv_up", compiler_params=_params("parallel"),
    )(ckv, kpe, w_kv)


def _mla_attn_body(has_ctx, q_ref, k_ref, v_ref, *rest):
    if has_ctx:
        kc_ref, vc_ref, o_ref = rest
        sources = [(kc_ref, vc_ref), (k_ref, v_ref)]
    else:
        (o_ref,) = rest
        sources = [(k_ref, v_ref)]
    q = q_ref[0]
    chunks = []
    for kr, vr in sources:
        n = kr.shape[1]
        step = min(MLA_KEY_CHUNK, n)
        chunks += [(kr, vr, lo, step) for lo in range(0, n, step)]

    def scores(j):
        kr, _, lo, step = chunks[j]
        return _dot_nt(q, kr[0, lo:lo + step, :])

    nxt = scores(0)
    for j, (_, vr, lo, step) in enumerate(chunks):
        s = nxt
        if j + 1 < len(chunks):
            nxt = scores(j + 1)
        top = jnp.max(s, axis=-1, keepdims=True)
        m_new = top if j == 0 else jnp.maximum(m, top)
        e = jnp.exp((s - m_new).astype(BF16))
        part = _dot(e, vr[0, lo:lo + step, :])
        o = part if j == 0 else jnp.exp(m - m_new) * o + part
        m = m_new
    o_ref[0] = (o[:, :C_V] / o[:, C_V:C_V + 1]).astype(o_ref.dtype)


def _mla_attn(q, kc, vv, ctx):
    bsz, seq, _ = q.shape
    tq = _row_tile(seq, MAX_ROW_TILE)
    has_ctx = ctx is not None

    def keys(n, width):
        return pl.BlockSpec((1, n, width), lambda b, h, i: (b, 0, h))

    in_specs = [pl.BlockSpec((1, tq, C_QK_PAD), lambda b, h, i: (b, i, h)),
                keys(seq, C_QK_PAD), keys(seq, C_V_PAD)]
    args = [q, kc, vv]
    if has_ctx:
        past = ctx[0].shape[1]
        in_specs += [keys(past, C_QK_PAD), keys(past, C_V_PAD)]
        args += list(ctx)
    return pl.pallas_call(
        functools.partial(_mla_attn_body, has_ctx), grid=(bsz, C_HEADS, seq // tq), in_specs=in_specs,
        out_specs=pl.BlockSpec((1, tq, C_V), lambda b, h, i: (b, i, h)),
        out_shape=jax.ShapeDtypeStruct((bsz, seq, C_HEADS * C_V), BF16),
        name="mla_attn", compiler_params=_params("parallel", "parallel", "parallel"),
    )(*args)


def _ffn_rows(x, xh, has_up, has_dn, g, sc, sh, g2, gp, wu_ref, cw_ref, wd_ref):
    rows = x.shape[0]
    h = jnp.concatenate([_norm_mod(x, g, sc, sh).astype(BF16), _norm_mod(xh, g, sc, sh).astype(BF16)], axis=0)

    def up_proj(f):
        lo = f * FF_TILE
        return _dot(h, wu_ref[:, lo:lo + FF_TILE]), _dot(h, wu_ref[:, D_FF + lo:D_FF + lo + FF_TILE])

    def conv(u, lo):
        up = jnp.where(has_up, u[rows + SUBLANES - 1:rows + SUBLANES], 0.0)
        dn = jnp.where(has_dn, u[rows + SUBLANES:rows + SUBLANES + 1], 0.0)
        return _dwconv3(u[:rows], up, dn, cw_ref[:, lo:lo + FF_TILE])

    nf = D_FF // FF_TILE
    acts = []
    nxt = up_proj(0)
    for f in range(nf):
        ua, ub = nxt
        if f + 1 < nf:
            nxt = up_proj(f + 1)
        lo = f * FF_TILE
        acts.append((_silu(conv(ua, lo)) * conv(ub, D_FF + lo)).astype(BF16))
    mix = _dot(jnp.concatenate(acts, axis=1), wd_ref[...])
    return x + g2 * _rms(mix, gp)


def _ffn_body(n_tiles, n_sub, x_ref, xp_ref, xn_ref, sc_ref, sh_ref, g2_ref, g_ref, gp_ref,
              wu_ref, cw_ref, wd_ref, o_ref):
    i = pl.program_id(1)
    rs = x_ref.shape[1] // n_sub
    for j in range(n_sub):
        lo, hi = j * rs, (j + 1) * rs
        before = xp_ref[0] if j == 0 else x_ref[0, lo - SUBLANES:lo]
        after = xn_ref[0] if j == n_sub - 1 else x_ref[0, hi:hi + SUBLANES]
        has_up = i > 0 if j == 0 else True
        has_dn = i < n_tiles - 1 if j == n_sub - 1 else True
        o_ref[0, lo:hi] = _ffn_rows(
            x_ref[0, lo:hi], jnp.concatenate([before, after], axis=0), has_up, has_dn,
            g_ref[...], sc_ref[0], sh_ref[0], g2_ref[0], gp_ref[...], wu_ref, cw_ref, wd_ref)


def _ffn(x, modflat, mod_rows, row_of, layer, g_pre, g_post, w_up, conv_w, w_down):
    bsz, seq, d = x.shape
    tm = _row_tile(seq, MAX_ROW_TILE)
    n_sub = max(tm // SUB_TILE, 1)
    nt = seq // tm
    prev, nxt = _halo_specs(tm, seq)
    in_specs = [pl.BlockSpec((1, tm, d), lambda b, i: (b, i, 0)), prev, nxt,
                _mod_spec(mod_rows, layer, row_of, 4), _mod_spec(mod_rows, layer, row_of, 3),
                _mod_spec(mod_rows, layer, row_of, 5), _const_spec((1, d)), _const_spec((1, d)),
                _resident_spec(w_up.shape), _const_spec(conv_w.shape), _resident_spec(w_down.shape)]
    return pl.pallas_call(
        functools.partial(_ffn_body, nt, n_sub), grid=(bsz, nt), in_specs=in_specs,
        out_specs=pl.BlockSpec((1, tm, d), lambda b, i: (b, i, 0)),
        out_shape=jax.ShapeDtypeStruct(x.shape, F32),
        name="conv_ffn", compiler_params=_params("parallel", "parallel"),
    )(x, x, x, modflat, modflat, modflat, g_pre, g_post, w_up, conv_w, w_down)


def _rope_tables(seq):
    t = jnp.arange(seq)
    row = (t // GRID_W).astype(F32)
    col = (t % GRID_W).astype(F32)
    n_freq = A_HEAD_DIM // 4
    inv_freq = ROPE_BASE ** (-jnp.arange(n_freq, dtype=F32) / n_freq)
    ang = jnp.concatenate([row[:, None] * inv_freq, col[:, None] * inv_freq], axis=-1)
    cos, sin = jnp.cos(ang), jnp.sin(ang)
    return (jnp.concatenate([cos, cos, cos, cos], axis=-1), jnp.concatenate([-sin, sin, -sin, sin], axis=-1))


def _lane_vec(p):
    v = jnp.zeros((2, 2, B_HEADS), F32).at[:, 0, :].set(p.astype(F32)).reshape(1, 4 * B_HEADS)
    return jnp.pad(v, ((0, 0), (0, LANES - 4 * B_HEADS)))


def _even_layer(x, layer, j, prm, mod, rope_tabs, ctx):
    modflat, mod_rows, row_of = mod
    w_in = jnp.pad(prm["ev_w_in"][j], ((0, 0), (0, EVEN_IN_PAD - prm["ev_w_in"].shape[-1]))).astype(BF16)
    qa, ka, va, qkvb, gate, gbv, gbt = _even_in(
        x, modflat, mod_rows, row_of, layer, prm["norm_pre"][layer, 0][None], w_in, prm["ev_conv"][j],
        _lane_vec(prm["ev_a_log"][j]), _lane_vec(prm["ev_dt_bias"][j]), rope_tabs)
    if ctx is None:
        oa = _even_attn(qa, ka, va, None, prm["ev_sink"][j])
        o_f, o_b, s_f, s_b = _delta(qkvb, gbv, gbt, None)
    else:
        k_ctx, v_ctx, s0_f, s0_b = ctx
        flat = lambda t: t.reshape(t.shape[0], t.shape[1], A_KV)
        oa = _even_attn(qa, ka, va, (flat(k_ctx), flat(v_ctx)), prm["ev_sink"][j])
        o_f, o_b, s_f, s_b = _delta(qkvb, gbv, gbt, (s0_f, s0_b))
    x = _mixer_out([oa, o_f, o_b, gate], x, modflat, mod_rows, row_of, layer,
                   prm["ev_w_out"][j].astype(BF16), prm["norm_post"][layer, 0][None],
                   prm["ev_out_norm"][j][None])
    return x, (ka, va, s_f, s_b)


def _odd_layer(x, layer, j, prm, mod, rope_tabs, ctx):
    modflat, mod_rows, row_of = mod
    w_in = jnp.pad(prm["od_w_in"][j], ((0, 0), (0, ODD_IN_PAD - prm["od_w_in"].shape[-1]))).astype(BF16)
    w_q = prm["od_w_q_up"][j].reshape(C_Q_LORA, C_HEADS, C_NOPE + C_ROPE)
    w_q = jnp.pad(w_q, ((0, 0), (0, 0), (0, C_QK_PAD - C_NOPE - C_ROPE)))
    w_q = w_q.reshape(C_Q_LORA, C_HEADS * C_QK_PAD).astype(BF16)
    w_kv = prm["od_w_kv_up"][j].reshape(C_KV_LORA, C_HEADS, 2, C_NOPE)
    w_kv = w_kv.transpose(0, 2, 1, 3).reshape(C_KV_LORA, 2 * C_HEADS * C_NOPE).astype(BF16)
    q, ckv, kpe, kn, vv = _odd_in(
        x, modflat, mod_rows, row_of, layer, prm["norm_pre"][layer, 0][None], w_in,
        prm["od_q_norm"][j][None], prm["od_kv_norm"][j][None], w_q, w_kv, rope_tabs)
    if ctx is None:
        o = _mla_attn(q, kn, vv, None)
    else:
        ckv_ctx, kpe_ctx = ctx
        kpe_c = jnp.pad(kpe_ctx, ((0, 0), (0, 0), (0, LANES - C_ROPE)))
        o = _mla_attn(q, kn, vv, _kv_up(ckv_ctx, kpe_c, w_kv))
    x = _mixer_out([o], x, modflat, mod_rows, row_of, layer, prm["od_w_out"][j].astype(BF16),
                   prm["norm_post"][layer, 0][None])
    return x, (ckv, kpe[..., :C_ROPE])


def _trunk(x, prm, mod, rope_tabs, ctxs):
    depth = prm["w_mod"].shape[0]
    modflat, mod_rows, row_of = mod
    new_ctx = []
    for layer in range(depth):
        j = layer // 2
        ctx = None if ctxs is None else ctxs[layer]
        if layer % 2 == 0:
            x, nc = _even_layer(x, layer, j, prm, mod, rope_tabs, ctx)
        else:
            x, nc = _odd_layer(x, layer, j, prm, mod, rope_tabs, ctx)
        new_ctx.append(nc)
        x = _ffn(x, modflat, mod_rows, row_of, layer, prm["norm_pre"][layer, 1][None],
                 prm["norm_post"][layer, 1][None], prm["ffn_w_up"][layer].astype(BF16),
                 prm["ffn_conv"][layer], prm["ffn_w_down"][layer].astype(BF16))
    return x, new_ctx


def kernel(x_prompt, x_sample, cache_attn_k, cache_attn_v, state_delta_fwd, state_delta_bwd,
           cache_mla_ckv, cache_mla_kpe, c, c_ctx, w_mod, b_mod, norm_pre, norm_post,
           ffn_w_up, ffn_conv, ffn_w_down, ev_w_in, ev_conv, ev_a_log, ev_dt_bias, ev_sink,
           ev_out_norm, ev_w_out, od_w_in, od_q_norm, od_kv_norm, od_w_q_up, od_w_kv_up, od_w_out):
    prm = {
        "w_mod": w_mod, "b_mod": b_mod, "norm_pre": norm_pre, "norm_post": norm_post,
        "ffn_w_up": ffn_w_up, "ffn_conv": ffn_conv, "ffn_w_down": ffn_w_down,
        "ev_w_in": ev_w_in, "ev_conv": ev_conv, "ev_a_log": ev_a_log, "ev_dt_bias": ev_dt_bias,
        "ev_sink": ev_sink, "ev_out_norm": ev_out_norm, "ev_w_out": ev_w_out,
        "od_w_in": od_w_in, "od_q_norm": od_q_norm, "od_kv_norm": od_kv_norm,
        "od_w_q_up": od_w_q_up, "od_w_kv_up": od_w_kv_up, "od_w_out": od_w_out,
    }
    depth = w_mod.shape[0]
    n_c = c.shape[0]
    mod_rows = -(-(n_c + 1) // SUBLANES) * SUBLANES
    cvec = jnp.concatenate([c, c_ctx[None], jnp.zeros((mod_rows - n_c - 1, c.shape[1]), F32)], axis=0)
    modflat = _mod_all(cvec, w_mod, b_mod).reshape(depth * mod_rows * N_MOD, 1, D_MODEL)

    y_prompt, pctx = _trunk(x_prompt, prm, (modflat, mod_rows, lambda b: n_c), None, None)

    ctxs = []
    for layer in range(depth):
        j = layer // 2
        if layer % 2 == 0:
            ctxs.append((cache_attn_k[:, j], cache_attn_v[:, j], state_delta_fwd[:, j], state_delta_bwd[:, j]))
        else:
            ctxs.append((cache_mla_ckv[:, j], cache_mla_kpe[:, j]))
    y_sample, _ = _trunk(x_sample, prm, (modflat, mod_rows, lambda b: b), _rope_tables(x_sample.shape[1]), ctxs)

    bsz, seq = x_prompt.shape[:2]
    even, odd = pctx[0::2], pctx[1::2]
    kv_shape = (bsz, len(even), seq, A_KV_HEADS, A_HEAD_DIM)
    new_attn_k = jnp.stack([e[0] for e in even], axis=1).reshape(kv_shape)
    new_attn_v = jnp.stack([e[1] for e in even], axis=1).reshape(kv_shape)
    new_delta_fwd = jnp.stack([e[2] for e in even], axis=1)
    new_delta_bwd = jnp.stack([e[3] for e in even], axis=1)
    new_mla_ckv = jnp.stack([o[0] for o in odd], axis=1)
    new_mla_kpe = jnp.stack([o[1] for o in odd], axis=1)
    return (y_prompt, y_sample, new_attn_k, new_attn_v, new_delta_fwd, new_delta_bwd, new_mla_ckv, new_mla_kpe)
```

```python
import functools
import math

import jax
import jax.numpy as jnp
from jax import lax
from jax.experimental import pallas as pl
from jax.experimental.pallas import tpu as pltpu

F32 = jnp.float32
BF16 = jnp.bfloat16

D_MODEL = 1024
GRID_W = 64
A_HEADS = 8
A_KV_HEADS = 2
A_HEAD_DIM = 64
A_REP = A_HEADS // A_KV_HEADS
WINDOW = 128
Q_BLOCK = 128
B_HEADS = 4
B_DK = 128
B_DV = 128
CHUNK = 64
C_HEADS = 8
C_Q_LORA = 384
C_KV_LORA = 256
C_NOPE = 128
C_ROPE = 64
C_V = 128
C_QK_PAD = 256
C_V_PAD = 256
D_FF = 2816
ROPE_BASE = 10000.0
EPS = 1e-6
N_MOD = 6

A_Q = A_HEADS * A_HEAD_DIM
A_KV = A_KV_HEADS * A_HEAD_DIM
B_QK = B_HEADS * B_DK
B_VW = B_HEADS * B_DV
B_CONV_CH = 2 * B_QK + B_VW
KA0 = A_Q
VA0 = KA0 + A_KV
QKVB0 = VA0 + A_KV
GATE0 = QKVB0 + B_CONV_CH
AB0 = GATE0 + B_VW
LANES = 128
SUBLANES = 8
EVEN_IN_PAD = AB0 + LANES
ODD_IN_PAD = C_Q_LORA + C_KV_LORA + LANES
FF_TILE = 256
MLA_KEY_CHUNK = 1024
MAX_ROW_TILE = 1024
SUB_TILE = 512
MIN_ROW_TILE = 256
DELTA_STEP_CHUNKS = 8
DELTA_PREP_CHUNKS = 2
VMEM_LIMIT_BYTES = 56 * 1024 * 1024


def _params(*sem):
    return pltpu.CompilerParams(dimension_semantics=sem, vmem_limit_bytes=VMEM_LIMIT_BYTES)


def _dot(a, b):
    return jnp.dot(a, b, preferred_element_type=F32)


def _dot_nt(a, b):
    return lax.dot_general(a, b, (((1,), (1,)), ((), ())), preferred_element_type=F32)


def _sigmoid(x):
    return 1.0 / (1.0 + jnp.exp(-x))


def _silu(x):
    return x * _sigmoid(x)


def _softplus(x):
    return jnp.maximum(x, 0.0) + jnp.log(1.0 + jnp.exp(-jnp.abs(x)))


def _rms(x, g):
    return x * lax.rsqrt(jnp.mean(x * x, axis=-1, keepdims=True) + EPS) * g


def _norm_mod(x, g, scale, shift):
    return _rms(x, g) * (1.0 + scale) + shift


def _rope(x, cos_t, sin_t):
    lane = lax.broadcasted_iota(jnp.int32, x.shape, 1)
    swapped = jnp.where((lane & 63) < 32, pltpu.roll(x, LANES - 32, 1), pltpu.roll(x, 32, 1))
    return x * cos_t + swapped * sin_t


def _dwconv3(u, up, dn, cw):
    tm = u.shape[0]
    row = lax.broadcasted_iota(jnp.int32, u.shape, 0)
    u_prev = jnp.where(row == 0, up, pltpu.roll(u, 1, 0))
    u_next = jnp.where(row == tm - 1, dn, pltpu.roll(u, tm - 1, 0))
    return cw[0:1] * u_prev + cw[1:2] * u + cw[2:3] * u_next


def _halo_specs(tm, seq):
    per = tm // SUBLANES
    last = seq // SUBLANES - 1
    prev = pl.BlockSpec((1, SUBLANES, D_MODEL), lambda b, i, *_: (b, jnp.maximum(i * per - 1, 0), 0))
    nxt = pl.BlockSpec((1, SUBLANES, D_MODEL), lambda b, i, *_: (b, jnp.minimum((i + 1) * per, last), 0))
    return prev, nxt


def _mod_spec(mod_rows, layer, row_of, k):
    return pl.BlockSpec((1, 1, D_MODEL),
                        lambda b, *_: ((layer * mod_rows + row_of(b)) * N_MOD + k, 0, 0))


def _row_tile(seq, cap):
    tile = cap
    while tile > MIN_ROW_TILE and seq % tile:
        tile //= 2
    assert seq % tile == 0, (seq, tile)
    return tile


def _const_spec(shape):
    nd = len(shape)
    return pl.BlockSpec(shape, lambda *_: (0,) * nd)


def _resident_spec(shape):
    nd = len(shape)
    return pl.BlockSpec(shape, lambda *_: (0,) * nd, pipeline_mode=pl.Buffered(1))


def _mod_body(c_ref, w_ref, b_ref, o_ref):
    s = _silu(c_ref[...]).astype(BF16)
    o_ref[0] = _dot(s, w_ref[0].astype(BF16)) + b_ref[0]


def _mod_all(cvec, w_mod, b_mod):
    depth, d, n = w_mod.shape
    rows = cvec.shape[0]
    tn = n // 4
    return pl.pallas_call(
        _mod_body, grid=(depth, n // tn),
        in_specs=[pl.BlockSpec((rows, d), lambda l, j: (0, 0)),
                  pl.BlockSpec((1, d, tn), lambda l, j: (l, 0, j)),
                  pl.BlockSpec((1, 1, tn), lambda l, j: (l, 0, j))],
        out_specs=pl.BlockSpec((1, rows, tn), lambda l, j: (l, 0, j)),
        out_shape=jax.ShapeDtypeStruct((depth, rows, n), F32),
        name="mod_vectors", compiler_params=_params("parallel", "parallel"),
    )(cvec, w_mod, b_mod.reshape(depth, 1, n))


def _even_in_body(rope, n_tiles, x_ref, xp_ref, xn_ref, sc_ref, sh_ref, g_ref, w_ref, cw_ref,
                  alog_ref, dt_ref, *rest):
    if rope:
        cos_ref, sin_ref = rest[:2]
        rest = rest[2:]
    qa_ref, ka_ref, va_ref, qkv_ref, gate_ref, gb_ref, gbt_ref = rest
    i = pl.program_id(1)
    g, sc, sh = g_ref[...], sc_ref[0], sh_ref[0]
    subs = _sub_tiles(x_ref.shape[1])
    for n, r in enumerate(subs):
        first, last = n == 0, n == len(subs) - 1
        rs = r.stop - r.start
        h = _norm_mod(x_ref[0, r], g, sc, sh).astype(BF16)
        before = xp_ref[0] if first else x_ref[0, r.start - SUBLANES:r.start]
        after = xn_ref[0] if last else x_ref[0, r.stop:r.stop + SUBLANES]
        hh = _norm_mod(jnp.concatenate([before, after], axis=0), g, sc, sh).astype(BF16)
        yb = _dot(jnp.concatenate([h, hh], axis=0), w_ref[:, QKVB0:GATE0])
        ya = _dot(h, w_ref[:, :QKVB0])
        up = jnp.where(i > 0 if first else True, yb[rs + SUBLANES - 1:rs + SUBLANES], 0.0)
        dn = jnp.where(i < n_tiles - 1 if last else True, yb[rs + SUBLANES:rs + SUBLANES + 1], 0.0)
        s = _silu(_dwconv3(yb[:rs], up, dn, cw_ref[...]))
        for j in range(B_CONV_CH // LANES):
            seg = s[:, j * LANES:(j + 1) * LANES]
            if j < 2 * B_HEADS:
                seg = seg * lax.rsqrt(jnp.sum(seg * seg, axis=-1, keepdims=True) + EPS)
            qkv_ref[0, r, j * LANES:(j + 1) * LANES] = seg

        yc = _dot(h, w_ref[:, GATE0:])
        qa = ya[:, :A_Q]
        ka = ya[:, KA0:VA0]
        if rope:
            cos_t, sin_t = cos_ref[r, :], sin_ref[r, :]
            qa = jnp.concatenate([_rope(qa[:, j * LANES:(j + 1) * LANES], cos_t, sin_t)
                                  for j in range(A_Q // LANES)], axis=1)
            ka = _rope(ka, cos_t, sin_t)
        qa_ref[0, r] = qa
        ka_ref[0, r] = ka
        va_ref[0, r] = ya[:, VA0:QKVB0]
        gate_ref[0, r] = yc[:, :B_VW]

        ab = yc[:, B_VW:]
        lane = lax.broadcasted_iota(jnp.int32, ab.shape, 1)
        gdec = -jnp.exp(alog_ref[...]) * _softplus(ab + dt_ref[...])
        gb = jnp.where((lane & B_HEADS) == 0, gdec, _sigmoid(ab))
        gb_ref[0, r] = gb
        gbt = gb.T
        for c in range(rs // CHUNK):
            gbt_ref[0, r.start // CHUNK + c] = gbt[:4 * B_HEADS, c * CHUNK:(c + 1) * CHUNK]


def _even_in(x, modflat, mod_rows, row_of, layer, g_pre, w_in, conv_w, alog_vec, dt_vec, rope_tabs):
    bsz, seq, d = x.shape
    tm = _row_tile(seq, MAX_ROW_TILE)
    nt = seq // tm
    rope = rope_tabs is not None
    prev, nxt = _halo_specs(tm, seq)
    in_specs = [pl.BlockSpec((1, tm, d), lambda b, i: (b, i, 0)), prev, nxt,
                _mod_spec(mod_rows, layer, row_of, 1), _mod_spec(mod_rows, layer, row_of, 0),
                _const_spec((1, d)), _resident_spec(w_in.shape), _const_spec(conv_w.shape),
                _const_spec((1, LANES)), _const_spec((1, LANES))]
    args = [x, x, x, modflat, modflat, g_pre, w_in, conv_w, alog_vec, dt_vec]
    if rope:
        in_specs += [pl.BlockSpec((tm, LANES), lambda b, i: (i, 0))] * 2
        args += list(rope_tabs)

    def out(width, dtype=F32):
        return (pl.BlockSpec((1, tm, width), lambda b, i: (b, i, 0)),
                jax.ShapeDtypeStruct((bsz, seq, width), dtype))

    outs = [out(A_Q), out(A_KV), out(A_KV), out(B_CONV_CH), out(B_VW), out(LANES),
            (pl.BlockSpec((1, tm // CHUNK, 4 * B_HEADS, CHUNK), lambda b, i: (b, i, 0, 0)),
             jax.ShapeDtypeStruct((bsz, seq // CHUNK, 4 * B_HEADS, CHUNK), F32))]
    return pl.pallas_call(
        functools.partial(_even_in_body, rope, nt), grid=(bsz, nt), in_specs=in_specs,
        out_specs=[o[0] for o in outs], out_shape=[o[1] for o in outs],
        name="even_in", compiler_params=_params("parallel", "parallel"),
    )(*args)


def _even_attn_body(windowed, q_ref, k_ref, v_ref, *rest):
    if windowed:
        kc_ref, vc_ref, sink_ref, o_ref = rest
    else:
        sink_ref, o_ref = rest
    i = pl.program_id(1)
    seq = k_ref.shape[1]
    q_all = q_ref[0] * (A_HEAD_DIM ** -0.5)
    rows = A_REP * Q_BLOCK
    if windowed:
        band = Q_BLOCK + 2 * WINDOW
        k0 = pl.multiple_of(jnp.clip(i * Q_BLOCK - WINDOW, 0, seq - band), Q_BLOCK)
        k_all = k_ref[0, pl.ds(k0, band), :].astype(BF16)
        v_all = v_ref[0, pl.ds(k0, band), :].astype(BF16)
        kc_all = kc_ref[0].astype(BF16)
        vc_all = vc_ref[0].astype(BF16)
        qpos = i * Q_BLOCK + (lax.broadcasted_iota(jnp.int32, (rows, band), 0) & (Q_BLOCK - 1))
        kpos = k0 + lax.broadcasted_iota(jnp.int32, (rows, band), 1)
        valid = jnp.abs(qpos - kpos) <= WINDOW
    else:
        k_all = k_ref[0].astype(BF16)
        v_all = v_ref[0].astype(BF16)
    rid = lax.broadcasted_iota(jnp.int32, (rows, 1), 0) // Q_BLOCK
    groups = range(A_KV_HEADS)
    cols = [slice(g * A_HEAD_DIM, (g + 1) * A_HEAD_DIM) for g in groups]
    q, sink = [], []
    for g in groups:
        heads = range(g * A_REP, (g + 1) * A_REP)
        q.append(jnp.concatenate([q_all[:, h * A_HEAD_DIM:(h + 1) * A_HEAD_DIM] for h in heads],
                                 axis=0).astype(BF16))
        sk = jnp.zeros((rows, 1), F32)
        for r, h in enumerate(heads):
            sk = jnp.where(rid == r, sink_ref[h], sk)
        sink.append(sk)
    s = [_dot_nt(q[g], k_all[:, cols[g]]) for g in groups]
    if windowed:
        s = [jnp.where(valid, s[g], -jnp.inf) for g in groups]
        s_c = [_dot_nt(q[g], kc_all[:, cols[g]]) for g in groups]
        m = [jnp.maximum(jnp.max(s_c[g], axis=-1, keepdims=True), sink[g]) for g in groups]
    else:
        m = sink
    m = [jnp.maximum(jnp.max(s[g], axis=-1, keepdims=True), m[g]) for g in groups]

    def with_ones(v):
        lane = lax.broadcasted_iota(jnp.int32, (v.shape[0], LANES - A_HEAD_DIM), 1)
        return jnp.concatenate([v, jnp.where(lane == 0, 1.0, 0.0).astype(BF16)], axis=1)

    e = [jnp.exp((s[g] - m[g]).astype(BF16)) for g in groups]
    o = [_dot(e[g], with_ones(v_all[:, cols[g]])) for g in groups]
    if windowed:
        e_c = [jnp.exp((s_c[g] - m[g]).astype(BF16)) for g in groups]
        o = [o[g] + _dot(e_c[g], with_ones(vc_all[:, cols[g]])) for g in groups]
    outs = []
    for g in groups:
        den = o[g][:, A_HEAD_DIM:A_HEAD_DIM + 1] + jnp.exp(sink[g] - m[g])
        og = o[g][:, :A_HEAD_DIM] / den
        outs += [og[r * Q_BLOCK:(r + 1) * Q_BLOCK] for r in range(A_REP)]
    o_ref[0] = jnp.concatenate(outs, axis=1).astype(o_ref.dtype)


def _even_attn(qa, ka, va, ctx, sink):
    bsz, seq, _ = qa.shape
    windowed = ctx is not None
    in_specs = [pl.BlockSpec((1, Q_BLOCK, A_Q), lambda b, i: (b, i, 0)),
                pl.BlockSpec((1, seq, A_KV), lambda b, i: (b, 0, 0)),
                pl.BlockSpec((1, seq, A_KV), lambda b, i: (b, 0, 0))]
    args = [qa, ka, va]
    if windowed:
        past = ctx[0].shape[1]
        in_specs += [pl.BlockSpec((1, past, A_KV), lambda b, i: (b, 0, 0))] * 2
        args += list(ctx)
    in_specs.append(pl.BlockSpec(memory_space=pltpu.SMEM))
    args.append(sink)
    return pl.pallas_call(
        functools.partial(_even_attn_body, windowed), grid=(bsz, seq // Q_BLOCK), in_specs=in_specs,
        out_specs=pl.BlockSpec((1, Q_BLOCK, A_Q), lambda b, i: (b, i, 0)),
        out_shape=jax.ShapeDtypeStruct((bsz, seq, A_Q), BF16),
        name="even_attn", compiler_params=_params("parallel", "parallel"),
    )(*args)


def _bf16_terms(x, n):
    terms = []
    for _ in range(n):
        t = x.astype(BF16).astype(F32)
        terms.append(t)
        x = x - t
    return terms


def _dot_split(a, b):
    ah, al = _bf16_terms(a, 2)
    bh, bl = _bf16_terms(b, 2)
    return _dot(jnp.concatenate([ah, al, ah, al], axis=1).astype(BF16),
                jnp.concatenate([bh, bh, bl, bl], axis=0).astype(BF16))


def _mask_dot(mask, x, mask_first):
    terms = _bf16_terms(x, 3)
    m16 = mask.astype(BF16)
    if mask_first:
        return _dot(jnp.concatenate([m16] * 3, axis=1), jnp.concatenate(terms, axis=0).astype(BF16))
    return _dot(jnp.concatenate(terms, axis=1).astype(BF16), jnp.concatenate([m16] * 3, axis=0))


def _delta_chunk_terms(chains, eye, interleaved=()):
    interleaved = list(interleaved)
    n = range(len(chains))
    q, k, v, gcol, grow, bcol, incl, strict, g_last = zip(*chains)
    decay = [jnp.where(incl[i], jnp.exp(jnp.where(incl[i], gcol[i] - grow[i], 0.0)), 0.0) for i in n]
    qs = [q[i] * (B_DK ** -0.5) for i in n]
    kbeta = [k[i] * bcol[i] for i in n]
    both = [_dot_nt(jnp.concatenate([kbeta[i], qs[i]], axis=0).astype(BF16), k[i].astype(BF16)) for i in n]
    qk = [jnp.where(incl[i], both[i][CHUNK:] * decay[i], 0.0).astype(BF16) for i in n]
    p = [-jnp.where(strict[i], both[i][:CHUNK] * decay[i], 0.0) for i in n]
    t_inv = [eye + p[i] for i in n]
    for _ in range(int(math.log2(CHUNK)) - 1):
        if interleaved:
            interleaved.pop(0)()
        p = [_dot_split(p[i], p[i]) for i in n]
        t_inv = [t_inv[i] + _dot_split(t_inv[i], p[i]) for i in n]
    while interleaved:
        interleaved.pop(0)()
    eg = [jnp.exp(gcol[i]) for i in n]
    rhs = [jnp.concatenate([v[i] * bcol[i], kbeta[i] * eg[i]], axis=1).astype(BF16) for i in n]
    uw = [_dot(t_inv[i].astype(BF16), rhs[i]).astype(BF16) for i in n]
    kd = [(k[i] * jnp.exp(g_last[i] - gcol[i])).T.astype(BF16) for i in n]
    qk_uw = [_dot(qk[i], uw[i]) for i in n]
    cp = [_dot(kd[i], uw[i]) for i in n]
    out = []
    for i in n:
        q_eff = qs[i] * eg[i] - qk_uw[i][:, B_DV:]
        pq = jnp.concatenate([cp[i][:, B_DV:], q_eff], axis=0).astype(BF16)
        out.append((pq, cp[i][:, :B_DV], qk_uw[i][:, :B_DV], jnp.exp(g_last[i])))
    return out


def _delta_body(zero_init, qf_ref, qb_ref, gf_ref, gb_ref, gtf_ref, gtb_ref, *rest):
    if not zero_init:
        s0f_ref, s0b_ref = rest[:2]
        rest = rest[2:]
    of_ref, ob_ref, sfo_ref, sbo_ref, s_scr = rest
    n = pl.program_id(1)
    n_chunks = gtf_ref.shape[1]
    n_chains = 2 * B_HEADS

    @pl.when(n == 0)
    def _():
        if zero_init:
            s_scr[...] = jnp.zeros_like(s_scr)
        else:
            s_scr[:B_HEADS] = s0f_ref[0]
            s_scr[B_HEADS:] = s0b_ref[0]

    ii = lax.broadcasted_iota(jnp.int32, (CHUNK, CHUNK), 0)
    jj = lax.broadcasted_iota(jnp.int32, (CHUNK, CHUNK), 1)
    lower, upper = ii >= jj, ii <= jj
    strict_lower, strict_upper = ii > jj, ii < jj
    tri_l, tri_u = lower.astype(F32), upper.astype(F32)
    eye = (ii == jj).astype(F32)

    def chains_of(q_ref, g_ref, gt_ref, c, lane0, tri_col, tri_row, incl, strict, last):
        rows = slice(c * CHUNK, (c + 1) * CHUNK)
        g = g_ref[0, rows, :]
        gc, gc_t = _mask_dot(tri_col, g, True), _mask_dot(tri_row, gt_ref[0, c], False)
        chains = []
        for h in range(B_HEADS):
            cq = slice(h * B_DK, (h + 1) * B_DK)
            ck = slice(B_QK + h * B_DK, B_QK + (h + 1) * B_DK)
            cv = slice(2 * B_QK + h * B_DV, 2 * B_QK + (h + 1) * B_DV)
            lg, lb = lane0 + h, lane0 + B_HEADS + h
            gcol = gc[:, lg:lg + 1]
            chains.append((q_ref[0, rows, cq], q_ref[0, rows, ck], q_ref[0, rows, cv], gcol,
                           gc_t[lg:lg + 1, :], g[:, lb:lb + 1], incl, strict, gcol[last:last + 1]))
        return chains

    states = [s_scr[idx] for idx in range(n_chains)]

    def scan_step(terms, cf, cb):
        def step():
            outs = []
            for idx, (pq, cc, oi, dec) in enumerate(terms):
                s = states[idx]
                r = _dot(pq, s.astype(BF16))
                outs.append(r[B_DK:] + oi)
                states[idx] = dec * s + cc - r[:B_DK]
            of_ref[0, cf * CHUNK:(cf + 1) * CHUNK, :] = jnp.concatenate(outs[:B_HEADS], axis=1)
            ob_ref[0, cb * CHUNK:(cb + 1) * CHUNK, :] = jnp.concatenate(outs[B_HEADS:], axis=1)
        return step

    pending = []
    for p in range(n_chunks // DELTA_PREP_CHUNKS):
        order = [(DELTA_PREP_CHUNKS * p + k, n_chunks - 1 - DELTA_PREP_CHUNKS * p - k)
                 for k in range(DELTA_PREP_CHUNKS)]
        chains = []
        for cf, cb in order:
            chains += chains_of(qf_ref, gf_ref, gtf_ref, cf, 0, tri_l, tri_u, lower, strict_lower, CHUNK - 1)
            chains += chains_of(qb_ref, gb_ref, gtb_ref, cb, 2 * B_HEADS, tri_u, tri_l, upper, strict_upper, 0)
        terms = _delta_chunk_terms(chains, eye, pending)
        pending = [scan_step(terms[k * n_chains:(k + 1) * n_chains], cf, cb) for k, (cf, cb) in enumerate(order)]
    for step in pending:
        step()
    for idx in range(n_chains):
        s_scr[idx] = states[idx]

    @pl.when(n == pl.num_programs(1) - 1)
    def _():
        sfo_ref[0] = s_scr[:B_HEADS]
        sbo_ref[0] = s_scr[B_HEADS:]


def _delta(qkvb, gbv, gbt, states):
    bsz, seq, _ = qkvb.shape
    per = min(DELTA_STEP_CHUNKS, seq // CHUNK)
    tb = per * CHUNK
    nb = seq // tb
    zero_init = states is None
    n_chains = 2 * B_HEADS

    def fwd(*tail):
        return lambda b, n: (b, n) + tail

    def bwd(*tail):
        return lambda b, n: (b, nb - 1 - n) + tail

    st_spec = pl.BlockSpec((1, B_HEADS, B_DK, B_DV), lambda b, n: (b, 0, 0, 0))
    in_specs = [pl.BlockSpec((1, tb, B_CONV_CH), fwd(0)), pl.BlockSpec((1, tb, B_CONV_CH), bwd(0)),
                pl.BlockSpec((1, tb, LANES), fwd(0)), pl.BlockSpec((1, tb, LANES), bwd(0)),
                pl.BlockSpec((1, per, 4 * B_HEADS, CHUNK), fwd(0, 0)),
                pl.BlockSpec((1, per, 4 * B_HEADS, CHUNK), bwd(0, 0))]
    args = [qkvb, qkvb, gbv, gbv, gbt, gbt]
    if not zero_init:
        in_specs += [st_spec, st_spec]
        args += list(states)
    st_shape = jax.ShapeDtypeStruct((bsz, B_HEADS, B_DK, B_DV), F32)
    o_shape = jax.ShapeDtypeStruct((bsz, seq, B_VW), F32)
    return pl.pallas_call(
        functools.partial(_delta_body, zero_init), grid=(bsz, nb), in_specs=in_specs,
        out_specs=[pl.BlockSpec((1, tb, B_VW), fwd(0)), pl.BlockSpec((1, tb, B_VW), bwd(0)),
                   st_spec, st_spec],
        out_shape=[o_shape, o_shape, st_shape, st_shape],
        scratch_shapes=[pltpu.VMEM((n_chains, B_DK, B_DV), F32)],
        name="delta_rule", compiler_params=_params("parallel", "arbitrary"),
    )(*args)


def _sub_tiles(rows):
    step = min(rows, SUB_TILE)
    return [slice(lo, lo + step) for lo in range(0, rows, step)]


def _even_out_body(oa_ref, of_ref, ob_ref, gate_ref, x_ref, g1_ref, on_ref, w_ref, gp_ref, o_ref):
    on = on_ref[...]
    for r in _sub_tiles(x_ref.shape[1]):
        ob = of_ref[0, r] + ob_ref[0, r]
        segs = [_rms(ob[:, h * B_DV:(h + 1) * B_DV], on) for h in range(B_HEADS)]
        obg = (jnp.concatenate(segs, axis=1) * _silu(gate_ref[0, r])).astype(BF16)
        mix = _dot(jnp.concatenate([oa_ref[0, r], obg], axis=1), w_ref[...])
        o_ref[0, r] = x_ref[0, r] + g1_ref[0] * _rms(mix, gp_ref[...])


def _odd_out_body(a_ref, x_ref, g1_ref, w_ref, gp_ref, o_ref):
    for r in _sub_tiles(x_ref.shape[1]):
        mix = _dot(a_ref[0, r], w_ref[...])
        o_ref[0, r] = x_ref[0, r] + g1_ref[0] * _rms(mix, gp_ref[...])


def _mixer_out(acts, x, modflat, mod_rows, row_of, layer, w_out, g_post, out_norm=None):
    bsz, seq, d = x.shape
    tm = _row_tile(seq, MAX_ROW_TILE)

    def row_spec(width):
        return pl.BlockSpec((1, tm, width), lambda b, i: (b, i, 0))

    in_specs = [row_spec(a.shape[-1]) for a in acts] + [row_spec(d), _mod_spec(mod_rows, layer, row_of, 2)]
    args = list(acts) + [x, modflat]
    if out_norm is not None:
        in_specs.append(_const_spec(out_norm.shape))
        args.append(out_norm)
    in_specs += [_const_spec(w_out.shape), _const_spec((1, d))]
    args += [w_out, g_post]
    return pl.pallas_call(
        _even_out_body if out_norm is not None else _odd_out_body,
        grid=(bsz, seq // tm), in_specs=in_specs, out_specs=row_spec(d),
        out_shape=jax.ShapeDtypeStruct(x.shape, F32),
        name="mixer_out", compiler_params=_params("parallel", "parallel"),
    )(*args)


def _write_keys_values(kv, kpe, k_ref, v_ref):
    kpe16 = kpe.astype(BF16)
    lane = lax.broadcasted_iota(jnp.int32, (kv.shape[0], C_V_PAD - C_V), 1)
    ones_col = jnp.where(lane == 0, 1.0, 0.0).astype(BF16)
    for hd in range(C_HEADS):
        base = hd * C_QK_PAD
        k_ref[0, :, base:base + C_NOPE] = kv[:, hd * C_NOPE:(hd + 1) * C_NOPE].astype(BF16)
        k_ref[0, :, base + C_NOPE:base + C_QK_PAD] = kpe16
        vbase = hd * C_V_PAD
        vcol = C_HEADS * C_NOPE + hd * C_V
        v_ref[0, :, vbase:vbase + C_V] = kv[:, vcol:vcol + C_V].astype(BF16)
        v_ref[0, :, vbase + C_V:vbase + C_V_PAD] = ones_col


def _odd_in_body(rope, x_ref, sc_ref, sh_ref, g_ref, w_ref, qn_ref, kvn_ref, wq_ref, wkv_ref, *rest):
    if rope:
        cos_ref, sin_ref = rest[:2]
        rest = rest[2:]
    q_ref, ckv_ref, kpe_ref, kn_ref, vv_ref = rest
    h = _norm_mod(x_ref[0], g_ref[...], sc_ref[0], sh_ref[0]).astype(BF16)
    y = _dot(h, w_ref[...])
    cq = _rms(y[:, :C_Q_LORA], qn_ref[...]).astype(BF16)
    ckv = _rms(y[:, C_Q_LORA:C_Q_LORA + C_KV_LORA], kvn_ref[...])
    q = _dot(cq, wq_ref[...])
    kv = _dot(ckv.astype(BF16), wkv_ref[...])
    kpe = y[:, C_Q_LORA + C_KV_LORA:]
    scale = (C_NOPE + C_ROPE) ** -0.5
    if rope:
        cos_t, sin_t = cos_ref[...], sin_ref[...]
        kpe = _rope(kpe, cos_t, sin_t)
    for hd in range(C_HEADS):
        base = hd * C_QK_PAD
        q_ref[0, :, base:base + C_NOPE] = (q[:, base:base + C_NOPE] * scale).astype(BF16)
        qpe = q[:, base + C_NOPE:base + C_QK_PAD]
        if rope:
            qpe = _rope(qpe, cos_t, sin_t)
        q_ref[0, :, base + C_NOPE:base + C_QK_PAD] = (qpe * scale).astype(BF16)
    ckv_ref[0] = ckv
    kpe_ref[0] = kpe
    _write_keys_values(kv, kpe, kn_ref, vv_ref)


def _odd_in(x, modflat, mod_rows, row_of, layer, g_pre, w_in, q_norm, kv_norm, w_q, w_kv, rope_tabs):
    bsz, seq, d = x.shape
    tm = _row_tile(seq, SUB_TILE)
    rope = rope_tabs is not None
    in_specs = [pl.BlockSpec((1, tm, d), lambda b, i: (b, i, 0)),
                _mod_spec(mod_rows, layer, row_of, 1), _mod_spec(mod_rows, layer, row_of, 0),
                _const_spec((1, d)), _const_spec(w_in.shape), _const_spec(q_norm.shape),
                _const_spec(kv_norm.shape), _const_spec(w_q.shape), _const_spec(w_kv.shape)]
    args = [x, modflat, modflat, g_pre, w_in, q_norm, kv_norm, w_q, w_kv]
    if rope:
        in_specs += [pl.BlockSpec((tm, LANES), lambda b, i: (i, 0))] * 2
        args += list(rope_tabs)

    def out(width, dtype):
        return (pl.BlockSpec((1, tm, width), lambda b, i: (b, i, 0)),
                jax.ShapeDtypeStruct((bsz, seq, width), dtype))

    outs = [out(C_HEADS * C_QK_PAD, BF16), out(C_KV_LORA, F32), out(LANES, F32),
            out(C_HEADS * C_QK_PAD, BF16), out(C_HEADS * C_V_PAD, BF16)]
    return pl.pallas_call(
        functools.partial(_odd_in_body, rope), grid=(bsz, seq // tm), in_specs=in_specs,
        out_specs=[o[0] for o in outs], out_shape=[o[1] for o in outs],
        name="odd_in", compiler_params=_params("parallel", "parallel"),
    )(*args)


def _kv_up_body(ckv_ref, kpe_ref, wkv_ref, kn_ref, vv_ref):
    _write_keys_values(_dot(ckv_ref[0].astype(BF16), wkv_ref[...]), kpe_ref[0], kn_ref, vv_ref)


def _kv_up(ckv, kpe, w_kv):
    bsz, n, r = ckv.shape

    def rows(width):
        return pl.BlockSpec((1, n, width), lambda b: (b, 0, 0))

    return pl.pallas_call(
        _kv_up_body, grid=(bsz,),
        in_specs=[rows(r), rows(LANES), _const_spec(w_kv.shape)],
        out_specs=[rows(C_HEADS * C_QK_PAD), rows(C_HEADS * C_V_PAD)],
        out_shape=[jax.ShapeDtypeStruct((bsz, n, C_HEADS * C_QK_PAD), BF16),
                   jax.ShapeDtypeStruct((bsz, n, C_HEADS * C_V_PAD), BF16)],
        name="kv_up", compiler_params=_params("parallel"),
    )(ckv, kpe, w_kv)


def _mla_attn_body(has_ctx, q_ref, k_ref, v_ref, *rest):
    if has_ctx:
        kc_ref, vc_ref, o_ref = rest
        sources = [(kc_ref, vc_ref), (k_ref, v_ref)]
    else:
        (o_ref,) = rest
        sources = [(k_ref, v_ref)]
    q = q_ref[0]
    chunks = []
    for kr, vr in sources:
        n = kr.shape[1]
        step = min(MLA_KEY_CHUNK, n)
        chunks += [(kr, vr, lo, step) for lo in range(0, n, step)]

    def scores(j):
        kr, _, lo, step = chunks[j]
        return _dot_nt(q, kr[0, lo:lo + step, :])

    nxt = scores(0)
    for j, (_, vr, lo, step) in enumerate(chunks):
        s = nxt
        if j + 1 < len(chunks):
            nxt = scores(j + 1)
        top = jnp.max(s, axis=-1, keepdims=True)
        m_new = top if j == 0 else jnp.maximum(m, top)
        e = jnp.exp((s - m_new).astype(BF16))
        part = _dot(e, vr[0, lo:lo + step, :])
        o = part if j == 0 else jnp.exp(m - m_new) * o + part
        m = m_new
    o_ref[0] = (o[:, :C_V] / o[:, C_V:C_V + 1]).astype(o_ref.dtype)


def _mla_attn(q, kc, vv, ctx):
    bsz, seq, _ = q.shape
    tq = _row_tile(seq, MAX_ROW_TILE)
    has_ctx = ctx is not None

    def keys(n, width):
        return pl.BlockSpec((1, n, width), lambda b, h, i: (b, 0, h))

    in_specs = [pl.BlockSpec((1, tq, C_QK_PAD), lambda b, h, i: (b, i, h)),
                keys(seq, C_QK_PAD), keys(seq, C_V_PAD)]
    args = [q, kc, vv]
    if has_ctx:
        past = ctx[0].shape[1]
        in_specs += [keys(past, C_QK_PAD), keys(past, C_V_PAD)]
        args += list(ctx)
    return pl.pallas_call(
        functools.partial(_mla_attn_body, has_ctx), grid=(bsz, C_HEADS, seq // tq), in_specs=in_specs,
        out_specs=pl.BlockSpec((1, tq, C_V), lambda b, h, i: (b, i, h)),
        out_shape=jax.ShapeDtypeStruct((bsz, seq, C_HEADS * C_V), BF16),
        name="mla_attn", compiler_params=_params("parallel", "parallel", "parallel"),
    )(*args)


def _ffn_rows(x, xh, has_up, has_dn, g, sc, sh, g2, gp, wu_ref, cw_ref, wd_ref):
    rows = x.shape[0]
    h = jnp.concatenate([_norm_mod(x, g, sc, sh).astype(BF16), _norm_mod(xh, g, sc, sh).astype(BF16)], axis=0)

    def up_proj(f):
        lo = f * FF_TILE
        return _dot(h, wu_ref[:, lo:lo + FF_TILE]), _dot(h, wu_ref[:, D_FF + lo:D_FF + lo + FF_TILE])

    def conv(u, lo):
        up = jnp.where(has_up, u[rows + SUBLANES - 1:rows + SUBLANES], 0.0)
        dn = jnp.where(has_dn, u[rows + SUBLANES:rows + SUBLANES + 1], 0.0)
        return _dwconv3(u[:rows], up, dn, cw_ref[:, lo:lo + FF_TILE])

    nf = D_FF // FF_TILE
    acts = []
    nxt = up_proj(0)
    for f in range(nf):
        ua, ub = nxt
        if f + 1 < nf:
            nxt = up_proj(f + 1)
        lo = f * FF_TILE
        acts.append((_silu(conv(ua, lo)) * conv(ub, D_FF + lo)).astype(BF16))
    mix = _dot(jnp.concatenate(acts, axis=1), wd_ref[...])
    return x + g2 * _rms(mix, gp)


def _ffn_body(n_tiles, n_sub, x_ref, xp_ref, xn_ref, sc_ref, sh_ref, g2_ref, g_ref, gp_ref,
              wu_ref, cw_ref, wd_ref, o_ref):
    i = pl.program_id(1)
    rs = x_ref.shape[1] // n_sub
    for j in range(n_sub):
        lo, hi = j * rs, (j + 1) * rs
        before = xp_ref[0] if j == 0 else x_ref[0, lo - SUBLANES:lo]
        after = xn_ref[0] if j == n_sub - 1 else x_ref[0, hi:hi + SUBLANES]
        has_up = i > 0 if j == 0 else True
        has_dn = i < n_tiles - 1 if j == n_sub - 1 else True
        o_ref[0, lo:hi] = _ffn_rows(
            x_ref[0, lo:hi], jnp.concatenate([before, after], axis=0), has_up, has_dn,
            g_ref[...], sc_ref[0], sh_ref[0], g2_ref[0], gp_ref[...], wu_ref, cw_ref, wd_ref)


def _ffn(x, modflat, mod_rows, row_of, layer, g_pre, g_post, w_up, conv_w, w_down):
    bsz, seq, d = x.shape
    tm = _row_tile(seq, MAX_ROW_TILE)
    n_sub = max(tm // SUB_TILE, 1)
    nt = seq // tm
    prev, nxt = _halo_specs(tm, seq)
    in_specs = [pl.BlockSpec((1, tm, d), lambda b, i: (b, i, 0)), prev, nxt,
                _mod_spec(mod_rows, layer, row_of, 4), _mod_spec(mod_rows, layer, row_of, 3),
                _mod_spec(mod_rows, layer, row_of, 5), _const_spec((1, d)), _const_spec((1, d)),
                _resident_spec(w_up.shape), _const_spec(conv_w.shape), _resident_spec(w_down.shape)]
    return pl.pallas_call(
        functools.partial(_ffn_body, nt, n_sub), grid=(bsz, nt), in_specs=in_specs,
        out_specs=pl.BlockSpec((1, tm, d), lambda b, i: (b, i, 0)),
        out_shape=jax.ShapeDtypeStruct(x.shape, F32),
        name="conv_ffn", compiler_params=_params("parallel", "parallel"),
    )(x, x, x, modflat, modflat, modflat, g_pre, g_post, w_up, conv_w, w_down)


def _rope_tables(seq):
    t = jnp.arange(seq)
    row = (t // GRID_W).astype(F32)
    col = (t % GRID_W).astype(F32)
    n_freq = A_HEAD_DIM // 4
    inv_freq = ROPE_BASE ** (-jnp.arange(n_freq, dtype=F32) / n_freq)
    ang = jnp.concatenate([row[:, None] * inv_freq, col[:, None] * inv_freq], axis=-1)
    cos, sin = jnp.cos(ang), jnp.sin(ang)
    return (jnp.concatenate([cos, cos, cos, cos], axis=-1), jnp.concatenate([-sin, sin, -sin, sin], axis=-1))


def _lane_vec(p):
    v = jnp.zeros((2, 2, B_HEADS), F32).at[:, 0, :].set(p.astype(F32)).reshape(1, 4 * B_HEADS)
    return jnp.pad(v, ((0, 0), (0, LANES - 4 * B_HEADS)))


def _even_layer(x, layer, j, prm, mod, rope_tabs, ctx):
    modflat, mod_rows, row_of = mod
    w_in = jnp.pad(prm["ev_w_in"][j], ((0, 0), (0, EVEN_IN_PAD - prm["ev_w_in"].shape[-1]))).astype(BF16)
    qa, ka, va, qkvb, gate, gbv, gbt = _even_in(
        x, modflat, mod_rows, row_of, layer, prm["norm_pre"][layer, 0][None], w_in, prm["ev_conv"][j],
        _lane_vec(prm["ev_a_log"][j]), _lane_vec(prm["ev_dt_bias"][j]), rope_tabs)
    if ctx is None:
        oa = _even_attn(qa, ka, va, None, prm["ev_sink"][j])
        o_f, o_b, s_f, s_b = _delta(qkvb, gbv, gbt, None)
    else:
        k_ctx, v_ctx, s0_f, s0_b = ctx
        flat = lambda t: t.reshape(t.shape[0], t.shape[1], A_KV)
        oa = _even_attn(qa, ka, va, (flat(k_ctx), flat(v_ctx)), prm["ev_sink"][j])
        o_f, o_b, s_f, s_b = _delta(qkvb, gbv, gbt, (s0_f, s0_b))
    x = _mixer_out([oa, o_f, o_b, gate], x, modflat, mod_rows, row_of, layer,
                   prm["ev_w_out"][j].astype(BF16), prm["norm_post"][layer, 0][None],
                   prm["ev_out_norm"][j][None])
    return x, (ka, va, s_f, s_b)


def _odd_layer(x, layer, j, prm, mod, rope_tabs, ctx):
    modflat, mod_rows, row_of = mod
    w_in = jnp.pad(prm["od_w_in"][j], ((0, 0), (0, ODD_IN_PAD - prm["od_w_in"].shape[-1]))).astype(BF16)
    w_q = prm["od_w_q_up"][j].reshape(C_Q_LORA, C_HEADS, C_NOPE + C_ROPE)
    w_q = jnp.pad(w_q, ((0, 0), (0, 0), (0, C_QK_PAD - C_NOPE - C_ROPE)))
    w_q = w_q.reshape(C_Q_LORA, C_HEADS * C_QK_PAD).astype(BF16)
    w_kv = prm["od_w_kv_up"][j].reshape(C_KV_LORA, C_HEADS, 2, C_NOPE)
    w_kv = w_kv.transpose(0, 2, 1, 3).reshape(C_KV_LORA, 2 * C_HEADS * C_NOPE).astype(BF16)
    q, ckv, kpe, kn, vv = _odd_in(
        x, modflat, mod_rows, row_of, layer, prm["norm_pre"][layer, 0][None], w_in,
        prm["od_q_norm"][j][None], prm["od_kv_norm"][j][None], w_q, w_kv, rope_tabs)
    if ctx is None:
        o = _mla_attn(q, kn, vv, None)
    else:
        ckv_ctx, kpe_ctx = ctx
        kpe_c = jnp.pad(kpe_ctx, ((0, 0), (0, 0), (0, LANES - C_ROPE)))
        o = _mla_attn(q, kn, vv, _kv_up(ckv_ctx, kpe_c, w_kv))
    x = _mixer_out([o], x, modflat, mod_rows, row_of, layer, prm["od_w_out"][j].astype(BF16),
                   prm["norm_post"][layer, 0][None])
    return x, (ckv, kpe[..., :C_ROPE])


def _trunk(x, prm, mod, rope_tabs, ctxs):
    depth = prm["w_mod"].shape[0]
    modflat, mod_rows, row_of = mod
    new_ctx = []
    for layer in range(depth):
        j = layer // 2
        ctx = None if ctxs is None else ctxs[layer]
        if layer % 2 == 0:
            x, nc = _even_layer(x, layer, j, prm, mod, rope_tabs, ctx)
        else:
            x, nc = _odd_layer(x, layer, j, prm, mod, rope_tabs, ctx)
        new_ctx.append(nc)
        x = _ffn(x, modflat, mod_rows, row_of, layer, prm["norm_pre"][layer, 1][None],
                 prm["norm_post"][layer, 1][None], prm["ffn_w_up"][layer].astype(BF16),
                 prm["ffn_conv"][layer], prm["ffn_w_down"][layer].astype(BF16))
    return x, new_ctx


def kernel(x_prompt, x_sample, cache_attn_k, cache_attn_v, state_delta_fwd, state_delta_bwd,
           cache_mla_ckv, cache_mla_kpe, c, c_ctx, w_mod, b_mod, norm_pre, norm_post,
           ffn_w_up, ffn_conv, ffn_w_down, ev_w_in, ev_conv, ev_a_log, ev_dt_bias, ev_sink,
           ev_out_norm, ev_w_out, od_w_in, od_q_norm, od_kv_norm, od_w_q_up, od_w_kv_up, od_w_out):
    prm = {
        "w_mod": w_mod, "b_mod": b_mod, "norm_pre": norm_pre, "norm_post": norm_post,
        "ffn_w_up": ffn_w_up, "ffn_conv": ffn_conv, "ffn_w_down": ffn_w_down,
        "ev_w_in": ev_w_in, "ev_conv": ev_conv, "ev_a_log": ev_a_log, "ev_dt_bias": ev_dt_bias,
        "ev_sink": ev_sink, "ev_out_norm": ev_out_norm, "ev_w_out": ev_w_out,
        "od_w_in": od_w_in, "od_q_norm": od_q_norm, "od_kv_norm": od_kv_norm,
        "od_w_q_up": od_w_q_up, "od_w_kv_up": od_w_kv_up, "od_w_out": od_w_out,
    }
    depth = w_mod.shape[0]
    n_c = c.shape[0]
    mod_rows = -(-(n_c + 1) // SUBLANES) * SUBLANES
    cvec = jnp.concatenate([c, c_ctx[None], jnp.zeros((mod_rows - n_c - 1, c.shape[1]), F32)], axis=0)
    modflat = _mod_all(cvec, w_mod, b_mod).reshape(depth * mod_rows * N_MOD, 1, D_MODEL)

    y_prompt, pctx = _trunk(x_prompt, prm, (modflat, mod_rows, lambda b: n_c), None, None)

    ctxs = []
    for layer in range(depth):
        j = layer // 2
        if layer % 2 == 0:
            ctxs.append((cache_attn_k[:, j], cache_attn_v[:, j], state_delta_fwd[:, j], state_delta_bwd[:, j]))
        else:
            ctxs.append((cache_mla_ckv[:, j], cache_mla_kpe[:, j]))
    y_sample, _ = _trunk(x_sample, prm, (modflat, mod_rows, lambda b: b), _rope_tables(x_sample.shape[1]), ctxs)

    bsz, seq = x_prompt.shape[:2]
    even, odd = pctx[0::2], pctx[1::2]
    kv_shape = (bsz, len(even), seq, A_KV_HEADS, A_HEAD_DIM)
    new_attn_k = jnp.stack([e[0] for e in even], axis=1).reshape(kv_shape)
    new_attn_v = jnp.stack([e[1] for e in even], axis=1).reshape(kv_shape)
    new_delta_fwd = jnp.stack([e[2] for e in even], axis=1)
    new_delta_bwd = jnp.stack([e[3] for e in even], axis=1)
    new_mla_ckv = jnp.stack([o[0] for o in odd], axis=1)
    new_mla_kpe = jnp.stack([o[1] for o in odd], axis=1)
    return (y_prompt, y_sample, new_attn_k, new_attn_v, new_delta_fwd, new_delta_bwd, new_mla_ckv, new_mla_kpe)
```

```python
import functools
import math

import jax
import jax.numpy as jnp
from jax import lax
from jax.experimental import pallas as pl
from jax.experimental.pallas import tpu as pltpu

F32 = jnp.float32
BF16 = jnp.bfloat16

D_MODEL = 1024
GRID_W = 64
A_HEADS = 8
A_KV_HEADS = 2
A_HEAD_DIM = 64
A_REP = A_HEADS // A_KV_HEADS
WINDOW = 128
Q_BLOCK = 128
B_HEADS = 4
B_DK = 128
B_DV = 128
CHUNK = 64
C_HEADS = 8
C_Q_LORA = 384
C_KV_LORA = 256
C_NOPE = 128
C_ROPE = 64
C_V = 128
C_QK_PAD = 256
C_V_PAD = 256
D_FF = 2816
ROPE_BASE = 10000.0
EPS = 1e-6
N_MOD = 6

A_Q = A_HEADS * A_HEAD_DIM
A_KV = A_KV_HEADS * A_HEAD_DIM
B_QK = B_HEADS * B_DK
B_VW = B_HEADS * B_DV
B_CONV_CH = 2 * B_QK + B_VW
KA0 = A_Q
VA0 = KA0 + A_KV
QKVB0 = VA0 + A_KV
GATE0 = QKVB0 + B_CONV_CH
AB0 = GATE0 + B_VW
LANES = 128
SUBLANES = 8
EVEN_IN_PAD = AB0 + LANES
ODD_IN_PAD = C_Q_LORA + C_KV_LORA + LANES
FF_TILE = 256
MLA_KEY_CHUNK = 2048
MAX_ROW_TILE = 1024
SUB_TILE = 512
MIN_ROW_TILE = 256
DELTA_STEP_CHUNKS = 8
DELTA_PREP_CHUNKS = 2
VMEM_LIMIT_BYTES = 56 * 1024 * 1024


def _params(*sem):
    return pltpu.CompilerParams(dimension_semantics=sem, vmem_limit_bytes=VMEM_LIMIT_BYTES)


def _dot(a, b):
    return jnp.dot(a, b, preferred_element_type=F32)


def _dot_nt(a, b):
    return lax.dot_general(a, b, (((1,), (1,)), ((), ())), preferred_element_type=F32)


def _sigmoid(x):
    return 1.0 / (1.0 + jnp.exp(-x))


def _silu(x):
    return x * _sigmoid(x)


def _softplus(x):
    return jnp.maximum(x, 0.0) + jnp.log(1.0 + jnp.exp(-jnp.abs(x)))


def _rms(x, g):
    return x * lax.rsqrt(jnp.mean(x * x, axis=-1, keepdims=True) + EPS) * g


def _norm_mod(x, g, scale, shift):
    return _rms(x, g) * (1.0 + scale) + shift


def _rope(x, cos_t, sin_t):
    lane = lax.broadcasted_iota(jnp.int32, x.shape, 1)
    swapped = jnp.where((lane & 63) < 32, pltpu.roll(x, LANES - 32, 1), pltpu.roll(x, 32, 1))
    return x * cos_t + swapped * sin_t


def _dwconv3(u, up, dn, cw):
    tm = u.shape[0]
    row = lax.broadcasted_iota(jnp.int32, u.shape, 0)
    u_prev = jnp.where(row == 0, up, pltpu.roll(u, 1, 0))
    u_next = jnp.where(row == tm - 1, dn, pltpu.roll(u, tm - 1, 0))
    return cw[0:1] * u_prev + cw[1:2] * u + cw[2:3] * u_next


def _halo_specs(tm, seq):
    per = tm // SUBLANES
    last = seq // SUBLANES - 1
    prev = pl.BlockSpec((1, SUBLANES, D_MODEL), lambda b, i, *_: (b, jnp.maximum(i * per - 1, 0), 0))
    nxt = pl.BlockSpec((1, SUBLANES, D_MODEL), lambda b, i, *_: (b, jnp.minimum((i + 1) * per, last), 0))
    return prev, nxt


def _mod_spec(mod_rows, layer, row_of, k):
    return pl.BlockSpec((1, 1, D_MODEL),
                        lambda b, *_: ((layer * mod_rows + row_of(b)) * N_MOD + k, 0, 0))


def _row_tile(seq, cap):
    tile = cap
    while tile > MIN_ROW_TILE and seq % tile:
        tile //= 2
    assert seq % tile == 0, (seq, tile)
    return tile


def _const_spec(shape):
    nd = len(shape)
    return pl.BlockSpec(shape, lambda *_: (0,) * nd)


def _resident_spec(shape):
    nd = len(shape)
    return pl.BlockSpec(shape, lambda *_: (0,) * nd, pipeline_mode=pl.Buffered(1))


def _mod_body(c_ref, w_ref, b_ref, o_ref):
    s = _silu(c_ref[...]).astype(BF16)
    o_ref[0] = _dot(s, w_ref[0].astype(BF16)) + b_ref[0]


def _mod_all(cvec, w_mod, b_mod):
    depth, d, n = w_mod.shape
    rows = cvec.shape[0]
    tn = n // 4
    return pl.pallas_call(
        _mod_body, grid=(depth, n // tn),
        in_specs=[pl.BlockSpec((rows, d), lambda l, j: (0, 0)),
                  pl.BlockSpec((1, d, tn), lambda l, j: (l, 0, j)),
                  pl.BlockSpec((1, 1, tn), lambda l, j: (l, 0, j))],
        out_specs=pl.BlockSpec((1, rows, tn), lambda l, j: (l, 0, j)),
        out_shape=jax.ShapeDtypeStruct((depth, rows, n), F32),
        name="mod_vectors", compiler_params=_params("parallel", "parallel"),
    )(cvec, w_mod, b_mod.reshape(depth, 1, n))


def _even_in_body(rope, n_tiles, x_ref, xp_ref, xn_ref, sc_ref, sh_ref, g_ref, w_ref, cw_ref,
                  alog_ref, dt_ref, *rest):
    if rope:
        cos_ref, sin_ref = rest[:2]
        rest = rest[2:]
    qa_ref, ka_ref, va_ref, qkv_ref, gate_ref, gb_ref, gbt_ref = rest
    i = pl.program_id(1)
    g, sc, sh = g_ref[...], sc_ref[0], sh_ref[0]
    subs = _sub_tiles(x_ref.shape[1])
    for n, r in enumerate(subs):
        first, last = n == 0, n == len(subs) - 1
        rs = r.stop - r.start
        h = _norm_mod(x_ref[0, r], g, sc, sh).astype(BF16)
        before = xp_ref[0] if first else x_ref[0, r.start - SUBLANES:r.start]
        after = xn_ref[0] if last else x_ref[0, r.stop:r.stop + SUBLANES]
        hh = _norm_mod(jnp.concatenate([before, after], axis=0), g, sc, sh).astype(BF16)
        yb = _dot(jnp.concatenate([h, hh], axis=0), w_ref[:, QKVB0:GATE0])
        ya = _dot(h, w_ref[:, :QKVB0])
        up = jnp.where(i > 0 if first else True, yb[rs + SUBLANES - 1:rs + SUBLANES], 0.0)
        dn = jnp.where(i < n_tiles - 1 if last else True, yb[rs + SUBLANES:rs + SUBLANES + 1], 0.0)
        s = _silu(_dwconv3(yb[:rs], up, dn, cw_ref[...]))
        for j in range(B_CONV_CH // LANES):
            seg = s[:, j * LANES:(j + 1) * LANES]
            if j < 2 * B_HEADS:
                seg = seg * lax.rsqrt(jnp.sum(seg * seg, axis=-1, keepdims=True) + EPS)
            qkv_ref[0, r, j * LANES:(j + 1) * LANES] = seg

        yc = _dot(h, w_ref[:, GATE0:])
        qa = ya[:, :A_Q]
        ka = ya[:, KA0:VA0]
        if rope:
            cos_t, sin_t = cos_ref[r, :], sin_ref[r, :]
            qa = jnp.concatenate([_rope(qa[:, j * LANES:(j + 1) * LANES], cos_t, sin_t)
                                  for j in range(A_Q // LANES)], axis=1)
            ka = _rope(ka, cos_t, sin_t)
        qa_ref[0, r] = qa
        ka_ref[0, r] = ka
        va_ref[0, r] = ya[:, VA0:QKVB0]
        gate_ref[0, r] = yc[:, :B_VW]

        ab = yc[:, B_VW:]
        lane = lax.broadcasted_iota(jnp.int32, ab.shape, 1)
        gdec = -jnp.exp(alog_ref[...]) * _softplus(ab + dt_ref[...])
        gb = jnp.where((lane & B_HEADS) == 0, gdec, _sigmoid(ab))
        gb_ref[0, r] = gb
        gbt = gb.T
        for c in range(rs // CHUNK):
            gbt_ref[0, r.start // CHUNK + c] = gbt[:4 * B_HEADS, c * CHUNK:(c + 1) * CHUNK]


def _even_in(x, modflat, mod_rows, row_of, layer, g_pre, w_in, conv_w, alog_vec, dt_vec, rope_tabs):
    bsz, seq, d = x.shape
    tm = _row_tile(seq, MAX_ROW_TILE)
    nt = seq // tm
    rope = rope_tabs is not None
    prev, nxt = _halo_specs(tm, seq)
    in_specs = [pl.BlockSpec((1, tm, d), lambda b, i: (b, i, 0)), prev, nxt,
                _mod_spec(mod_rows, layer, row_of, 1), _mod_spec(mod_rows, layer, row_of, 0),
                _const_spec((1, d)), _resident_spec(w_in.shape), _const_spec(conv_w.shape),
                _const_spec((1, LANES)), _const_spec((1, LANES))]
    args = [x, x, x, modflat, modflat, g_pre, w_in, conv_w, alog_vec, dt_vec]
    if rope:
        in_specs += [pl.BlockSpec((tm, LANES), lambda b, i: (i, 0))] * 2
        args += list(rope_tabs)

    def out(width, dtype=F32):
        return (pl.BlockSpec((1, tm, width), lambda b, i: (b, i, 0)),
                jax.ShapeDtypeStruct((bsz, seq, width), dtype))

    outs = [out(A_Q), out(A_KV), out(A_KV), out(B_CONV_CH), out(B_VW), out(LANES),
            (pl.BlockSpec((1, tm // CHUNK, 4 * B_HEADS, CHUNK), lambda b, i: (b, i, 0, 0)),
             jax.ShapeDtypeStruct((bsz, seq // CHUNK, 4 * B_HEADS, CHUNK), F32))]
    return pl.pallas_call(
        functools.partial(_even_in_body, rope, nt), grid=(bsz, nt), in_specs=in_specs,
        out_specs=[o[0] for o in outs], out_shape=[o[1] for o in outs],
        name="even_in", compiler_params=_params("parallel", "parallel"),
    )(*args)


def _even_attn_body(windowed, q_ref, k_ref, v_ref, *rest):
    if windowed:
        kc_ref, vc_ref, sink_ref, o_ref = rest
    else:
        sink_ref, o_ref = rest
    i = pl.program_id(1)
    seq = k_ref.shape[1]
    q_all = q_ref[0] * (A_HEAD_DIM ** -0.5)
    rows = A_REP * Q_BLOCK
    if windowed:
        band = Q_BLOCK + 2 * WINDOW
        k0 = pl.multiple_of(jnp.clip(i * Q_BLOCK - WINDOW, 0, seq - band), Q_BLOCK)
        k_all = k_ref[0, pl.ds(k0, band), :].astype(BF16)
        v_all = v_ref[0, pl.ds(k0, band), :].astype(BF16)
        kc_all = kc_ref[0].astype(BF16)
        vc_all = vc_ref[0].astype(BF16)
        qpos = i * Q_BLOCK + (lax.broadcasted_iota(jnp.int32, (rows, band), 0) & (Q_BLOCK - 1))
        kpos = k0 + lax.broadcasted_iota(jnp.int32, (rows, band), 1)
        valid = jnp.abs(qpos - kpos) <= WINDOW
    else:
        k_all = k_ref[0].astype(BF16)
        v_all = v_ref[0].astype(BF16)
    rid = lax.broadcasted_iota(jnp.int32, (rows, 1), 0) // Q_BLOCK
    groups = range(A_KV_HEADS)
    cols = [slice(g * A_HEAD_DIM, (g + 1) * A_HEAD_DIM) for g in groups]
    q, sink = [], []
    for g in groups:
        heads = range(g * A_REP, (g + 1) * A_REP)
        q.append(jnp.concatenate([q_all[:, h * A_HEAD_DIM:(h + 1) * A_HEAD_DIM] for h in heads],
                                 axis=0).astype(BF16))
        sk = jnp.zeros((rows, 1), F32)
        for r, h in enumerate(heads):
            sk = jnp.where(rid == r, sink_ref[h], sk)
        sink.append(sk)
    s = [_dot_nt(q[g], k_all[:, cols[g]]) for g in groups]
    if windowed:
        s = [jnp.where(valid, s[g], -jnp.inf) for g in groups]
        s_c = [_dot_nt(q[g], kc_all[:, cols[g]]) for g in groups]
        m = [jnp.maximum(jnp.max(s_c[g], axis=-1, keepdims=True), sink[g]) for g in groups]
    else:
        m = sink
    m = [jnp.maximum(jnp.max(s[g], axis=-1, keepdims=True), m[g]) for g in groups]

    def with_ones(v):
        lane = lax.broadcasted_iota(jnp.int32, (v.shape[0], LANES - A_HEAD_DIM), 1)
        return jnp.concatenate([v, jnp.where(lane == 0, 1.0, 0.0).astype(BF16)], axis=1)

    e = [jnp.exp((s[g] - m[g]).astype(BF16)) for g in groups]
    o = [_dot(e[g], with_ones(v_all[:, cols[g]])) for g in groups]
    if windowed:
        e_c = [jnp.exp((s_c[g] - m[g]).astype(BF16)) for g in groups]
        o = [o[g] + _dot(e_c[g], with_ones(vc_all[:, cols[g]])) for g in groups]
    outs = []
    for g in groups:
        den = o[g][:, A_HEAD_DIM:A_HEAD_DIM + 1] + jnp.exp(sink[g] - m[g])
        og = o[g][:, :A_HEAD_DIM] / den
        outs += [og[r * Q_BLOCK:(r + 1) * Q_BLOCK] for r in range(A_REP)]
    o_ref[0] = jnp.concatenate(outs, axis=1).astype(o_ref.dtype)


def _even_attn(qa, ka, va, ctx, sink):
    bsz, seq, _ = qa.shape
    windowed = ctx is not None
    in_specs = [pl.BlockSpec((1, Q_BLOCK, A_Q), lambda b, i: (b, i, 0)),
                pl.BlockSpec((1, seq, A_KV), lambda b, i: (b, 0, 0)),
                pl.BlockSpec((1, seq, A_KV), lambda b, i: (b, 0, 0))]
    args = [qa, ka, va]
    if windowed:
        past = ctx[0].shape[1]
        in_specs += [pl.BlockSpec((1, past, A_KV), lambda b, i: (b, 0, 0))] * 2
        args += list(ctx)
    in_specs.append(pl.BlockSpec(memory_space=pltpu.SMEM))
    args.append(sink)
    return pl.pallas_call(
        functools.partial(_even_attn_body, windowed), grid=(bsz, seq // Q_BLOCK), in_specs=in_specs,
        out_specs=pl.BlockSpec((1, Q_BLOCK, A_Q), lambda b, i: (b, i, 0)),
        out_shape=jax.ShapeDtypeStruct((bsz, seq, A_Q), BF16),
        name="even_attn", compiler_params=_params("parallel", "parallel"),
    )(*args)


def _bf16_terms(x, n):
    terms = []
    for _ in range(n):
        t = x.astype(BF16).astype(F32)
        terms.append(t)
        x = x - t
    return terms


def _dot_split(a, b):
    ah, al = _bf16_terms(a, 2)
    bh, bl = _bf16_terms(b, 2)
    return _dot(jnp.concatenate([ah, al, ah, al], axis=1).astype(BF16),
                jnp.concatenate([bh, bh, bl, bl], axis=0).astype(BF16))


def _mask_dot(mask, x, mask_first):
    terms = _bf16_terms(x, 3)
    m16 = mask.astype(BF16)
    if mask_first:
        return _dot(jnp.concatenate([m16] * 3, axis=1), jnp.concatenate(terms, axis=0).astype(BF16))
    return _dot(jnp.concatenate(terms, axis=1).astype(BF16), jnp.concatenate([m16] * 3, axis=0))


def _delta_chunk_terms(chains, eye, interleaved=()):
    interleaved = list(interleaved)
    n = range(len(chains))
    q, k, v, gcol, grow, bcol, incl, strict, g_last = zip(*chains)
    decay = [jnp.where(incl[i], jnp.exp(jnp.where(incl[i], gcol[i] - grow[i], 0.0)), 0.0) for i in n]
    qs = [q[i] * (B_DK ** -0.5) for i in n]
    kbeta = [k[i] * bcol[i] for i in n]
    both = [_dot_nt(jnp.concatenate([kbeta[i], qs[i]], axis=0).astype(BF16), k[i].astype(BF16)) for i in n]
    qk = [jnp.where(incl[i], both[i][CHUNK:] * decay[i], 0.0).astype(BF16) for i in n]
    p = [-jnp.where(strict[i], both[i][:CHUNK] * decay[i], 0.0) for i in n]
    t_inv = [eye + p[i] for i in n]
    for _ in range(int(math.log2(CHUNK)) - 1):
        if interleaved:
            interleaved.pop(0)()
        p = [_dot_split(p[i], p[i]) for i in n]
        t_inv = [t_inv[i] + _dot_split(t_inv[i], p[i]) for i in n]
    while interleaved:
        interleaved.pop(0)()
    eg = [jnp.exp(gcol[i]) for i in n]
    rhs = [jnp.concatenate([v[i] * bcol[i], kbeta[i] * eg[i]], axis=1).astype(BF16) for i in n]
    uw = [_dot(t_inv[i].astype(BF16), rhs[i]).astype(BF16) for i in n]
    kd = [(k[i] * jnp.exp(g_last[i] - gcol[i])).T.astype(BF16) for i in n]
    qk_uw = [_dot(qk[i], uw[i]) for i in n]
    cp = [_dot(kd[i], uw[i]) for i in n]
    out = []
    for i in n:
        q_eff = qs[i] * eg[i] - qk_uw[i][:, B_DV:]
        pq = jnp.concatenate([cp[i][:, B_DV:], q_eff], axis=0).astype(BF16)
        out.append((pq, cp[i][:, :B_DV], qk_uw[i][:, :B_DV], jnp.exp(g_last[i])))
    return out


def _delta_body(zero_init, qf_ref, qb_ref, gf_ref, gb_ref, gtf_ref, gtb_ref, *rest):
    if not zero_init:
        s0f_ref, s0b_ref = rest[:2]
        rest = rest[2:]
    of_ref, ob_ref, sfo_ref, sbo_ref, s_scr = rest
    n = pl.program_id(1)
    n_chunks = gtf_ref.shape[1]
    n_chains = 2 * B_HEADS

    @pl.when(n == 0)
    def _():
        if zero_init:
            s_scr[...] = jnp.zeros_like(s_scr)
        else:
            s_scr[:B_HEADS] = s0f_ref[0]
            s_scr[B_HEADS:] = s0b_ref[0]

    ii = lax.broadcasted_iota(jnp.int32, (CHUNK, CHUNK), 0)
    jj = lax.broadcasted_iota(jnp.int32, (CHUNK, CHUNK), 1)
    lower, upper = ii >= jj, ii <= jj
    strict_lower, strict_upper = ii > jj, ii < jj
    tri_l, tri_u = lower.astype(F32), upper.astype(F32)
    eye = (ii == jj).astype(F32)

    def chains_of(q_ref, g_ref, gt_ref, c, lane0, tri_col, tri_row, incl, strict, last):
        rows = slice(c * CHUNK, (c + 1) * CHUNK)
        g = g_ref[0, rows, :]
        gc, gc_t = _mask_dot(tri_col, g, True), _mask_dot(tri_row, gt_ref[0, c], False)
        chains = []
        for h in range(B_HEADS):
            cq = slice(h * B_DK, (h + 1) * B_DK)
            ck = slice(B_QK + h * B_DK, B_QK + (h + 1) * B_DK)
            cv = slice(2 * B_QK + h * B_DV, 2 * B_QK + (h + 1) * B_DV)
            lg, lb = lane0 + h, lane0 + B_HEADS + h
            gcol = gc[:, lg:lg + 1]
            chains.append((q_ref[0, rows, cq], q_ref[0, rows, ck], q_ref[0, rows, cv], gcol,
                           gc_t[lg:lg + 1, :], g[:, lb:lb + 1], incl, strict, gcol[last:last + 1]))
        return chains

    states = [s_scr[idx] for idx in range(n_chains)]

    def scan_step(terms, cf, cb):
        def step():
            outs = []
            for idx, (pq, cc, oi, dec) in enumerate(terms):
                s = states[idx]
                r = _dot(pq, s.astype(BF16))
                outs.append(r[B_DK:] + oi)
                states[idx] = dec * s + cc - r[:B_DK]
            of_ref[0, cf * CHUNK:(cf + 1) * CHUNK, :] = jnp.concatenate(outs[:B_HEADS], axis=1)
            ob_ref[0, cb * CHUNK:(cb + 1) * CHUNK, :] = jnp.concatenate(outs[B_HEADS:], axis=1)
        return step

    pending = []
    for p in range(n_chunks // DELTA_PREP_CHUNKS):
        order = [(DELTA_PREP_CHUNKS * p + k, n_chunks - 1 - DELTA_PREP_CHUNKS * p - k)
                 for k in range(DELTA_PREP_CHUNKS)]
        chains = []
        for cf, cb in order:
            chains += chains_of(qf_ref, gf_ref, gtf_ref, cf, 0, tri_l, tri_u, lower, strict_lower, CHUNK - 1)
            chains += chains_of(qb_ref, gb_ref, gtb_ref, cb, 2 * B_HEADS, tri_u, tri_l, upper, strict_upper, 0)
        terms = _delta_chunk_terms(chains, eye, pending)
        pending = [scan_step(terms[k * n_chains:(k + 1) * n_chains], cf, cb) for k, (cf, cb) in enumerate(order)]
    for step in pending:
        step()
    for idx in range(n_chains):
        s_scr[idx] = states[idx]

    @pl.when(n == pl.num_programs(1) - 1)
    def _():
        sfo_ref[0] = s_scr[:B_HEADS]
        sbo_ref[0] = s_scr[B_HEADS:]


def _delta(qkvb, gbv, gbt, states):
    bsz, seq, _ = qkvb.shape
    per = min(DELTA_STEP_CHUNKS, seq // CHUNK)
    tb = per * CHUNK
    nb = seq // tb
    zero_init = states is None
    n_chains = 2 * B_HEADS

    def fwd(*tail):
        return lambda b, n: (b, n) + tail

    def bwd(*tail):
        return lambda b, n: (b, nb - 1 - n) + tail

    st_spec = pl.BlockSpec((1, B_HEADS, B_DK, B_DV), lambda b, n: (b, 0, 0, 0))
    in_specs = [pl.BlockSpec((1, tb, B_CONV_CH), fwd(0)), pl.BlockSpec((1, tb, B_CONV_CH), bwd(0)),
                pl.BlockSpec((1, tb, LANES), fwd(0)), pl.BlockSpec((1, tb, LANES), bwd(0)),
                pl.BlockSpec((1, per, 4 * B_HEADS, CHUNK), fwd(0, 0)),
                pl.BlockSpec((1, per, 4 * B_HEADS, CHUNK), bwd(0, 0))]
    args = [qkvb, qkvb, gbv, gbv, gbt, gbt]
    if not zero_init:
        in_specs += [st_spec, st_spec]
        args += list(states)
    st_shape = jax.ShapeDtypeStruct((bsz, B_HEADS, B_DK, B_DV), F32)
    o_shape = jax.ShapeDtypeStruct((bsz, seq, B_VW), F32)
    return pl.pallas_call(
        functools.partial(_delta_body, zero_init), grid=(bsz, nb), in_specs=in_specs,
        out_specs=[pl.BlockSpec((1, tb, B_VW), fwd(0)), pl.BlockSpec((1, tb, B_VW), bwd(0)),
                   st_spec, st_spec],
        out_shape=[o_shape, o_shape, st_shape, st_shape],
        scratch_shapes=[pltpu.VMEM((n_chains, B_DK, B_DV), F32)],
        name="delta_rule", compiler_params=_params("parallel", "arbitrary"),
    )(*args)


def _sub_tiles(rows):
    step = min(rows, SUB_TILE)
    return [slice(lo, lo + step) for lo in range(0, rows, step)]


def _even_out_body(oa_ref, of_ref, ob_ref, gate_ref, x_ref, g1_ref, on_ref, w_ref, gp_ref, o_ref):
    on = on_ref[...]
    for r in _sub_tiles(x_ref.shape[1]):
        ob = of_ref[0, r] + ob_ref[0, r]
        segs = [_rms(ob[:, h * B_DV:(h + 1) * B_DV], on) for h in range(B_HEADS)]
        obg = (jnp.concatenate(segs, axis=1) * _silu(gate_ref[0, r])).astype(BF16)
        mix = _dot(jnp.concatenate([oa_ref[0, r], obg], axis=1), w_ref[...])
        o_ref[0, r] = x_ref[0, r] + g1_ref[0] * _rms(mix, gp_ref[...])


def _odd_out_body(a_ref, x_ref, g1_ref, w_ref, gp_ref, o_ref):
    for r in _sub_tiles(x_ref.shape[1]):
        mix = _dot(a_ref[0, r], w_ref[...])
        o_ref[0, r] = x_ref[0, r] + g1_ref[0] * _rms(mix, gp_ref[...])


def _mixer_out(acts, x, modflat, mod_rows, row_of, layer, w_out, g_post, out_norm=None):
    bsz, seq, d = x.shape
    tm = _row_tile(seq, MAX_ROW_TILE)

    def row_spec(width):
        return pl.BlockSpec((1, tm, width), lambda b, i: (b, i, 0))

    in_specs = [row_spec(a.shape[-1]) for a in acts] + [row_spec(d), _mod_spec(mod_rows, layer, row_of, 2)]
    args = list(acts) + [x, modflat]
    if out_norm is not None:
        in_specs.append(_const_spec(out_norm.shape))
        args.append(out_norm)
    in_specs += [_const_spec(w_out.shape), _const_spec((1, d))]
    args += [w_out, g_post]
    return pl.pallas_call(
        _even_out_body if out_norm is not None else _odd_out_body,
        grid=(bsz, seq // tm), in_specs=in_specs, out_specs=row_spec(d),
        out_shape=jax.ShapeDtypeStruct(x.shape, F32),
        name="mixer_out", compiler_params=_params("parallel", "parallel"),
    )(*args)


def _write_keys_values(kv, kpe, k_ref, v_ref):
    kpe16 = kpe.astype(BF16)
    lane = lax.broadcasted_iota(jnp.int32, (kv.shape[0], C_V_PAD - C_V), 1)
    ones_col = jnp.where(lane == 0, 1.0, 0.0).astype(BF16)
    for hd in range(C_HEADS):
        base = hd * C_QK_PAD
        k_ref[0, :, base:base + C_NOPE] = kv[:, hd * C_NOPE:(hd + 1) * C_NOPE].astype(BF16)
        k_ref[0, :, base + C_NOPE:base + C_QK_PAD] = kpe16
        vbase = hd * C_V_PAD
        vcol = C_HEADS * C_NOPE + hd * C_V
        v_ref[0, :, vbase:vbase + C_V] = kv[:, vcol:vcol + C_V].astype(BF16)
        v_ref[0, :, vbase + C_V:vbase + C_V_PAD] = ones_col


def _odd_in_body(rope, x_ref, sc_ref, sh_ref, g_ref, w_ref, qn_ref, kvn_ref, wq_ref, wkv_ref, *rest):
    if rope:
        cos_ref, sin_ref = rest[:2]
        rest = rest[2:]
    q_ref, ckv_ref, kpe_ref, kn_ref, vv_ref = rest
    h = _norm_mod(x_ref[0], g_ref[...], sc_ref[0], sh_ref[0]).astype(BF16)
    y = _dot(h, w_ref[...])
    cq = _rms(y[:, :C_Q_LORA], qn_ref[...]).astype(BF16)
    ckv = _rms(y[:, C_Q_LORA:C_Q_LORA + C_KV_LORA], kvn_ref[...])
    q = _dot(cq, wq_ref[...])
    kv = _dot(ckv.astype(BF16), wkv_ref[...])
    kpe = y[:, C_Q_LORA + C_KV_LORA:]
    scale = (C_NOPE + C_ROPE) ** -0.5
    if rope:
        cos_t, sin_t = cos_ref[...], sin_ref[...]
        kpe = _rope(kpe, cos_t, sin_t)
    for hd in range(C_HEADS):
        base = hd * C_QK_PAD
        q_ref[0, :, base:base + C_NOPE] = (q[:, base:base + C_NOPE] * scale).astype(BF16)
        qpe = q[:, base + C_NOPE:base + C_QK_PAD]
        if rope:
            qpe = _rope(qpe, cos_t, sin_t)
        q_ref[0, :, base + C_NOPE:base + C_QK_PAD] = (qpe * scale).astype(BF16)
    ckv_ref[0] = ckv
    kpe_ref[0] = kpe
    _write_keys_values(kv, kpe, kn_ref, vv_ref)


def _odd_in(x, modflat, mod_rows, row_of, layer, g_pre, w_in, q_norm, kv_norm, w_q, w_kv, rope_tabs):
    bsz, seq, d = x.shape
    tm = _row_tile(seq, SUB_TILE)
    rope = rope_tabs is not None
    in_specs = [pl.BlockSpec((1, tm, d), lambda b, i: (b, i, 0)),
                _mod_spec(mod_rows, layer, row_of, 1), _mod_spec(mod_rows, layer, row_of, 0),
                _const_spec((1, d)), _const_spec(w_in.shape), _const_spec(q_norm.shape),
                _const_spec(kv_norm.shape), _const_spec(w_q.shape), _const_spec(w_kv.shape)]
    args = [x, modflat, modflat, g_pre, w_in, q_norm, kv_norm, w_q, w_kv]
    if rope:
        in_specs += [pl.BlockSpec((tm, LANES), lambda b, i: (i, 0))] * 2
        args += list(rope_tabs)

    def out(width, dtype):
        return (pl.BlockSpec((1, tm, width), lambda b, i: (b, i, 0)),
                jax.ShapeDtypeStruct((bsz, seq, width), dtype))

    outs = [out(C_HEADS * C_QK_PAD, BF16), out(C_KV_LORA, F32), out(LANES, F32),
            out(C_HEADS * C_QK_PAD, BF16), out(C_HEADS * C_V_PAD, BF16)]
    return pl.pallas_call(
        functools.partial(_odd_in_body, rope), grid=(bsz, seq // tm), in_specs=in_specs,
        out_specs=[o[0] for o in outs], out_shape=[o[1] for o in outs],
        name="odd_in", compiler_params=_params("parallel", "parallel"),
    )(*args)


def _kv_up_body(ckv_ref, kpe_ref, wkv_ref, kn_ref, vv_ref):
    _write_keys_values(_dot(ckv_ref[0].astype(BF16), wkv_ref[...]), kpe_ref[0], kn_ref, vv_ref)


def _kv_up(ckv, kpe, w_kv):
    bsz, n, r = ckv.shape

    def rows(width):
        return pl.BlockSpec((1, n, width), lambda b: (b, 0, 0))

    return pl.pallas_call(
        _kv_up_body, grid=(bsz,),
        in_specs=[rows(r), rows(LANES), _const_spec(w_kv.shape)],
        out_specs=[rows(C_HEADS * C_QK_PAD), rows(C_HEADS * C_V_PAD)],
        out_shape=[jax.ShapeDtypeStruct((bsz, n, C_HEADS * C_QK_PAD), BF16),
                   jax.ShapeDtypeStruct((bsz, n, C_HEADS * C_V_PAD), BF16)],
        name="kv_up", compiler_params=_params("parallel"),
    )(ckv, kpe, w_kv)


def _mla_attn_body(has_ctx, q_ref, k_ref, v_ref, *rest):
    if has_ctx:
        kc_ref, vc_ref, o_ref = rest
        sources = [(kc_ref, vc_ref), (k_ref, v_ref)]
    else:
        (o_ref,) = rest
        sources = [(k_ref, v_ref)]
    q = q_ref[0]
    chunks = []
    for kr, vr in sources:
        n = kr.shape[1]
        step = min(MLA_KEY_CHUNK, n)
        chunks += [(kr, vr, lo, step) for lo in range(0, n, step)]

    def scores(j):
        kr, _, lo, step = chunks[j]
        return _dot_nt(q, kr[0, lo:lo + step, :])

    nxt = scores(0)
    for j, (_, vr, lo, step) in enumerate(chunks):
        s = nxt
        if j + 1 < len(chunks):
            nxt = scores(j + 1)
        top = jnp.max(s, axis=-1, keepdims=True)
        m_new = top if j == 0 else jnp.maximum(m, top)
        e = jnp.exp((s - m_new).astype(BF16))
        part = _dot(e, vr[0, lo:lo + step, :])
        o = part if j == 0 else jnp.exp(m - m_new) * o + part
        m = m_new
    o_ref[0] = (o[:, :C_V] / o[:, C_V:C_V + 1]).astype(o_ref.dtype)


def _mla_attn(q, kc, vv, ctx):
    bsz, seq, _ = q.shape
    tq = _row_tile(seq, MAX_ROW_TILE)
    has_ctx = ctx is not None

    def keys(n, width):
        return pl.BlockSpec((1, n, width), lambda b, h, i: (b, 0, h))

    in_specs = [pl.BlockSpec((1, tq, C_QK_PAD), lambda b, h, i: (b, i, h)),
                keys(seq, C_QK_PAD), keys(seq, C_V_PAD)]
    args = [q, kc, vv]
    if has_ctx:
        past = ctx[0].shape[1]
        in_specs += [keys(past, C_QK_PAD), keys(past, C_V_PAD)]
        args += list(ctx)
    return pl.pallas_call(
        functools.partial(_mla_attn_body, has_ctx), grid=(bsz, C_HEADS, seq // tq), in_specs=in_specs,
        out_specs=pl.BlockSpec((1, tq, C_V), lambda b, h, i: (b, i, h)),
        out_shape=jax.ShapeDtypeStruct((bsz, seq, C_HEADS * C_V), BF16),
        name="mla_attn", compiler_params=_params("parallel", "parallel", "parallel"),
    )(*args)


def _ffn_rows(x, xh, has_up, has_dn, g, sc, sh, g2, gp, wu_ref, cw_ref, wd_ref):
    rows = x.shape[0]
    h = jnp.concatenate([_norm_mod(x, g, sc, sh).astype(BF16), _norm_mod(xh, g, sc, sh).astype(BF16)], axis=0)

    def up_proj(f):
        lo = f * FF_TILE
        return _dot(h, wu_ref[:, lo:lo + FF_TILE]), _dot(h, wu_ref[:, D_FF + lo:D_FF + lo + FF_TILE])

    def conv(u, lo):
        up = jnp.where(has_up, u[rows + SUBLANES - 1:rows + SUBLANES], 0.0)
        dn = jnp.where(has_dn, u[rows + SUBLANES:rows + SUBLANES + 1], 0.0)
        return _dwconv3(u[:rows], up, dn, cw_ref[:, lo:lo + FF_TILE])

    nf = D_FF // FF_TILE
    acts = []
    nxt = up_proj(0)
    for f in range(nf):
        ua, ub = nxt
        if f + 1 < nf:
            nxt = up_proj(f + 1)
        lo = f * FF_TILE
        acts.append((_silu(conv(ua, lo)) * conv(ub, D_FF + lo)).astype(BF16))
    mix = _dot(jnp.concatenate(acts, axis=1), wd_ref[...])
    return x + g2 * _rms(mix, gp)


def _ffn_body(n_tiles, n_sub, x_ref, xp_ref, xn_ref, sc_ref, sh_ref, g2_ref, g_ref, gp_ref,
              wu_ref, cw_ref, wd_ref, o_ref):
    i = pl.program_id(1)
    rs = x_ref.shape[1] // n_sub
    for j in range(n_sub):
        lo, hi = j * rs, (j + 1) * rs
        before = xp_ref[0] if j == 0 else x_ref[0, lo - SUBLANES:lo]
        after = xn_ref[0] if j == n_sub - 1 else x_ref[0, hi:hi + SUBLANES]
        has_up = i > 0 if j == 0 else True
        has_dn = i < n_tiles - 1 if j == n_sub - 1 else True
        o_ref[0, lo:hi] = _ffn_rows(
            x_ref[0, lo:hi], jnp.concatenate([before, after], axis=0), has_up, has_dn,
            g_ref[...], sc_ref[0], sh_ref[0], g2_ref[0], gp_ref[...], wu_ref, cw_ref, wd_ref)


def _ffn(x, modflat, mod_rows, row_of, layer, g_pre, g_post, w_up, conv_w, w_down):
    bsz, seq, d = x.shape
    tm = _row_tile(seq, MAX_ROW_TILE)
    n_sub = max(tm // SUB_TILE, 1)
    nt = seq // tm
    prev, nxt = _halo_specs(tm, seq)
    in_specs = [pl.BlockSpec((1, tm, d), lambda b, i: (b, i, 0)), prev, nxt,
                _mod_spec(mod_rows, layer, row_of, 4), _mod_spec(mod_rows, layer, row_of, 3),
                _mod_spec(mod_rows, layer, row_of, 5), _const_spec((1, d)), _const_spec((1, d)),
                _resident_spec(w_up.shape), _const_spec(conv_w.shape), _resident_spec(w_down.shape)]
    return pl.pallas_call(
        functools.partial(_ffn_body, nt, n_sub), grid=(bsz, nt), in_specs=in_specs,
        out_specs=pl.BlockSpec((1, tm, d), lambda b, i: (b, i, 0)),
        out_shape=jax.ShapeDtypeStruct(x.shape, F32),
        name="conv_ffn", compiler_params=_params("parallel", "parallel"),
    )(x, x, x, modflat, modflat, modflat, g_pre, g_post, w_up, conv_w, w_down)


def _rope_tables(seq):
    t = jnp.arange(seq)
    row = (t // GRID_W).astype(F32)
    col = (t % GRID_W).astype(F32)
    n_freq = A_HEAD_DIM // 4
    inv_freq = ROPE_BASE ** (-jnp.arange(n_freq, dtype=F32) / n_freq)
    ang = jnp.concatenate([row[:, None] * inv_freq, col[:, None] * inv_freq], axis=-1)
    cos, sin = jnp.cos(ang), jnp.sin(ang)
    return (jnp.concatenate([cos, cos, cos, cos], axis=-1), jnp.concatenate([-sin, sin, -sin, sin], axis=-1))


def _lane_vec(p):
    v = jnp.zeros((2, 2, B_HEADS), F32).at[:, 0, :].set(p.astype(F32)).reshape(1, 4 * B_HEADS)
    return jnp.pad(v, ((0, 0), (0, LANES - 4 * B_HEADS)))


def _even_layer(x, layer, j, prm, mod, rope_tabs, ctx):
    modflat, mod_rows, row_of = mod
    w_in = jnp.pad(prm["ev_w_in"][j], ((0, 0), (0, EVEN_IN_PAD - prm["ev_w_in"].shape[-1]))).astype(BF16)
    qa, ka, va, qkvb, gate, gbv, gbt = _even_in(
        x, modflat, mod_rows, row_of, layer, prm["norm_pre"][layer, 0][None], w_in, prm["ev_conv"][j],
        _lane_vec(prm["ev_a_log"][j]), _lane_vec(prm["ev_dt_bias"][j]), rope_tabs)
    if ctx is None:
        oa = _even_attn(qa, ka, va, None, prm["ev_sink"][j])
        o_f, o_b, s_f, s_b = _delta(qkvb, gbv, gbt, None)
    else:
        k_ctx, v_ctx, s0_f, s0_b = ctx
        flat = lambda t: t.reshape(t.shape[0], t.shape[1], A_KV)
        oa = _even_attn(qa, ka, va, (flat(k_ctx), flat(v_ctx)), prm["ev_sink"][j])
        o_f, o_b, s_f, s_b = _delta(qkvb, gbv, gbt, (s0_f, s0_b))
    x = _mixer_out([oa, o_f, o_b, gate], x, modflat, mod_rows, row_of, layer,
                   prm["ev_w_out"][j].astype(BF16), prm["norm_post"][layer, 0][None],
                   prm["ev_out_norm"][j][None])
    return x, (ka, va, s_f, s_b)


def _odd_layer(x, layer, j, prm, mod, rope_tabs, ctx):
    modflat, mod_rows, row_of = mod
    w_in = jnp.pad(prm["od_w_in"][j], ((0, 0), (0, ODD_IN_PAD - prm["od_w_in"].shape[-1]))).astype(BF16)
    w_q = prm["od_w_q_up"][j].reshape(C_Q_LORA, C_HEADS, C_NOPE + C_ROPE)
    w_q = jnp.pad(w_q, ((0, 0), (0, 0), (0, C_QK_PAD - C_NOPE - C_ROPE)))
    w_q = w_q.reshape(C_Q_LORA, C_HEADS * C_QK_PAD).astype(BF16)
    w_kv = prm["od_w_kv_up"][j].reshape(C_KV_LORA, C_HEADS, 2, C_NOPE)
    w_kv = w_kv.transpose(0, 2, 1, 3).reshape(C_KV_LORA, 2 * C_HEADS * C_NOPE).astype(BF16)
    q, ckv, kpe, kn, vv = _odd_in(
        x, modflat, mod_rows, row_of, layer, prm["norm_pre"][layer, 0][None], w_in,
        prm["od_q_norm"][j][None], prm["od_kv_norm"][j][None], w_q, w_kv, rope_tabs)
    if ctx is None:
        o = _mla_attn(q, kn, vv, None)
    else:
        ckv_ctx, kpe_ctx = ctx
        kpe_c = jnp.pad(kpe_ctx, ((0, 0), (0, 0), (0, LANES - C_ROPE)))
        o = _mla_attn(q, kn, vv, _kv_up(ckv_ctx, kpe_c, w_kv))
    x = _mixer_out([o], x, modflat, mod_rows, row_of, layer, prm["od_w_out"][j].astype(BF16),
                   prm["norm_post"][layer, 0][None])
    return x, (ckv, kpe[..., :C_ROPE])


def _trunk(x, prm, mod, rope_tabs, ctxs):
    depth = prm["w_mod"].shape[0]
    modflat, mod_rows, row_of = mod
    new_ctx = []
    for layer in range(depth):
        j = layer // 2
        ctx = None if ctxs is None else ctxs[layer]
        if layer % 2 == 0:
            x, nc = _even_layer(x, layer, j, prm, mod, rope_tabs, ctx)
        else:
            x, nc = _odd_layer(x, layer, j, prm, mod, rope_tabs, ctx)
        new_ctx.append(nc)
        x = _ffn(x, modflat, mod_rows, row_of, layer, prm["norm_pre"][layer, 1][None],
                 prm["norm_post"][layer, 1][None], prm["ffn_w_up"][layer].astype(BF16),
                 prm["ffn_conv"][layer], prm["ffn_w_down"][layer].astype(BF16))
    return x, new_ctx


def kernel(x_prompt, x_sample, cache_attn_k, cache_attn_v, state_delta_fwd, state_delta_bwd,
           cache_mla_ckv, cache_mla_kpe, c, c_ctx, w_mod, b_mod, norm_pre, norm_post,
           ffn_w_up, ffn_conv, ffn_w_down, ev_w_in, ev_conv, ev_a_log, ev_dt_bias, ev_sink,
           ev_out_norm, ev_w_out, od_w_in, od_q_norm, od_kv_norm, od_w_q_up, od_w_kv_up, od_w_out):
    prm = {
        "w_mod": w_mod, "b_mod": b_mod, "norm_pre": norm_pre, "norm_post": norm_post,
        "ffn_w_up": ffn_w_up, "ffn_conv": ffn_conv, "ffn_w_down": ffn_w_down,
        "ev_w_in": ev_w_in, "ev_conv": ev_conv, "ev_a_log": ev_a_log, "ev_dt_bias": ev_dt_bias,
        "ev_sink": ev_sink, "ev_out_norm": ev_out_norm, "ev_w_out": ev_w_out,
        "od_w_in": od_w_in, "od_q_norm": od_q_norm, "od_kv_norm": od_kv_norm,
        "od_w_q_up": od_w_q_up, "od_w_kv_up": od_w_kv_up, "od_w_out": od_w_out,
    }
    depth = w_mod.shape[0]
    n_c = c.shape[0]
    mod_rows = -(-(n_c + 1) // SUBLANES) * SUBLANES
    cvec = jnp.concatenate([c, c_ctx[None], jnp.zeros((mod_rows - n_c - 1, c.shape[1]), F32)], axis=0)
    modflat = _mod_all(cvec, w_mod, b_mod).reshape(depth * mod_rows * N_MOD, 1, D_MODEL)

    y_prompt, pctx = _trunk(x_prompt, prm, (modflat, mod_rows, lambda b: n_c), None, None)

    ctxs = []
    for layer in range(depth):
        j = layer // 2
        if layer % 2 == 0:
            ctxs.append((cache_attn_k[:, j], cache_attn_v[:, j], state_delta_fwd[:, j], state_delta_bwd[:, j]))
        else:
            ctxs.append((cache_mla_ckv[:, j], cache_mla_kpe[:, j]))
    y_sample, _ = _trunk(x_sample, prm, (modflat, mod_rows, lambda b: b), _rope_tables(x_sample.shape[1]), ctxs)

    bsz, seq = x_prompt.shape[:2]
    even, odd = pctx[0::2], pctx[1::2]
    kv_shape = (bsz, len(even), seq, A_KV_HEADS, A_HEAD_DIM)
    new_attn_k = jnp.stack([e[0] for e in even], axis=1).reshape(kv_shape)
    new_attn_v = jnp.stack([e[1] for e in even], axis=1).reshape(kv_shape)
    new_delta_fwd = jnp.stack([e[2] for e in even], axis=1)
    new_delta_bwd = jnp.stack([e[3] for e in even], axis=1)
    new_mla_ckv = jnp.stack([o[0] for o in odd], axis=1)
    new_mla_kpe = jnp.stack([o[1] for o in odd], axis=1)
    return (y_prompt, y_sample, new_attn_k, new_attn_v, new_delta_fwd, new_delta_bwd, new_mla_ckv, new_mla_kpe)
```
